```python
import math, functools
import numpy as np
import jax
import jax.numpy as jnp
from jax import lax

D_MODEL = 1024
BATCH = 8
SEQ = 2048
DEPTH = 4
DEC_BATCH = 128
DEC_SEQ = 1
PAST_LEN = 2048
PAGE_SIZE = 128

N_HEADS = 8
N_KV_HEADS = 2
HEAD_DIM = 64
GROUP = N_HEADS // N_KV_HEADS
Q_W = N_HEADS * HEAD_DIM
KV_W = N_KV_HEADS * HEAD_DIM
CMP_BLOCK = 32
CMP_STRIDE = 16
CMP_HIDDEN = 128
SEL_BLOCK = 64
N_SEL = 8
WINDOW = 512
Q_BLOCK = 128
ROPE_THETA = 10000.0
SSM_WIDTH = D_MODEL // 2
SSM_GROUP = 16
SSM_GROUPS = SSM_WIDTH // SSM_GROUP
SSM_STATE = 64
SSM_CHUNK = 128
D_FF = 11 * D_MODEL // 4
CONV_W = 3
PLE_DIM = 256
EPS = 1e-6
NEG_INF = -1e30
FORCE_SCORE = 1e9
SCALE = HEAD_DIM ** -0.5
SPLIT_SIZES = (Q_W, KV_W, KV_W, KV_W, KV_W, KV_W, KV_W, 3 * N_HEADS, SSM_WIDTH, D_MODEL, D_MODEL)
N_IN = sum(SPLIT_SIZES)

kernel_name = 'nsa_s5_hybrid_decode_step'


def rmsnorm(x, g):
    xf = x.astype(jnp.float32)
    y = xf * lax.rsqrt(jnp.mean(xf * xf, axis=-1, keepdims=True) + EPS)
    return (y * g.astype(jnp.float32)).astype(x.dtype)


def rope(x, pos):
    half = HEAD_DIM // 2
    inv = jnp.float32(ROPE_THETA) ** (-jnp.arange(half, dtype=jnp.float32) / half)
    ang = pos.astype(jnp.float32)[:, None] * inv[None, :]
    cos = jnp.cos(ang)[:, None, :]
    sin = jnp.sin(ang)[:, None, :]
    xf = x.astype(jnp.float32)
    x1, x2 = xf[..., :half], xf[..., half:]
    return jnp.concatenate([x1 * cos - x2 * sin, x2 * cos + x1 * sin], axis=-1).astype(x.dtype)


def masked_softmax(s, mask):
    s = jnp.where(mask, s.astype(jnp.float32), NEG_INF)
    m = jnp.max(s, axis=-1, keepdims=True)
    e = jnp.where(mask, jnp.exp(s - m), 0.0)
    return e / jnp.maximum(jnp.sum(e, axis=-1, keepdims=True), 1e-30)


def attend(q, k, v, mask):
    s = jnp.einsum('btkgd,bskd->bkgts', q, k) * SCALE
    p = masked_softmax(s, mask)
    o = jnp.einsum('bkgts,bskd->btkgd', p.astype(v.dtype), v)
    return o, p


def window_mask(q_pos, k_pos):
    d = q_pos[:, None] - k_pos[None, :]
    return (d >= 0) & (d < WINDOW) & (k_pos[None, :] >= 0)


def compress(x, w1, pe, w2):
    B, L, K, D = x.shape
    xt = x.transpose(0, 2, 1, 3).reshape(B * K, L, D)
    hdn = lax.conv_general_dilated(xt, w1.astype(xt.dtype), (CMP_STRIDE,), 'VALID',
                                   dimension_numbers=('NWC', 'WIO', 'NWC'))
    hdn = hdn + jnp.einsum('ld,ldf->f', pe, w1)
    out = jax.nn.gelu(hdn) @ w2
    n = out.shape[1]
    return out.reshape(B, K, n, D).transpose(0, 2, 1, 3)


def cmp_branch(q, kc_full, vc_full, q_pos, wk1, pek, wk2, wv1, pev, wv2):
    kcmp = compress(kc_full, wk1, pek, wk2)
    vcmp = compress(vc_full, wv1, pev, wv2)
    n_cmp = kcmp.shape[1]
    start = jnp.arange(n_cmp) * CMP_STRIDE
    end = start + CMP_BLOCK - 1
    kcmp = rope(kcmp, end)
    o, p = attend(q, kcmp, vcmp, end[None, :] <= q_pos[:, None])
    L = kc_full.shape[1]
    n_sb = -(-L // SEL_BLOCK)
    sb_start = jnp.arange(n_sb) * SEL_BLOCK
    overlap = ((start[:, None] < sb_start[None, :] + SEL_BLOCK) & (end[:, None] >= sb_start[None, :])).astype(jnp.float32)
    imp = jnp.einsum('bkgtn,nj->btkj', p, overlap)
    blk = jnp.arange(n_sb)
    forced = (blk[None, :] == 0) | (blk[None, :] == (q_pos // SEL_BLOCK)[:, None])
    causal = sb_start[None, :] <= q_pos[:, None]
    imp = jnp.where(forced[None, :, None, :], FORCE_SCORE, imp)
    imp = jnp.where(causal[None, :, None, :], imp, NEG_INF)
    top_v, top_i = lax.top_k(imp, min(N_SEL, n_sb))
    return o, top_i, top_v > 0.5 * NEG_INF


def sel_attend(q, idx, valid, q_pos, fetch):
    pos = idx[..., None] * SEL_BLOCK + jnp.arange(SEL_BLOCK)
    k, v = fetch(pos)
    B, Tc, K, N, S = pos.shape
    mask = (valid[..., None] & (pos <= q_pos[None, :, None, None, None])).reshape(B, Tc, K, N * S)
    k = k.reshape(B, Tc, K, N * S, HEAD_DIM)
    v = v.reshape(B, Tc, K, N * S, HEAD_DIM)
    s = jnp.einsum('btkgd,btksd->btkgs', q, k) * SCALE
    p = masked_softmax(s, mask[:, :, :, None, :])
    return jnp.einsum('btkgs,btksd->btkgd', p.astype(v.dtype), v)


def to_chunks(x, c):
    B, T = x.shape[:2]
    return x.reshape((B, T // c, c) + x.shape[2:]).swapaxes(0, 1)


def from_chunks(x):
    nc, B, c = x.shape[:3]
    return x.swapaxes(0, 1).reshape((B, nc * c) + x.shape[3:])


def batch_head_index(B):
    bi = jnp.arange(B)[:, None, None, None, None]
    hd = jnp.arange(N_KV_HEADS)[None, None, :, None, None]
    return bi, hd


def nsa_prompt(q, kc, vc, ks, vs, kw, vw, cmpw):
    B, T = q.shape[:2]
    pos = jnp.arange(T)
    o_cmp, idx, valid = cmp_branch(q, kc, vc, pos, *cmpw)
    bi, hd = batch_head_index(B)

    def fetch(p):
        pc = jnp.clip(p, 0, T - 1)
        return ks[bi, pc, hd], vs[bi, pc, hd]

    qb = Q_BLOCK if T % Q_BLOCK == 0 else T
    o_sel = from_chunks(lax.map(lambda a: sel_attend(a[0], a[1], a[2], a[3], fetch),
                                (to_chunks(q, qb), to_chunks(idx, qb), to_chunks(valid, qb), pos.reshape(-1, qb))))
    kp = jnp.pad(kw, ((0, 0), (WINDOW, 0), (0, 0), (0, 0)))
    vp = jnp.pad(vw, ((0, 0), (WINDOW, 0), (0, 0), (0, 0)))

    def win_block(a):
        c, qc = a
        start = c * qb
        kb = lax.dynamic_slice_in_dim(kp, start, WINDOW + qb, axis=1)
        vb = lax.dynamic_slice_in_dim(vp, start, WINDOW + qb, axis=1)
        qpos = start + jnp.arange(qb)
        kpos = start - WINDOW + jnp.arange(WINDOW + qb)
        return attend(qc, kb, vb, window_mask(qpos, kpos))[0]

    o_win = from_chunks(lax.map(win_block, (jnp.arange(T // qb), to_chunks(q, qb))))
    n_keep = min(WINDOW, T)
    rows = (jnp.stack([kc, vc], axis=2), jnp.stack([ks, vs], axis=2), jnp.stack([kw, vw], axis=2)[:, T - n_keep:])
    return o_cmp, o_sel, o_win, rows


def nsa_sample(q, kc, vc, ks, vs, kw, vw, cmpw, cmp_pool, sel_pool, win_buf, page_table):
    B, T = q.shape[:2]
    past = page_table.shape[1] * PAGE_SIZE
    pos = past + jnp.arange(T)
    past_cmp = cmp_pool[page_table].reshape((B, past) + cmp_pool.shape[2:])
    kc_full = jnp.concatenate([past_cmp[:, :, 0].astype(kc.dtype), kc], axis=1)
    vc_full = jnp.concatenate([past_cmp[:, :, 1].astype(vc.dtype), vc], axis=1)
    o_cmp, idx, valid = cmp_branch(q, kc_full, vc_full, pos, *cmpw)
    sel_rows = sel_pool.reshape((-1,) + sel_pool.shape[2:])
    bi, hd = batch_head_index(B)

    def fetch(p):
        pc = jnp.clip(p, 0, past - 1)
        phys = page_table[bi, pc // PAGE_SIZE] * PAGE_SIZE + pc % PAGE_SIZE
        pn = jnp.clip(p - past, 0, T - 1)
        in_past = (p < past)[..., None]
        k = jnp.where(in_past, sel_rows[phys, 0, hd].astype(ks.dtype), ks[bi, pn, hd])
        v = jnp.where(in_past, sel_rows[phys, 1, hd].astype(vs.dtype), vs[bi, pn, hd])
        return k, v

    o_sel = sel_attend(q, idx, valid, pos, fetch)
    wb = win_buf.shape[1]
    win_all = jnp.concatenate([win_buf.astype(kw.dtype), jnp.stack([kw, vw], axis=2)], axis=1)
    kpos = past - wb + jnp.arange(wb + T)
    o_win, _ = attend(q, win_all[:, :, 0], win_all[:, :, 1], window_mask(pos, kpos))
    rows = (jnp.stack([kc, vc], axis=2), jnp.stack([ks, vs], axis=2), win_all[:, T:])
    return o_cmp, o_sel, o_win, rows


def cmul_combine(e1, e2):
    ar1, ai1, br1, bi1 = e1
    ar2, ai2, br2, bi2 = e2
    return (ar2 * ar1 - ai2 * ai1, ar2 * ai1 + ai2 * ar1,
            ar2 * br1 - ai2 * bi1 + br2, ar2 * bi1 + ai2 * br1 + bi2)


def ssm_scan(u, h0_re, h0_im, a_re, a_im, log_dt, b_re, b_im, c_re, c_im, d_skip):
    B, T, _ = u.shape
    uf = u.astype(jnp.float32).reshape(B, T, SSM_GROUPS, SSM_GROUP)
    ar = a_re.astype(jnp.float32)
    ai = a_im.astype(jnp.float32)
    dt = jnp.exp(log_dt.astype(jnp.float32))[:, None]
    mag = jnp.exp(dt * ar)
    ab_re = mag * jnp.cos(dt * ai)
    ab_im = mag * jnp.sin(dt * ai)
    den = ar * ar + ai * ai
    zr = ((ab_re - 1.0) * ar + ab_im * ai) / den
    zi = (ab_im * ar - (ab_re - 1.0) * ai) / den
    br = b_re.astype(jnp.float32)
    bim = b_im.astype(jnp.float32)
    bb_re = zr[..., None] * br - zi[..., None] * bim
    bb_im = zr[..., None] * bim + zi[..., None] * br
    bu_re = jnp.einsum('gpc,btgc->tbgp', bb_re, uf)
    bu_im = jnp.einsum('gpc,btgc->tbgp', bb_im, uf)
    chunk = SSM_CHUNK if T % SSM_CHUNK == 0 else T
    nc = T // chunk
    shp = (chunk, B, SSM_GROUPS, SSM_STATE)
    bu_re = bu_re.reshape((nc,) + shp)
    bu_im = bu_im.reshape((nc,) + shp)
    a_re_c = jnp.broadcast_to(ab_re, shp)
    a_im_c = jnp.broadcast_to(ab_im, shp)
    cr = c_re.astype(jnp.float32)
    ci = c_im.astype(jnp.float32)

    def step(carry, xs):
        hr0, hi0 = carry
        xr, xi = xs
        pr, pim, sr, si = lax.associative_scan(cmul_combine, (a_re_c, a_im_c, xr, xi), axis=0)
        hr = sr + pr * hr0 - pim * hi0
        hi = si + pr * hi0 + pim * hr0
        y = jnp.einsum('gcp,tbgp->tbgc', cr, hr) - jnp.einsum('gcp,tbgp->tbgc', ci, hi)
        return (hr[-1], hi[-1]), y

    (hr, hi), y = lax.scan(step, (h0_re.astype(jnp.float32), h0_im.astype(jnp.float32)), (bu_re, bu_im))
    y = y.reshape(T, B, SSM_GROUPS, SSM_GROUP).transpose(1, 0, 2, 3) + d_skip.astype(jnp.float32) * uf
    return y.reshape(B, T, SSM_WIDTH).astype(u.dtype), hr.astype(h0_re.dtype), hi.astype(h0_im.dtype)


def block(h, p_l, pos, lw, nsa_fn, h0_re, h0_im, conv_prefix):
    (g_attn, w_in, g_q, g_kc, g_ks, g_kw, wk1, pek, wk2, wv1, pev, wv2,
     a_re, a_im, log_dt, b_re, b_im, c_re, c_im, d_skip,
     w_a, w_glu1, w_glu2, w_o, g_ffn, w_up, conv_w, conv_b, w_down,
     g_ple, w_ple_gate, w_ple) = lw
    B, T, _ = h.shape
    xn = rmsnorm(h, g_attn)
    z = xn @ w_in
    pts = [int(s) for s in np.cumsum(SPLIT_SIZES)[:-1]]
    q, kc, vc, ks, vs, kw, vw, gl, u, ga, gb = jnp.split(z, pts, axis=-1)

    def heads(t):
        return t.reshape(B, T, -1, HEAD_DIM)

    q = rope(rmsnorm(heads(q), g_q), pos).reshape(B, T, N_KV_HEADS, GROUP, HEAD_DIM)
    kc = rmsnorm(heads(kc), g_kc)
    ks = rope(rmsnorm(heads(ks), g_ks), pos)
    kw = rope(rmsnorm(heads(kw), g_kw), pos)
    vc, vs, vw = heads(vc), heads(vs), heads(vw)
    o_cmp, o_sel, o_win, rows = nsa_fn(q, kc, vc, ks, vs, kw, vw, (wk1, pek, wk2, wv1, pev, wv2))
    bg = jax.nn.sigmoid(gl).reshape(B, T, N_KV_HEADS, GROUP, 3)
    o = bg[..., 0:1] * o_cmp + bg[..., 1:2] * o_sel + bg[..., 2:3] * o_win
    a_out = o.reshape(B, T, Q_W) @ w_a
    y, hr, hi = ssm_scan(u, h0_re, h0_im, a_re, a_im, log_dt, b_re, b_im, c_re, c_im, d_skip)
    yg = jax.nn.gelu(y)
    b_out = (yg @ w_glu1) * jax.nn.sigmoid(yg @ w_glu2)
    h = h + (jax.nn.sigmoid(ga) * a_out + jax.nn.sigmoid(gb) * b_out) @ w_o
    up = rmsnorm(h, g_ffn) @ w_up
    gp, val = jnp.split(up, 2, axis=-1)
    ext = jnp.concatenate([conv_prefix.astype(gp.dtype), gp], axis=1)
    conv = conv_b
    for j in range(CONV_W):
        conv = conv + conv_w[j] * ext[:, j:j + T]
    h = h + (jax.nn.gelu(conv) * val) @ w_down
    h = h + jax.nn.sigmoid(rmsnorm(h, g_ple) @ w_ple_gate) * (p_l @ w_ple)
    return h, rows, hr, hi, ext[:, T:]


def setup_inputs(seed: int = 0) -> dict:
    key = jax.random.key(seed)
    ks = jax.random.split(key, 48)
    f32 = jnp.float32
    n_pages = PAST_LEN // PAGE_SIZE
    n_pool = (DEC_BATCH * n_pages * 5) // 4
    win_buf = min(WINDOW, PAST_LEN)
    kv_row = (2, N_KV_HEADS, HEAD_DIM)

    def nrm(i, shape, scale=1.0):
        return scale * jax.random.normal(ks[i], shape, f32)

    def gain(i, shape):
        return 1.0 + 0.05 * jax.random.normal(ks[i], shape, f32)

    page_table = jax.random.permutation(ks[8], n_pool)[:DEC_BATCH * n_pages].reshape(DEC_BATCH, n_pages).astype(jnp.int32)
    a_im = jnp.pi * jnp.arange(SSM_STATE, dtype=f32) + nrm(24, (DEPTH, SSM_GROUPS, SSM_STATE), 0.01)
    log_dt = math.log(1e-3) + (math.log(1e-1) - math.log(1e-3)) * jax.random.uniform(ks[25], (DEPTH, SSM_GROUPS), f32)
    return {
        'x_prompt': nrm(0, (BATCH, SEQ, D_MODEL)),
        'x_sample': nrm(1, (DEC_BATCH, DEC_SEQ, D_MODEL)),
        'cache_cmp': nrm(2, (DEPTH, n_pool, PAGE_SIZE) + kv_row),
        'cache_sel': nrm(3, (DEPTH, n_pool, PAGE_SIZE) + kv_row),
        'cache_win': nrm(4, (DEPTH, DEC_BATCH, win_buf) + kv_row),
        'state_ssm_re': nrm(5, (DEPTH, DEC_BATCH, SSM_GROUPS, SSM_STATE), 0.3),
        'state_ssm_im': nrm(6, (DEPTH, DEC_BATCH, SSM_GROUPS, SSM_STATE), 0.3),
        'state_conv': nrm(7, (DEPTH, DEC_BATCH, CONV_W - 1, D_FF)),
        'page_table': page_table,
        'p_prompt': nrm(9, (DEPTH, BATCH, SEQ, PLE_DIM)),
        'p_sample': nrm(10, (DEPTH, DEC_BATCH, DEC_SEQ, PLE_DIM)),
        'g_attn': gain(11, (DEPTH, D_MODEL)),
        'w_in': nrm(12, (DEPTH, D_MODEL, N_IN), D_MODEL ** -0.5),
        'g_q': gain(13, (DEPTH, HEAD_DIM)),
        'g_kc': gain(14, (DEPTH, HEAD_DIM)),
        'g_ks': gain(15, (DEPTH, HEAD_DIM)),
        'g_kw': gain(16, (DEPTH, HEAD_DIM)),
        'cmp_wk1': nrm(17, (DEPTH, CMP_BLOCK, HEAD_DIM, CMP_HIDDEN), (CMP_BLOCK * HEAD_DIM) ** -0.5),
        'cmp_pek': nrm(18, (DEPTH, CMP_BLOCK, HEAD_DIM), 0.1),
        'cmp_wk2': nrm(19, (DEPTH, CMP_HIDDEN, HEAD_DIM), CMP_HIDDEN ** -0.5),
        'cmp_wv1': nrm(20, (DEPTH, CMP_BLOCK, HEAD_DIM, CMP_HIDDEN), (CMP_BLOCK * HEAD_DIM) ** -0.5),
        'cmp_pev': nrm(21, (DEPTH, CMP_BLOCK, HEAD_DIM), 0.1),
        'cmp_wv2': nrm(22, (DEPTH, CMP_HIDDEN, HEAD_DIM), CMP_HIDDEN ** -0.5),
        'ssm_a_re': -0.5 + nrm(23, (DEPTH, SSM_GROUPS, SSM_STATE), 0.01),
        'ssm_a_im': a_im,
        'ssm_log_dt': log_dt,
        'ssm_b_re': nrm(26, (DEPTH, SSM_GROUPS, SSM_STATE, SSM_GROUP), (2 * SSM_GROUP) ** -0.5),
        'ssm_b_im': nrm(27, (DEPTH, SSM_GROUPS, SSM_STATE, SSM_GROUP), (2 * SSM_GROUP) ** -0.5),
        'ssm_c_re': nrm(28, (DEPTH, SSM_GROUPS, SSM_GROUP, SSM_STATE), SSM_STATE ** -0.5),
        'ssm_c_im': nrm(29, (DEPTH, SSM_GROUPS, SSM_GROUP, SSM_STATE), SSM_STATE ** -0.5),
        'ssm_d': nrm(30, (DEPTH, SSM_GROUPS, SSM_GROUP), 0.5),
        'w_a': nrm(31, (DEPTH, Q_W, D_MODEL), Q_W ** -0.5),
        'w_glu1': nrm(32, (DEPTH, SSM_WIDTH, D_MODEL), SSM_WIDTH ** -0.5),
        'w_glu2': nrm(33, (DEPTH, SSM_WIDTH, D_MODEL), SSM_WIDTH ** -0.5),
        'w_o': nrm(34, (DEPTH, D_MODEL, D_MODEL), D_MODEL ** -0.5),
        'g_ffn': gain(35, (DEPTH, D_MODEL)),
        'w_up': nrm(36, (DEPTH, D_MODEL, 2 * D_FF), D_MODEL ** -0.5),
        'conv_w': nrm(37, (DEPTH, CONV_W, D_FF), CONV_W ** -0.5),
        'conv_b': nrm(38, (DEPTH, D_FF), 0.02),
        'w_down': nrm(39, (DEPTH, D_FF, D_MODEL), D_FF ** -0.5),
        'g_ple': gain(40, (DEPTH, D_MODEL)),
        'w_ple_gate': nrm(41, (DEPTH, D_MODEL, D_MODEL), D_MODEL ** -0.5),
        'w_ple': nrm(42, (DEPTH, PLE_DIM, D_MODEL), PLE_DIM ** -0.5),
    }


def reference(x_prompt, x_sample, cache_cmp, cache_sel, cache_win, state_ssm_re, state_ssm_im, state_conv,
              page_table, p_prompt, p_sample, g_attn, w_in, g_q, g_kc, g_ks, g_kw,
              cmp_wk1, cmp_pek, cmp_wk2, cmp_wv1, cmp_pev, cmp_wv2,
              ssm_a_re, ssm_a_im, ssm_log_dt, ssm_b_re, ssm_b_im, ssm_c_re, ssm_c_im, ssm_d,
              w_a, w_glu1, w_glu2, w_o, g_ffn, w_up, conv_w, conv_b, w_down, g_ple, w_ple_gate, w_ple):
    Bp, Tp = x_prompt.shape[:2]
    Ts = x_sample.shape[1]
    past = page_table.shape[1] * PAGE_SIZE
    pos_p = jnp.arange(Tp)
    pos_s = past + jnp.arange(Ts)
    zeros_h = jnp.zeros((Bp, SSM_GROUPS, SSM_STATE), x_prompt.dtype)
    zeros_c = jnp.zeros((Bp, CONV_W - 1, D_FF), x_prompt.dtype)
    layer_w = (g_attn, w_in, g_q, g_kc, g_ks, g_kw, cmp_wk1, cmp_pek, cmp_wk2, cmp_wv1, cmp_pev, cmp_wv2,
               ssm_a_re, ssm_a_im, ssm_log_dt, ssm_b_re, ssm_b_im, ssm_c_re, ssm_c_im, ssm_d,
               w_a, w_glu1, w_glu2, w_o, g_ffn, w_up, conv_w, conv_b, w_down, g_ple, w_ple_gate, w_ple)
    st = [[] for _ in range(12)]
    hp, hs = x_prompt, x_sample
    for i in range(DEPTH):
        lw = [w[i] for w in layer_w]
        hp, rows, hr, hi, cv = block(hp, p_prompt[i], pos_p, lw, nsa_prompt, zeros_h, zeros_h, zeros_c)
        for j, a in enumerate(list(rows) + [hr, hi, cv]):
            st[j].append(a)
        nsa_s = functools.partial(nsa_sample, cmp_pool=cache_cmp[i], sel_pool=cache_sel[i],
                                  win_buf=cache_win[i], page_table=page_table)
        hs, rows, hr, hi, cv = block(hs, p_sample[i], pos_s, lw, nsa_s, state_ssm_re[i], state_ssm_im[i], state_conv[i])
        for j, a in enumerate(list(rows) + [hr, hi, cv]):
            st[6 + j].append(a)
    return (hp, hs,
            jnp.stack(st[0]), jnp.stack(st[1]), jnp.stack(st[2]), jnp.stack(st[3]), jnp.stack(st[4]), jnp.stack(st[5]),
            jnp.stack(st[6]), jnp.stack(st[7]), jnp.stack(st[8]), jnp.stack(st[9]), jnp.stack(st[10]), jnp.stack(st[11]))
```

```python
import functools
import math

import numpy as np
import jax
import jax.numpy as jnp
from jax import lax
from jax.experimental import pallas as pl
from jax.experimental.pallas import tpu as pltpu

D_MODEL = 1024
N_HEADS = 8
N_KV_HEADS = 2
HEAD_DIM = 64
GROUP = N_HEADS // N_KV_HEADS
Q_W = N_HEADS * HEAD_DIM
KV_W = N_KV_HEADS * HEAD_DIM
CMP_BLOCK = 32
CMP_STRIDE = 16
SEL_BLOCK = 64
N_SEL = 8
WINDOW = 512
Q_BLOCK = 128
PAGE_SIZE = 128
ROPE_THETA = 10000.0
SSM_WIDTH = D_MODEL // 2
SSM_GROUP = 16
SSM_GROUPS = SSM_WIDTH // SSM_GROUP
SSM_STATE = 64
SSM_CHUNK = 128
D_FF = 11 * D_MODEL // 4
CONV_W = 3
EPS = 1e-6
NEG_INF = -1e30
FORCE_SCORE = 1e9
SCALE = HEAD_DIM ** -0.5
SPLIT_SIZES = (Q_W, KV_W, KV_W, KV_W, KV_W, KV_W, KV_W, 3 * N_HEADS, SSM_WIDTH, D_MODEL, D_MODEL)

VMEM_LIMIT_BYTES = 56 * 1024 * 1024


def _mm_kernel(x_ref, g_ref, w_ref, o_ref, *, norm):
    x = x_ref[...]
    if norm:
        x = x * lax.rsqrt(jnp.mean(x * x, axis=-1, keepdims=True) + EPS) * g_ref[...]
    o_ref[...] = jnp.dot(x.astype(jnp.bfloat16), w_ref[...], preferred_element_type=jnp.float32)


def _pick_tile(n, cands):
    for c in cands:
        if n % c == 0:
            return c
    return n


def _matmul(x, w, g=None):
    n, k = x.shape
    m = w.shape[1]
    tm = _pick_tile(n, (512, 256, 128))
    tn = _pick_tile(m, (1024, 512, 256, 128))
    norm = g is not None
    if g is None:
        g = jnp.ones((k,), jnp.float32)
    return pl.pallas_call(
        functools.partial(_mm_kernel, norm=norm),
        out_shape=jax.ShapeDtypeStruct((n, m), jnp.float32),
        grid=(n // tm, m // tn),
        in_specs=[pl.BlockSpec((tm, k), lambda i, j: (i, 0)),
                  pl.BlockSpec((1, k), lambda i, j: (0, 0)),
                  pl.BlockSpec((k, tn), lambda i, j: (0, j))],
        out_specs=pl.BlockSpec((tm, tn), lambda i, j: (i, j)),
        compiler_params=pltpu.CompilerParams(
            dimension_semantics=("parallel", "parallel"), vmem_limit_bytes=VMEM_LIMIT_BYTES),
    )(x, g.reshape(1, k).astype(jnp.float32), w.astype(jnp.bfloat16))


def _rmsnorm(x, g):
    xf = x.astype(jnp.float32)
    y = xf * lax.rsqrt(jnp.mean(xf * xf, axis=-1, keepdims=True) + EPS)
    return (y * g.astype(jnp.float32)).astype(x.dtype)


def _rope(x, pos):
    half = HEAD_DIM // 2
    inv = jnp.float32(ROPE_THETA) ** (-jnp.arange(half, dtype=jnp.float32) / half)
    ang = pos.astype(jnp.float32)[:, None] * inv[None, :]
    cos = jnp.cos(ang)[:, None, :]
    sin = jnp.sin(ang)[:, None, :]
    x1, x2 = x[..., :half], x[..., half:]
    return jnp.concatenate([x1 * cos - x2 * sin, x2 * cos + x1 * sin], axis=-1)


def _masked_softmax(s, mask):
    s = jnp.where(mask, s.astype(jnp.float32), NEG_INF)
    m = jnp.max(s, axis=-1, keepdims=True)
    e = jnp.where(mask, jnp.exp(s - m), 0.0)
    return e / jnp.maximum(jnp.sum(e, axis=-1, keepdims=True), 1e-30)


def _attend(q, k, v, mask):
    s = jnp.einsum('btkgd,bskd->bkgts', q, k) * SCALE
    p = _masked_softmax(s, mask)
    o = jnp.einsum('bkgts,bskd->btkgd', p.astype(v.dtype), v)
    return o, p


def _window_mask(q_pos, k_pos):
    d = q_pos[:, None] - k_pos[None, :]
    return (d >= 0) & (d < WINDOW) & (k_pos[None, :] >= 0)


def _compress(x, w1, pe, w2):
    B, L, K, D = x.shape
    xt = x.transpose(0, 2, 1, 3).reshape(B * K, L, D)
    hdn = lax.conv_general_dilated(xt, w1.astype(xt.dtype), (CMP_STRIDE,), 'VALID',
                                   dimension_numbers=('NWC', 'WIO', 'NWC'))
    hdn = hdn + jnp.einsum('ld,ldf->f', pe, w1)
    out = jax.nn.gelu(hdn) @ w2
    n = out.shape[1]
    return out.reshape(B, K, n, D).transpose(0, 2, 1, 3)


def _cmp_branch(q, kc_full, vc_full, q_pos, wk1, pek, wk2, wv1, pev, wv2):
    kcmp = _compress(kc_full, wk1, pek, wk2)
    vcmp = _compress(vc_full, wv1, pev, wv2)
    n_cmp = kcmp.shape[1]
    start = jnp.arange(n_cmp) * CMP_STRIDE
    end = start + CMP_BLOCK - 1
    kcmp = _rope(kcmp, end)
    o, p = _attend(q, kcmp, vcmp, end[None, :] <= q_pos[:, None])
    L = kc_full.shape[1]
    n_sb = -(-L // SEL_BLOCK)
    sb_start = jnp.arange(n_sb) * SEL_BLOCK
    overlap = ((start[:, None] < sb_start[None, :] + SEL_BLOCK) & (end[:, None] >= sb_start[None, :])).astype(jnp.float32)
    imp = jnp.einsum('bkgtn,nj->btkj', p, overlap)
    blk = jnp.arange(n_sb)
    forced = (blk[None, :] == 0) | (blk[None, :] == (q_pos // SEL_BLOCK)[:, None])
    causal = sb_start[None, :] <= q_pos[:, None]
    imp = jnp.where(forced[None, :, None, :], FORCE_SCORE, imp)
    imp = jnp.where(causal[None, :, None, :], imp, NEG_INF)
    top_v, top_i = lax.top_k(imp, min(N_SEL, n_sb))
    return o, top_i, top_v > 0.5 * NEG_INF


def _sel_attend(q, idx, valid, q_pos, fetch):
    pos = idx[..., None] * SEL_BLOCK + jnp.arange(SEL_BLOCK)
    k, v = fetch(pos)
    B, Tc, K, N, S = pos.shape
    mask = (valid[..., None] & (pos <= q_pos[None, :, None, None, None])).reshape(B, Tc, K, N * S)
    k = k.reshape(B, Tc, K, N * S, HEAD_DIM)
    v = v.reshape(B, Tc, K, N * S, HEAD_DIM)
    s = jnp.einsum('btkgd,btksd->btkgs', q, k) * SCALE
    p = _masked_softmax(s, mask[:, :, :, None, :])
    return jnp.einsum('btkgs,btksd->btkgd', p.astype(v.dtype), v)


def _to_chunks(x, c):
    B, T = x.shape[:2]
    return x.reshape((B, T // c, c) + x.shape[2:]).swapaxes(0, 1)


def _from_chunks(x):
    nc, B, c = x.shape[:3]
    return x.swapaxes(0, 1).reshape((B, nc * c) + x.shape[3:])


def _batch_head_index(B):
    bi = jnp.arange(B)[:, None, None, None, None]
    hd = jnp.arange(N_KV_HEADS)[None, None, :, None, None]
    return bi, hd


def _nsa_prompt(q, kc, vc, ks, vs, kw, vw, cmpw):
    B, T = q.shape[:2]
    pos = jnp.arange(T)
    o_cmp, idx, valid = _cmp_branch(q, kc, vc, pos, *cmpw)
    bi, hd = _batch_head_index(B)

    def fetch(p):
        pc = jnp.clip(p, 0, T - 1)
        return ks[bi, pc, hd], vs[bi, pc, hd]

    qb = Q_BLOCK if T % Q_BLOCK == 0 else T
    o_sel = _from_chunks(lax.map(lambda a: _sel_attend(a[0], a[1], a[2], a[3], fetch),
                                 (_to_chunks(q, qb), _to_chunks(idx, qb), _to_chunks(valid, qb), pos.reshape(-1, qb))))
    kp = jnp.pad(kw, ((0, 0), (WINDOW, 0), (0, 0), (0, 0)))
    vp = jnp.pad(vw, ((0, 0), (WINDOW, 0), (0, 0), (0, 0)))

    def win_block(a):
        c, qc = a
        start = c * qb
        kb = lax.dynamic_slice_in_dim(kp, start, WINDOW + qb, axis=1)
        vb = lax.dynamic_slice_in_dim(vp, start, WINDOW + qb, axis=1)
        qpos = start + jnp.arange(qb)
        kpos = start - WINDOW + jnp.arange(WINDOW + qb)
        return _attend(qc, kb, vb, _window_mask(qpos, kpos))[0]

    o_win = _from_chunks(lax.map(win_block, (jnp.arange(T // qb), _to_chunks(q, qb))))
    n_keep = min(WINDOW, T)
    rows = (jnp.stack([kc, vc], axis=2), jnp.stack([ks, vs], axis=2), jnp.stack([kw, vw], axis=2)[:, T - n_keep:])
    return o_cmp, o_sel, o_win, rows


def _nsa_sample(q, kc, vc, ks, vs, kw, vw, cmpw, cmp_pool, sel_pool, win_buf, page_table):
    B, T = q.shape[:2]
    past = page_table.shape[1] * PAGE_SIZE
    pos = past + jnp.arange(T)
    past_cmp = cmp_pool[page_table].reshape((B, past) + cmp_pool.shape[2:])
    kc_full = jnp.concatenate([past_cmp[:, :, 0], kc], axis=1)
    vc_full = jnp.concatenate([past_cmp[:, :, 1], vc], axis=1)
    o_cmp, idx, valid = _cmp_branch(q, kc_full, vc_full, pos, *cmpw)
    sel_rows = sel_pool.reshape((-1,) + sel_pool.shape[2:])
    bi, hd = _batch_head_index(B)

    def fetch(p):
        pc = jnp.clip(p, 0, past - 1)
        phys = page_table[bi, pc // PAGE_SIZE] * PAGE_SIZE + pc % PAGE_SIZE
        pn = jnp.clip(p - past, 0, T - 1)
        in_past = (p < past)[..., None]
        k = jnp.where(in_past, sel_rows[phys, 0, hd], ks[bi, pn, hd])
        v = jnp.where(in_past, sel_rows[phys, 1, hd], vs[bi, pn, hd])
        return k, v

    o_sel = _sel_attend(q, idx, valid, pos, fetch)
    wb = win_buf.shape[1]
    win_all = jnp.concatenate([win_buf, jnp.stack([kw, vw], axis=2)], axis=1)
    kpos = past - wb + jnp.arange(wb + T)
    o_win, _ = _attend(q, win_all[:, :, 0], win_all[:, :, 1], _window_mask(pos, kpos))
    rows = (jnp.stack([kc, vc], axis=2), jnp.stack([ks, vs], axis=2), win_all[:, T:])
    return o_cmp, o_sel, o_win, rows


def _cmul_combine(e1, e2):
    ar1, ai1, br1, bi1 = e1
    ar2, ai2, br2, bi2 = e2
    return (ar2 * ar1 - ai2 * ai1, ar2 * ai1 + ai2 * ar1,
            ar2 * br1 - ai2 * bi1 + br2, ar2 * bi1 + ai2 * br1 + bi2)


def _ssm_scan(u, h0_re, h0_im, a_re, a_im, log_dt, b_re, b_im, c_re, c_im, d_skip):
    B, T, _ = u.shape
    uf = u.reshape(B, T, SSM_GROUPS, SSM_GROUP)
    ar = a_re
    ai = a_im
    dt = jnp.exp(log_dt)[:, None]
    mag = jnp.exp(dt * ar)
    ab_re = mag * jnp.cos(dt * ai)
    ab_im = mag * jnp.sin(dt * ai)
    den = ar * ar + ai * ai
    zr = ((ab_re - 1.0) * ar + ab_im * ai) / den
    zi = (ab_im * ar - (ab_re - 1.0) * ai) / den
    bb_re = zr[..., None] * b_re - zi[..., None] * b_im
    bb_im = zr[..., None] * b_im + zi[..., None] * b_re
    bu_re = jnp.einsum('gpc,btgc->tbgp', bb_re, uf)
    bu_im = jnp.einsum('gpc,btgc->tbgp', bb_im, uf)
    chunk = SSM_CHUNK if T % SSM_CHUNK == 0 else T
    nc = T // chunk
    shp = (chunk, B, SSM_GROUPS, SSM_STATE)
    bu_re = bu_re.reshape((nc,) + shp)
    bu_im = bu_im.reshape((nc,) + shp)
    a_re_c = jnp.broadcast_to(ab_re, shp)
    a_im_c = jnp.broadcast_to(ab_im, shp)

    def step(carry, xs):
        hr0, hi0 = carry
        xr, xi = xs
        pr, pim, sr, si = lax.associative_scan(_cmul_combine, (a_re_c, a_im_c, xr, xi), axis=0)
        hr = sr + pr * hr0 - pim * hi0
        hi = si + pr * hi0 + pim * hr0
        y = jnp.einsum('gcp,tbgp->tbgc', c_re, hr) - jnp.einsum('gcp,tbgp->tbgc', c_im, hi)
        return (hr[-1], hi[-1]), y

    (hr, hi), y = lax.scan(step, (h0_re, h0_im), (bu_re, bu_im))
    y = y.reshape(T, B, SSM_GROUPS, SSM_GROUP).transpose(1, 0, 2, 3) + d_skip * uf
    return y.reshape(B, T, SSM_WIDTH), hr, hi


def _block(h, p_l, pos, lw, nsa_fn, h0_re, h0_im, conv_prefix):
    (g_attn, w_in, g_q, g_kc, g_ks, g_kw, wk1, pek, wk2, wv1, pev, wv2,
     a_re, a_im, log_dt, b_re, b_im, c_re, c_im, d_skip,
     w_a, w_glu1, w_glu2, w_o, g_ffn, w_up, conv_w, conv_b, w_down,
     g_ple, w_ple_gate, w_ple) = lw
    B, T, _ = h.shape
    N = B * T

    def mm(x, w, g=None):
        return _matmul(x.reshape(N, x.shape[-1]), w, g).reshape(B, T, w.shape[1])

    z = mm(h, w_in, g_attn)
    pts = [int(s) for s in np.cumsum(SPLIT_SIZES)[:-1]]
    q, kc, vc, ks, vs, kw, vw, gl, u, ga, gb = jnp.split(z, pts, axis=-1)

    def heads(t):
        return t.reshape(B, T, -1, HEAD_DIM)

    q = _rope(_rmsnorm(heads(q), g_q), pos).reshape(B, T, N_KV_HEADS, GROUP, HEAD_DIM)
    kc = _rmsnorm(heads(kc), g_kc)
    ks = _rope(_rmsnorm(heads(ks), g_ks), pos)
    kw = _rope(_rmsnorm(heads(kw), g_kw), pos)
    vc, vs, vw = heads(vc), heads(vs), heads(vw)
    o_cmp, o_sel, o_win, rows = nsa_fn(q, kc, vc, ks, vs, kw, vw, (wk1, pek, wk2, wv1, pev, wv2))
    bg = jax.nn.sigmoid(gl).reshape(B, T, N_KV_HEADS, GROUP, 3)
    o = bg[..., 0:1] * o_cmp + bg[..., 1:2] * o_sel + bg[..., 2:3] * o_win
    a_out = mm(o.reshape(B, T, Q_W), w_a)
    y, hr, hi = _ssm_scan(u, h0_re, h0_im, a_re, a_im, log_dt, b_re, b_im, c_re, c_im, d_skip)
    yg = jax.nn.gelu(y)
    b_out = mm(yg, w_glu1) * jax.nn.sigmoid(mm(yg, w_glu2))
    h = h + mm(jax.nn.sigmoid(ga) * a_out + jax.nn.sigmoid(gb) * b_out, w_o)
    up = mm(h, w_up, g_ffn)
    gp, val = jnp.split(up, 2, axis=-1)
    ext = jnp.concatenate([conv_prefix, gp], axis=1)
    conv = conv_b
    for j in range(CONV_W):
        conv = conv + conv_w[j] * ext[:, j:j + T]
    h = h + mm(jax.nn.gelu(conv) * val, w_down)
    h = h + jax.nn.sigmoid(mm(h, w_ple_gate, g_ple)) * mm(p_l, w_ple)
    return h, rows, hr, hi, ext[:, T:]


def kernel(x_prompt, x_sample, cache_cmp, cache_sel, cache_win, state_ssm_re, state_ssm_im, state_conv, page_table, p_prompt, p_sample, g_attn, w_in, g_q, g_kc, g_ks, g_kw, cmp_wk1, cmp_pek, cmp_wk2, cmp_wv1, cmp_pev, cmp_wv2, ssm_a_re, ssm_a_im, ssm_log_dt, ssm_b_re, ssm_b_im, ssm_c_re, ssm_c_im, ssm_d, w_a, w_glu1, w_glu2, w_o, g_ffn, w_up, conv_w, conv_b, w_down, g_ple, w_ple_gate, w_ple):
    Bp, Tp = x_prompt.shape[:2]
    Ts = x_sample.shape[1]
    depth = w_in.shape[0]
    past = page_table.shape[1] * PAGE_SIZE
    pos_p = jnp.arange(Tp)
    pos_s = past + jnp.arange(Ts)
    zeros_h = jnp.zeros((Bp, SSM_GROUPS, SSM_STATE), x_prompt.dtype)
    zeros_c = jnp.zeros((Bp, CONV_W - 1, D_FF), x_prompt.dtype)
    layer_w = (g_attn, w_in, g_q, g_kc, g_ks, g_kw, cmp_wk1, cmp_pek, cmp_wk2, cmp_wv1, cmp_pev, cmp_wv2,
               ssm_a_re, ssm_a_im, ssm_log_dt, ssm_b_re, ssm_b_im, ssm_c_re, ssm_c_im, ssm_d,
               w_a, w_glu1, w_glu2, w_o, g_ffn, w_up, conv_w, conv_b, w_down, g_ple, w_ple_gate, w_ple)
    st = [[] for _ in range(12)]
    hp, hs = x_prompt, x_sample
    for i in range(depth):
        lw = [w[i] for w in layer_w]
        hp, rows, hr, hi, cv = _block(hp, p_prompt[i], pos_p, lw, _nsa_prompt, zeros_h, zeros_h, zeros_c)
        for j, a in enumerate(list(rows) + [hr, hi, cv]):
            st[j].append(a)
        nsa_s = functools.partial(_nsa_sample, cmp_pool=cache_cmp[i], sel_pool=cache_sel[i],
                                  win_buf=cache_win[i], page_table=page_table)
        hs, rows, hr, hi, cv = _block(hs, p_sample[i], pos_s, lw, nsa_s, state_ssm_re[i], state_ssm_im[i], state_conv[i])
        for j, a in enumerate(list(rows) + [hr, hi, cv]):
            st[6 + j].append(a)
    return (hp, hs) + tuple(jnp.stack(s) for s in st)
```

```python
import functools
import math

import numpy as np
import jax
import jax.numpy as jnp
from jax import lax
from jax.experimental import pallas as pl
from jax.experimental.pallas import tpu as pltpu

D_MODEL = 1024
N_HEADS = 8
N_KV_HEADS = 2
HEAD_DIM = 64
GROUP = N_HEADS // N_KV_HEADS
Q_W = N_HEADS * HEAD_DIM
KV_W = N_KV_HEADS * HEAD_DIM
CMP_BLOCK = 32
CMP_STRIDE = 16
SEL_BLOCK = 64
N_SEL = 8
WINDOW = 512
Q_BLOCK = 128
PAGE_SIZE = 128
ROPE_THETA = 10000.0
SSM_WIDTH = D_MODEL // 2
SSM_GROUP = 16
SSM_GROUPS = SSM_WIDTH // SSM_GROUP
SSM_STATE = 64
SSM_CHUNK = 128
D_FF = 11 * D_MODEL // 4
CONV_W = 3
EPS = 1e-6
NEG_INF = -1e30
FORCE_SCORE = 1e9
SCALE = HEAD_DIM ** -0.5
SPLIT_SIZES = (Q_W, KV_W, KV_W, KV_W, KV_W, KV_W, KV_W, 3 * N_HEADS, SSM_WIDTH, D_MODEL, D_MODEL)

VMEM_LIMIT_BYTES = 56 * 1024 * 1024


def _mm_kernel(x_ref, g_ref, w_ref, o_ref, *, norm):
    x = x_ref[...]
    if norm:
        x = x * lax.rsqrt(jnp.mean(x * x, axis=-1, keepdims=True) + EPS) * g_ref[...]
    o_ref[...] = jnp.dot(x.astype(jnp.bfloat16), w_ref[...], preferred_element_type=jnp.float32)


def _pick_tile(n, cands):
    for c in cands:
        if n % c == 0:
            return c
    return n


def _matmul(x, w, g=None):
    n, k = x.shape
    m = w.shape[1]
    tm = _pick_tile(n, (512, 256, 128))
    tn = _pick_tile(m, (1024, 512, 256, 128))
    norm = g is not None
    if g is None:
        g = jnp.ones((k,), jnp.float32)
    return pl.pallas_call(
        functools.partial(_mm_kernel, norm=norm),
        out_shape=jax.ShapeDtypeStruct((n, m), jnp.float32),
        grid=(n // tm, m // tn),
        in_specs=[pl.BlockSpec((tm, k), lambda i, j: (i, 0)),
                  pl.BlockSpec((1, k), lambda i, j: (0, 0)),
                  pl.BlockSpec((k, tn), lambda i, j: (0, j))],
        out_specs=pl.BlockSpec((tm, tn), lambda i, j: (i, j)),
        compiler_params=pltpu.CompilerParams(
            dimension_semantics=("parallel", "parallel"), vmem_limit_bytes=VMEM_LIMIT_BYTES),
    )(x, g.reshape(1, k).astype(jnp.float32), w.astype(jnp.bfloat16))


HALF_ROWS = CMP_BLOCK // CMP_STRIDE
assert HALF_ROWS == 2


def _rope_lanes(x, cos, sin_signed):
    w = x.shape[-1]
    half = HEAD_DIM // 2
    lane = lax.broadcasted_iota(jnp.int32, x.shape, x.ndim - 1)
    first = (lane % HEAD_DIM) < half
    partner = jnp.where(first, pltpu.roll(x, w - half, x.ndim - 1), pltpu.roll(x, half, x.ndim - 1))
    return x * cos + partner * sin_signed


def _compress_kernel(xk_ref, xv_ref, wk_ref, wv_ref, pek_ref, pev_ref, w1k_ref, w1v_ref, w2k_ref, w2v_ref,
                     cos_ref, sin_ref, ko_ref, vo_ref, *, n_half, n_cmp):
    def one(x_ref, w_ref, pe_ref, w1_ref, w2_ref):
        acc = jnp.zeros((n_half, 4 * 128), jnp.float32)
        for j in range(CMP_STRIDE):
            xj = x_ref[0, pl.ds(j, n_half, stride=CMP_STRIDE), :]
            acc = acc + jnp.dot(xj.astype(jnp.bfloat16), w_ref[j], preferred_element_type=jnp.float32)
        pa = acc[:, :256]
        pb = pltpu.roll(acc[:, 256:], n_half - 1, 0)
        bias = jnp.dot(pe_ref[...].astype(jnp.bfloat16), w1_ref[...], preferred_element_type=jnp.float32)[0:1]
        bias2 = jnp.concatenate([bias, bias], axis=1)
        hdn = jax.nn.gelu(pa + pb + bias2)
        return jnp.dot(hdn.astype(jnp.bfloat16), w2_ref[...], preferred_element_type=jnp.float32)

    row = lax.broadcasted_iota(jnp.int32, (n_half, KV_W), 0)
    k = one(xk_ref, wk_ref, pek_ref, w1k_ref, w2k_ref)
    k = _rope_lanes(k, cos_ref[...], sin_ref[...])
    v = one(xv_ref, wv_ref, pev_ref, w1v_ref, w2v_ref)
    ko_ref[0] = jnp.where(row < n_cmp, k, 0.0).astype(ko_ref.dtype)
    vo_ref[0] = jnp.where(row < n_cmp, v, 0.0).astype(vo_ref.dtype)


def _blockdiag2(w):
    z = jnp.zeros_like(w)
    return jnp.concatenate([jnp.concatenate([w, z], axis=-1), jnp.concatenate([z, w], axis=-1)], axis=-2)


def _compress_weights(w1, pe, w2):
    bd = _blockdiag2(w1)
    wcat = jnp.concatenate([bd[:CMP_STRIDE], bd[CMP_STRIDE:]], axis=-1).astype(jnp.bfloat16)
    pe_flat = jnp.broadcast_to(pe.reshape(1, -1), (8, pe.size))
    w1_flat = w1.reshape(-1, w1.shape[-1]).astype(jnp.bfloat16)
    w2bd = _blockdiag2(w2).astype(jnp.bfloat16)
    return wcat, pe_flat, w1_flat, w2bd


def _rope_tables(pos, reps):
    half = HEAD_DIM // 2
    inv = jnp.float32(ROPE_THETA) ** (-jnp.arange(half, dtype=jnp.float32) / half)
    ang = pos.astype(jnp.float32)[:, None] * inv[None, :]
    cos = jnp.cos(ang)
    sin = jnp.sin(ang)
    return (jnp.tile(jnp.concatenate([cos, cos], axis=-1), (1, reps)),
            jnp.tile(jnp.concatenate([-sin, sin], axis=-1), (1, reps)))


def _compress_pallas(krows, vrows, cmpw):
    wk1, pek, wk2, wv1, pev, wv2 = cmpw
    B, L, _ = krows.shape
    n_half = L // CMP_STRIDE
    n_cmp = n_half - 1
    wk, pekf, w1k, w2k = _compress_weights(wk1, pek, wk2)
    wv, pevf, w1v, w2v = _compress_weights(wv1, pev, wv2)
    end = jnp.arange(n_half) * CMP_STRIDE + CMP_BLOCK - 1
    cos, sin = _rope_tables(end, N_KV_HEADS)
    full = lambda a: pl.BlockSpec(a.shape, lambda b: (0,) * a.ndim)
    consts = (wk, wv, pekf, pevf, w1k, w1v, w2k, w2v, cos, sin)
    return pl.pallas_call(
        functools.partial(_compress_kernel, n_half=n_half, n_cmp=n_cmp),
        out_shape=(jax.ShapeDtypeStruct((B, n_half, KV_W), jnp.bfloat16),) * 2,
        grid=(B,),
        in_specs=[pl.BlockSpec((1, L, KV_W), lambda b: (b, 0, 0))] * 2 + [full(a) for a in consts],
        out_specs=(pl.BlockSpec((1, n_half, KV_W), lambda b: (b, 0, 0)),) * 2,
        compiler_params=pltpu.CompilerParams(dimension_semantics=("parallel",), vmem_limit_bytes=VMEM_LIMIT_BYTES),
    )(krows, vrows, *consts)


ATT_TQ = 128
SEL_KC = 512
BIG_NEG = -3.0e38


def _softmax_rows(s, mask):
    s = jnp.where(mask, s, NEG_INF)
    m = jnp.max(s, axis=-1, keepdims=True)
    e = jnp.where(mask, jnp.exp(s - m), 0.0)
    return e / jnp.maximum(jnp.sum(e, axis=-1, keepdims=True), 1e-30)


def _select_blocks(imp, tpos, n_sb):
    tq = imp.shape[0]
    jl = lax.broadcasted_iota(jnp.int32, (tq, 128), 1)
    jf = jl.astype(jnp.float32)
    forced = (jl == 0) | (jl == (tpos >> 6))
    imp = jnp.where(forced, FORCE_SCORE, imp)
    imp = jnp.where(jl * SEL_BLOCK <= tpos, imp, NEG_INF)
    imp = jnp.where(jl < n_sb, imp, BIG_NEG)
    sel = jnp.zeros((tq, 128), jnp.float32)
    for _ in range(N_SEL):
        m = jnp.max(imp, axis=-1, keepdims=True)
        first = jnp.min(jnp.where(imp == m, jf, 1e9), axis=-1, keepdims=True)
        hit = jf == first
        sel = jnp.where(hit & (m > 0.5 * NEG_INF), 1.0, sel)
        imp = jnp.where(hit, BIG_NEG, imp)
    return sel


def _online_step(q4, k, v, mask, m, l, acc):
    tq = mask.shape[0]
    s = lax.dot_general(q4, k, (((1,), (1,)), ((), ())), preferred_element_type=jnp.float32)
    s = jnp.where(mask[None], s.reshape(4, tq, -1), NEG_INF)
    m_new = jnp.maximum(m, jnp.max(s, axis=-1, keepdims=True))
    alpha = jnp.exp(m - m_new)
    p = jnp.exp(s - m_new)
    l = alpha * l + jnp.sum(p, axis=-1, keepdims=True)
    pv = jnp.dot(p.reshape(4 * tq, -1).astype(jnp.bfloat16), v, preferred_element_type=jnp.float32)
    acc = alpha * acc + pv.reshape(4, tq, HEAD_DIM)
    return m_new, l, acc


def _nsa_prompt_kernel(q_ref, kcmp_ref, vcmp_ref, ks_ref, vs_ref, kw_ref, vw_ref, gl_ref, ov_ref, o_ref,
                       *, tq, n_cmp, n_sb):
    i = pl.program_id(2)
    t0 = i * tq
    qf = q_ref[0] * SCALE
    q4 = jnp.concatenate([qf[:, g * HEAD_DIM:(g + 1) * HEAD_DIM] for g in range(GROUP)], axis=0).astype(jnp.bfloat16)
    tpos = t0 + lax.broadcasted_iota(jnp.int32, (tq, 1), 0)

    nl = lax.broadcasted_iota(jnp.int32, (tq, 128), 1)
    maskc = ((nl * CMP_STRIDE + (CMP_BLOCK - 1)) <= tpos) & (nl < n_cmp)
    sc = lax.dot_general(q4, kcmp_ref[0, 0], (((1,), (1,)), ((), ())), preferred_element_type=jnp.float32)
    pc = _softmax_rows(sc.reshape(GROUP, tq, 128), maskc[None])
    o_cmp = jnp.dot(pc.reshape(GROUP * tq, 128).astype(jnp.bfloat16), vcmp_ref[0, 0],
                    preferred_element_type=jnp.float32).reshape(GROUP, tq, HEAD_DIM)
    psum = pc[0] + pc[1] + pc[2] + pc[3]
    p_hi = psum.astype(jnp.bfloat16)
    p_lo = (psum - p_hi.astype(jnp.float32)).astype(jnp.bfloat16)
    imp = (jnp.dot(p_hi, ov_ref[...], preferred_element_type=jnp.float32)
           + jnp.dot(p_lo, ov_ref[...], preferred_element_type=jnp.float32))
    sel = _select_blocks(imp, tpos, n_sb).astype(jnp.bfloat16)

    init = (jnp.full((GROUP, tq, 1), NEG_INF, jnp.float32), jnp.zeros((GROUP, tq, 1), jnp.float32),
            jnp.zeros((GROUP, tq, HEAD_DIM), jnp.float32))

    def sel_step(c, carry):
        k0 = pl.multiple_of(c * SEL_KC, SEL_KC)
        kpos = k0 + lax.broadcasted_iota(jnp.int32, (1, SEL_KC), 1)
        jrow = lax.broadcasted_iota(jnp.int32, (128, 1), 0)
        expand = ((kpos >> 6) == jrow).astype(jnp.bfloat16)
        picked = jnp.dot(sel, expand, preferred_element_type=jnp.float32)
        mask = (picked > 0.5) & (kpos <= tpos)
        return _online_step(q4, ks_ref[0, 0, pl.ds(k0, SEL_KC), :], vs_ref[0, 0, pl.ds(k0, SEL_KC), :], mask, *carry)

    n_kc = (t0 + tq + SEL_KC - 1) // SEL_KC
    m_s, l_s, acc_s = lax.fori_loop(0, n_kc, sel_step, init)
    o_sel = acc_s / jnp.maximum(l_s, 1e-30)

    def win_step(c, carry):
        k0 = pl.multiple_of((i - c) * tq, tq)
        kpos = k0 + lax.broadcasted_iota(jnp.int32, (1, tq), 1)
        d = tpos - kpos
        mask = (d >= 0) & (d < WINDOW)
        return _online_step(q4, kw_ref[0, 0, pl.ds(k0, tq), :], vw_ref[0, 0, pl.ds(k0, tq), :], mask, *carry)

    n_wc = jnp.minimum(i, WINDOW // tq) + 1
    m_w, l_w, acc_w = lax.fori_loop(0, n_wc, win_step, init)
    o_win = acc_w / jnp.maximum(l_w, 1e-30)

    gate = jax.nn.sigmoid(gl_ref[0, 0])
    for g in range(GROUP):
        o = (gate[:, 3 * g:3 * g + 1] * o_cmp[g] + gate[:, 3 * g + 1:3 * g + 2] * o_sel[g]
             + gate[:, 3 * g + 2:3 * g + 3] * o_win[g])
        o_ref[0, :, g * HEAD_DIM:(g + 1) * HEAD_DIM] = o


def _overlap_matrix(n_cmp, n_sb):
    start = np.arange(128) * CMP_STRIDE
    end = start + CMP_BLOCK - 1
    sb = np.arange(128) * SEL_BLOCK
    ov = (start[:, None] < sb[None, :] + SEL_BLOCK) & (end[:, None] >= sb[None, :])
    ov &= (np.arange(128)[:, None] < n_cmp) & (np.arange(128)[None, :] < n_sb)
    return jnp.asarray(ov, jnp.bfloat16)


def _heads_major(x):
    B, T, W = x.shape
    return x.reshape(B, T, N_KV_HEADS, W // N_KV_HEADS).transpose(0, 2, 1, 3)


def _nsa_prompt_pallas(q, kcmp, vcmp, ks, vs, kw, vw, gl):
    B, T, _ = q.shape
    tq = ATT_TQ
    n_cmp = (T - CMP_BLOCK) // CMP_STRIDE + 1
    n_sb = -(-T // SEL_BLOCK)
    assert T % SEL_KC == 0 and kcmp.shape[1] == 128 and n_sb <= 128
    bf = lambda x: _heads_major(x.astype(jnp.bfloat16))
    kv_spec = pl.BlockSpec((1, 1, T, HEAD_DIM), lambda b, k, i: (b, k, 0, 0))
    cmp_spec = pl.BlockSpec((1, 1, 128, HEAD_DIM), lambda b, k, i: (b, k, 0, 0))
    return pl.pallas_call(
        functools.partial(_nsa_prompt_kernel, tq=tq, n_cmp=n_cmp, n_sb=n_sb),
        out_shape=jax.ShapeDtypeStruct((B, T, Q_W), jnp.float32),
        grid=(B, N_KV_HEADS, T // tq),
        in_specs=[pl.BlockSpec((1, tq, GROUP * HEAD_DIM), lambda b, k, i: (b, i, k)),
                  cmp_spec, cmp_spec, kv_spec, kv_spec, kv_spec, kv_spec,
                  pl.BlockSpec((1, 1, tq, 3 * GROUP), lambda b, k, i: (b, k, i, 0)),
                  pl.BlockSpec((128, 128), lambda b, k, i: (0, 0))],
        out_specs=pl.BlockSpec((1, tq, GROUP * HEAD_DIM), lambda b, k, i: (b, i, k)),
        compiler_params=pltpu.CompilerParams(
            dimension_semantics=("parallel", "parallel", "arbitrary"), vmem_limit_bytes=VMEM_LIMIT_BYTES),
    )(q, _heads_major(kcmp), _heads_major(vcmp), bf(ks), bf(vs), bf(kw), bf(vw), _heads_major(gl),
      _overlap_matrix(n_cmp, n_sb))


SSM_N = SSM_GROUPS * SSM_STATE
SSM_LANE_BLK = 512
SSM_TL = 64


def _ssm_kernel(u_ref, h0r_ref, h0i_ref, ar_ref, ai_ref, bm_ref, cr_ref, ci_ref, d_ref,
                y_ref, hr_ref, hi_ref, xr_s, xi_s, *, tl, nb):
    c = pl.program_id(0)

    @pl.when(c == 0)
    def _():
        hr_ref[...] = h0r_ref[...]
        hi_ref[...] = h0i_ref[...]

    u = u_ref[...]
    ub = u.astype(jnp.bfloat16)
    n_grp = SSM_WIDTH // 128
    for j in range(n_grp):
        bu = jnp.dot(ub[:, 128 * j:128 * (j + 1)], bm_ref[j], preferred_element_type=jnp.float32)
        xr_s[:, 512 * j:512 * (j + 1)] = bu[:, :512]
        xi_s[:, 512 * j:512 * (j + 1)] = bu[:, 512:]

    for lb in range(SSM_N // SSM_LANE_BLK):
        sl = slice(lb * SSM_LANE_BLK, (lb + 1) * SSM_LANE_BLK)
        ar = jnp.broadcast_to(ar_ref[:, sl], (8, SSM_LANE_BLK))
        ai = jnp.broadcast_to(ai_ref[:, sl], (8, SSM_LANE_BLK))
        for r in range(nb // 8):
            def step(t, carry):
                hr, hi = carry
                row = pl.multiple_of(t * nb + r * 8, 8)
                xr = xr_s[pl.ds(row, 8), sl]
                xi = xi_s[pl.ds(row, 8), sl]
                nr = ar * hr - ai * hi + xr
                ni = ar * hi + ai * hr + xi
                xr_s[pl.ds(row, 8), sl] = nr
                xi_s[pl.ds(row, 8), sl] = ni
                return nr, ni

            hr, hi = lax.fori_loop(0, tl, step, (hr_ref[r * 8:(r + 1) * 8, sl], hi_ref[r * 8:(r + 1) * 8, sl]))
            hr_ref[r * 8:(r + 1) * 8, sl] = hr
            hi_ref[r * 8:(r + 1) * 8, sl] = hi

    for j in range(n_grp):
        yr = jnp.dot(xr_s[:, 512 * j:512 * (j + 1)].astype(jnp.bfloat16), cr_ref[j], preferred_element_type=jnp.float32)
        yi = jnp.dot(xi_s[:, 512 * j:512 * (j + 1)].astype(jnp.bfloat16), ci_ref[j], preferred_element_type=jnp.float32)
        y_ref[:, 128 * j:128 * (j + 1)] = yr - yi + d_ref[:, 128 * j:128 * (j + 1)] * u[:, 128 * j:128 * (j + 1)]


def _ssm_params(a_re, a_im, log_dt, b_re, b_im, c_re, c_im, d_skip):
    dt = jnp.exp(log_dt)[:, None]
    mag = jnp.exp(dt * a_re)
    ab_re = mag * jnp.cos(dt * a_im)
    ab_im = mag * jnp.sin(dt * a_im)
    den = a_re * a_re + a_im * a_im
    zr = ((ab_re - 1.0) * a_re + ab_im * a_im) / den
    zi = (ab_im * a_re - (ab_re - 1.0) * a_im) / den
    bb_re = zr[..., None] * b_re - zi[..., None] * b_im
    bb_im = zr[..., None] * b_im + zi[..., None] * b_re
    n_grp = SSM_WIDTH // 128
    gpl = 128 // SSM_GROUP
    eye = jnp.eye(gpl, dtype=jnp.float32)

    def b_blocks(bb):
        x = bb.reshape(n_grp, gpl, SSM_STATE, SSM_GROUP)
        return jnp.einsum('jgpc,gh->jgchp', x, eye).reshape(n_grp, 128, gpl * SSM_STATE)

    def c_blocks(cc):
        x = cc.reshape(n_grp, gpl, SSM_GROUP, SSM_STATE)
        return jnp.einsum('jgcp,gh->jgphc', x, eye).reshape(n_grp, gpl * SSM_STATE, 128)

    bm = jnp.concatenate([b_blocks(bb_re), b_blocks(bb_im)], axis=-1).astype(jnp.bfloat16)
    return (ab_re.reshape(1, SSM_N), ab_im.reshape(1, SSM_N), bm,
            c_blocks(c_re).astype(jnp.bfloat16), c_blocks(c_im).astype(jnp.bfloat16), d_skip.reshape(1, SSM_WIDTH))


def _ssm_pallas(u, h0_re, h0_im, params):
    B, T, _ = u.shape
    ab_re, ab_im, bm, cr, ci, d = params
    tl = _pick_tile(T, (SSM_TL,))
    u_tb = u.transpose(1, 0, 2).reshape(T * B, SSM_WIDTH)
    full = lambda a: pl.BlockSpec(a.shape, lambda c: (0,) * a.ndim)
    h0r = h0_re.reshape(B, SSM_N)
    h0i = h0_im.reshape(B, SSM_N)
    consts = (h0r, h0i, ab_re, ab_im, bm, cr, ci, d)
    y, hr, hi = pl.pallas_call(
        functools.partial(_ssm_kernel, tl=tl, nb=B),
        out_shape=(jax.ShapeDtypeStruct((T * B, SSM_WIDTH), jnp.float32),
                   jax.ShapeDtypeStruct((B, SSM_N), jnp.float32), jax.ShapeDtypeStruct((B, SSM_N), jnp.float32)),
        grid=(T // tl,),
        in_specs=[pl.BlockSpec((tl * B, SSM_WIDTH), lambda c: (c, 0))] + [full(a) for a in consts],
        out_specs=(pl.BlockSpec((tl * B, SSM_WIDTH), lambda c: (c, 0)),
                   pl.BlockSpec((B, SSM_N), lambda c: (0, 0)), pl.BlockSpec((B, SSM_N), lambda c: (0, 0))),
        scratch_shapes=[pltpu.VMEM((tl * B, SSM_N), jnp.float32), pltpu.VMEM((tl * B, SSM_N), jnp.float32)],
        compiler_params=pltpu.CompilerParams(dimension_semantics=("arbitrary",), vmem_limit_bytes=VMEM_LIMIT_BYTES),
    )(u_tb, *consts)
    y = y.reshape(T, B, SSM_WIDTH).transpose(1, 0, 2)
    return y, hr.reshape(B, SSM_GROUPS, SSM_STATE), hi.reshape(B, SSM_GROUPS, SSM_STATE)


def _rmsnorm(x, g):
    xf = x.astype(jnp.float32)
    y = xf * lax.rsqrt(jnp.mean(xf * xf, axis=-1, keepdims=True) + EPS)
    return (y * g.astype(jnp.float32)).astype(x.dtype)


def _rope(x, pos):
    half = HEAD_DIM // 2
    inv = jnp.float32(ROPE_THETA) ** (-jnp.arange(half, dtype=jnp.float32) / half)
    ang = pos.astype(jnp.float32)[:, None] * inv[None, :]
    cos = jnp.cos(ang)[:, None, :]
    sin = jnp.sin(ang)[:, None, :]
    x1, x2 = x[..., :half], x[..., half:]
    return jnp.concatenate([x1 * cos - x2 * sin, x2 * cos + x1 * sin], axis=-1)


def _masked_softmax(s, mask):
    s = jnp.where(mask, s.astype(jnp.float32), NEG_INF)
    m = jnp.max(s, axis=-1, keepdims=True)
    e = jnp.where(mask, jnp.exp(s - m), 0.0)
    return e / jnp.maximum(jnp.sum(e, axis=-1, keepdims=True), 1e-30)


def _attend(q, k, v, mask):
    s = jnp.einsum('btkgd,bskd->bkgts', q, k) * SCALE
    p = _masked_softmax(s, mask)
    o = jnp.einsum('bkgts,bskd->btkgd', p.astype(v.dtype), v)
    return o, p


def _window_mask(q_pos, k_pos):
    d = q_pos[:, None] - k_pos[None, :]
    return (d >= 0) & (d < WINDOW) & (k_pos[None, :] >= 0)


def _compress(x, w1, pe, w2):
    B, L, K, D = x.shape
    xt = x.transpose(0, 2, 1, 3).reshape(B * K, L, D)
    hdn = lax.conv_general_dilated(xt, w1.astype(xt.dtype), (CMP_STRIDE,), 'VALID',
                                   dimension_numbers=('NWC', 'WIO', 'NWC'))
    hdn = hdn + jnp.einsum('ld,ldf->f', pe, w1)
    out = jax.nn.gelu(hdn) @ w2
    n = out.shape[1]
    return out.reshape(B, K, n, D).transpose(0, 2, 1, 3)


def _cmp_branch(q, kc_full, vc_full, q_pos, wk1, pek, wk2, wv1, pev, wv2):
    kcmp = _compress(kc_full, wk1, pek, wk2)
    vcmp = _compress(vc_full, wv1, pev, wv2)
    n_cmp = kcmp.shape[1]
    start = jnp.arange(n_cmp) * CMP_STRIDE
    end = start + CMP_BLOCK - 1
    kcmp = _rope(kcmp, end)
    o, p = _attend(q, kcmp, vcmp, end[None, :] <= q_pos[:, None])
    L = kc_full.shape[1]
    n_sb = -(-L // SEL_BLOCK)
    sb_start = jnp.arange(n_sb) * SEL_BLOCK
    overlap = ((start[:, None] < sb_start[None, :] + SEL_BLOCK) & (end[:, None] >= sb_start[None, :])).astype(jnp.float32)
    imp = jnp.einsum('bkgtn,nj->btkj', p, overlap)
    blk = jnp.arange(n_sb)
    forced = (blk[None, :] == 0) | (blk[None, :] == (q_pos // SEL_BLOCK)[:, None])
    causal = sb_start[None, :] <= q_pos[:, None]
    imp = jnp.where(forced[None, :, None, :], FORCE_SCORE, imp)
    imp = jnp.where(causal[None, :, None, :], imp, NEG_INF)
    top_v, top_i = lax.top_k(imp, min(N_SEL, n_sb))
    return o, top_i, top_v > 0.5 * NEG_INF


def _sel_attend(q, idx, valid, q_pos, fetch):
    pos = idx[..., None] * SEL_BLOCK + jnp.arange(SEL_BLOCK)
    k, v = fetch(pos)
    B, Tc, K, N, S = pos.shape
    mask = (valid[..., None] & (pos <= q_pos[None, :, None, None, None])).reshape(B, Tc, K, N * S)
    k = k.reshape(B, Tc, K, N * S, HEAD_DIM)
    v = v.reshape(B, Tc, K, N * S, HEAD_DIM)
    s = jnp.einsum('btkgd,btksd->btkgs', q, k) * SCALE
    p = _masked_softmax(s, mask[:, :, :, None, :])
    return jnp.einsum('btkgs,btksd->btkgd', p.astype(v.dtype), v)


def _to_chunks(x, c):
    B, T = x.shape[:2]
    return x.reshape((B, T // c, c) + x.shape[2:]).swapaxes(0, 1)


def _from_chunks(x):
    nc, B, c = x.shape[:3]
    return x.swapaxes(0, 1).reshape((B, nc * c) + x.shape[3:])


def _batch_head_index(B):
    bi = jnp.arange(B)[:, None, None, None, None]
    hd = jnp.arange(N_KV_HEADS)[None, None, :, None, None]
    return bi, hd


def _nsa_prompt(q, kc, vc, ks, vs, kw, vw, cmpw):
    B, T = q.shape[:2]
    pos = jnp.arange(T)
    o_cmp, idx, valid = _cmp_branch(q, kc, vc, pos, *cmpw)
    bi, hd = _batch_head_index(B)

    def fetch(p):
        pc = jnp.clip(p, 0, T - 1)
        return ks[bi, pc, hd], vs[bi, pc, hd]

    qb = Q_BLOCK if T % Q_BLOCK == 0 else T
    o_sel = _from_chunks(lax.map(lambda a: _sel_attend(a[0], a[1], a[2], a[3], fetch),
                                 (_to_chunks(q, qb), _to_chunks(idx, qb), _to_chunks(valid, qb), pos.reshape(-1, qb))))
    kp = jnp.pad(kw, ((0, 0), (WINDOW, 0), (0, 0), (0, 0)))
    vp = jnp.pad(vw, ((0, 0), (WINDOW, 0), (0, 0), (0, 0)))

    def win_block(a):
        c, qc = a
        start = c * qb
        kb = lax.dynamic_slice_in_dim(kp, start, WINDOW + qb, axis=1)
        vb = lax.dynamic_slice_in_dim(vp, start, WINDOW + qb, axis=1)
        qpos = start + jnp.arange(qb)
        kpos = start - WINDOW + jnp.arange(WINDOW + qb)
        return _attend(qc, kb, vb, _window_mask(qpos, kpos))[0]

    o_win = _from_chunks(lax.map(win_block, (jnp.arange(T // qb), _to_chunks(q, qb))))
    n_keep = min(WINDOW, T)
    rows = (jnp.stack([kc, vc], axis=2), jnp.stack([ks, vs], axis=2), jnp.stack([kw, vw], axis=2)[:, T - n_keep:])
    return o_cmp, o_sel, o_win, rows


def _nsa_sample(q, kc, vc, ks, vs, kw, vw, cmpw, cmp_pool, sel_pool, win_buf, page_table):
    B, T = q.shape[:2]
    past = page_table.shape[1] * PAGE_SIZE
    pos = past + jnp.arange(T)
    past_cmp = cmp_pool[page_table].reshape((B, past) + cmp_pool.shape[2:])
    kc_full = jnp.concatenate([past_cmp[:, :, 0], kc], axis=1)
    vc_full = jnp.concatenate([past_cmp[:, :, 1], vc], axis=1)
    o_cmp, idx, valid = _cmp_branch(q, kc_full, vc_full, pos, *cmpw)
    sel_rows = sel_pool.reshape((-1,) + sel_pool.shape[2:])
    bi, hd = _batch_head_index(B)

    def fetch(p):
        pc = jnp.clip(p, 0, past - 1)
        phys = page_table[bi, pc // PAGE_SIZE] * PAGE_SIZE + pc % PAGE_SIZE
        pn = jnp.clip(p - past, 0, T - 1)
        in_past = (p < past)[..., None]
        k = jnp.where(in_past, sel_rows[phys, 0, hd], ks[bi, pn, hd])
        v = jnp.where(in_past, sel_rows[phys, 1, hd], vs[bi, pn, hd])
        return k, v

    o_sel = _sel_attend(q, idx, valid, pos, fetch)
    wb = win_buf.shape[1]
    win_all = jnp.concatenate([win_buf, jnp.stack([kw, vw], axis=2)], axis=1)
    kpos = past - wb + jnp.arange(wb + T)
    o_win, _ = _attend(q, win_all[:, :, 0], win_all[:, :, 1], _window_mask(pos, kpos))
    rows = (jnp.stack([kc, vc], axis=2), jnp.stack([ks, vs], axis=2), win_all[:, T:])
    return o_cmp, o_sel, o_win, rows


def _cmul_combine(e1, e2):
    ar1, ai1, br1, bi1 = e1
    ar2, ai2, br2, bi2 = e2
    return (ar2 * ar1 - ai2 * ai1, ar2 * ai1 + ai2 * ar1,
            ar2 * br1 - ai2 * bi1 + br2, ar2 * bi1 + ai2 * br1 + bi2)


def _ssm_scan(u, h0_re, h0_im, a_re, a_im, log_dt, b_re, b_im, c_re, c_im, d_skip):
    B, T, _ = u.shape
    uf = u.reshape(B, T, SSM_GROUPS, SSM_GROUP)
    ar = a_re
    ai = a_im
    dt = jnp.exp(log_dt)[:, None]
    mag = jnp.exp(dt * ar)
    ab_re = mag * jnp.cos(dt * ai)
    ab_im = mag * jnp.sin(dt * ai)
    den = ar * ar + ai * ai
    zr = ((ab_re - 1.0) * ar + ab_im * ai) / den
    zi = (ab_im * ar - (ab_re - 1.0) * ai) / den
    bb_re = zr[..., None] * b_re - zi[..., None] * b_im
    bb_im = zr[..., None] * b_im + zi[..., None] * b_re
    bu_re = jnp.einsum('gpc,btgc->tbgp', bb_re, uf)
    bu_im = jnp.einsum('gpc,btgc->tbgp', bb_im, uf)
    chunk = SSM_CHUNK if T % SSM_CHUNK == 0 else T
    nc = T // chunk
    shp = (chunk, B, SSM_GROUPS, SSM_STATE)
    bu_re = bu_re.reshape((nc,) + shp)
    bu_im = bu_im.reshape((nc,) + shp)
    a_re_c = jnp.broadcast_to(ab_re, shp)
    a_im_c = jnp.broadcast_to(ab_im, shp)

    def step(carry, xs):
        hr0, hi0 = carry
        xr, xi = xs
        pr, pim, sr, si = lax.associative_scan(_cmul_combine, (a_re_c, a_im_c, xr, xi), axis=0)
        hr = sr + pr * hr0 - pim * hi0
        hi = si + pr * hi0 + pim * hr0
        y = jnp.einsum('gcp,tbgp->tbgc', c_re, hr) - jnp.einsum('gcp,tbgp->tbgc', c_im, hi)
        return (hr[-1], hi[-1]), y

    (hr, hi), y = lax.scan(step, (h0_re, h0_im), (bu_re, bu_im))
    y = y.reshape(T, B, SSM_GROUPS, SSM_GROUP).transpose(1, 0, 2, 3) + d_skip * uf
    return y.reshape(B, T, SSM_WIDTH), hr, hi


def _block(h, p_l, pos, lw, nsa_fn, h0_re, h0_im, conv_prefix):
    (g_attn, w_in, g_q, g_kc, g_ks, g_kw, wk1, pek, wk2, wv1, pev, wv2,
     a_re, a_im, log_dt, b_re, b_im, c_re, c_im, d_skip,
     w_a, w_glu1, w_glu2, w_o, g_ffn, w_up, conv_w, conv_b, w_down,
     g_ple, w_ple_gate, w_ple) = lw
    B, T, _ = h.shape
    N = B * T

    def mm(x, w, g=None):
        return _matmul(x.reshape(N, x.shape[-1]), w, g).reshape(B, T, w.shape[1])

    z = mm(h, w_in, g_attn)
    pts = [int(s) for s in np.cumsum(SPLIT_SIZES)[:-1]]
    q, kc, vc, ks, vs, kw, vw, gl, u, ga, gb = jnp.split(z, pts, axis=-1)

    def heads(t):
        return t.reshape(B, T, -1, HEAD_DIM)

    q = _rope(_rmsnorm(heads(q), g_q), pos).reshape(B, T, N_KV_HEADS, GROUP, HEAD_DIM)
    kc = _rmsnorm(heads(kc), g_kc)
    ks = _rope(_rmsnorm(heads(ks), g_ks), pos)
    kw = _rope(_rmsnorm(heads(kw), g_kw), pos)
    vc, vs, vw = heads(vc), heads(vs), heads(vw)
    cmpw = (wk1, pek, wk2, wv1, pev, wv2)
    if nsa_fn is None:
        flat = lambda t: t.reshape(B, T, KV_W)
        kcmp, vcmp = _compress_pallas(flat(kc), flat(vc), cmpw)
        o = _nsa_prompt_pallas(q.reshape(B, T, Q_W), kcmp, vcmp, flat(ks), flat(vs), flat(kw), flat(vw), gl)
        n_keep = min(WINDOW, T)
        rows = (jnp.stack([kc, vc], axis=2), jnp.stack([ks, vs], axis=2),
                jnp.stack([kw, vw], axis=2)[:, T - n_keep:])
    else:
        o_cmp, o_sel, o_win, rows = nsa_fn(q, kc, vc, ks, vs, kw, vw, cmpw)
        bg = jax.nn.sigmoid(gl).reshape(B, T, N_KV_HEADS, GROUP, 3)
        o = bg[..., 0:1] * o_cmp + bg[..., 1:2] * o_sel + bg[..., 2:3] * o_win
    a_out = mm(o.reshape(B, T, Q_W), w_a)
    y, hr, hi = _ssm_pallas(u, h0_re, h0_im, _ssm_params(a_re, a_im, log_dt, b_re, b_im, c_re, c_im, d_skip))
    yg = jax.nn.gelu(y)
    b_out = mm(yg, w_glu1) * jax.nn.sigmoid(mm(yg, w_glu2))
    h = h + mm(jax.nn.sigmoid(ga) * a_out + jax.nn.sigmoid(gb) * b_out, w_o)
    up = mm(h, w_up, g_ffn)
    gp, val = jnp.split(up, 2, axis=-1)
    ext = jnp.concatenate([conv_prefix, gp], axis=1)
    conv = conv_b
    for j in range(CONV_W):
        conv = conv + conv_w[j] * ext[:, j:j + T]
    h = h + mm(jax.nn.gelu(conv) * val, w_down)
    h = h + jax.nn.sigmoid(mm(h, w_ple_gate, g_ple)) * mm(p_l, w_ple)
    return h, rows, hr, hi, ext[:, T:]


def kernel(x_prompt, x_sample, cache_cmp, cache_sel, cache_win, state_ssm_re, state_ssm_im, state_conv, page_table, p_prompt, p_sample, g_attn, w_in, g_q, g_kc, g_ks, g_kw, cmp_wk1, cmp_pek, cmp_wk2, cmp_wv1, cmp_pev, cmp_wv2, ssm_a_re, ssm_a_im, ssm_log_dt, ssm_b_re, ssm_b_im, ssm_c_re, ssm_c_im, ssm_d, w_a, w_glu1, w_glu2, w_o, g_ffn, w_up, conv_w, conv_b, w_down, g_ple, w_ple_gate, w_ple):
    Bp, Tp = x_prompt.shape[:2]
    Ts = x_sample.shape[1]
    depth = w_in.shape[0]
    past = page_table.shape[1] * PAGE_SIZE
    pos_p = jnp.arange(Tp)
    pos_s = past + jnp.arange(Ts)
    zeros_h = jnp.zeros((Bp, SSM_GROUPS, SSM_STATE), x_prompt.dtype)
    zeros_c = jnp.zeros((Bp, CONV_W - 1, D_FF), x_prompt.dtype)
    layer_w = (g_attn, w_in, g_q, g_kc, g_ks, g_kw, cmp_wk1, cmp_pek, cmp_wk2, cmp_wv1, cmp_pev, cmp_wv2,
               ssm_a_re, ssm_a_im, ssm_log_dt, ssm_b_re, ssm_b_im, ssm_c_re, ssm_c_im, ssm_d,
               w_a, w_glu1, w_glu2, w_o, g_ffn, w_up, conv_w, conv_b, w_down, g_ple, w_ple_gate, w_ple)
    st = [[] for _ in range(12)]
    hp, hs = x_prompt, x_sample
    for i in range(depth):
        lw = [w[i] for w in layer_w]
        hp, rows, hr, hi, cv = _block(hp, p_prompt[i], pos_p, lw, None, zeros_h, zeros_h, zeros_c)
        for j, a in enumerate(list(rows) + [hr, hi, cv]):
            st[j].append(a)
        nsa_s = functools.partial(_nsa_sample, cmp_pool=cache_cmp[i], sel_pool=cache_sel[i],
                                  win_buf=cache_win[i], page_table=page_table)
        hs, rows, hr, hi, cv = _block(hs, p_sample[i], pos_s, lw, nsa_s, state_ssm_re[i], state_ssm_im[i], state_conv[i])
        for j, a in enumerate(list(rows) + [hr, hi, cv]):
            st[6 + j].append(a)
    return (hp, hs) + tuple(jnp.stack(s) for s in st)
```

```python
import functools
import math

import numpy as np
import jax
import jax.numpy as jnp
from jax import lax
from jax.experimental import pallas as pl
from jax.experimental.pallas import tpu as pltpu

D_MODEL = 1024
N_HEADS = 8
N_KV_HEADS = 2
HEAD_DIM = 64
GROUP = N_HEADS // N_KV_HEADS
Q_W = N_HEADS * HEAD_DIM
KV_W = N_KV_HEADS * HEAD_DIM
CMP_BLOCK = 32
CMP_STRIDE = 16
SEL_BLOCK = 64
N_SEL = 8
WINDOW = 512
Q_BLOCK = 128
PAGE_SIZE = 128
ROPE_THETA = 10000.0
SSM_WIDTH = D_MODEL // 2
SSM_GROUP = 16
SSM_GROUPS = SSM_WIDTH // SSM_GROUP
SSM_STATE = 64
SSM_CHUNK = 128
D_FF = 11 * D_MODEL // 4
CONV_W = 3
EPS = 1e-6
NEG_INF = -1e30
FORCE_SCORE = 1e9
SCALE = HEAD_DIM ** -0.5
SPLIT_SIZES = (Q_W, KV_W, KV_W, KV_W, KV_W, KV_W, KV_W, 3 * N_HEADS, SSM_WIDTH, D_MODEL, D_MODEL)

VMEM_LIMIT_BYTES = 56 * 1024 * 1024


def _mm_kernel(x_ref, g_ref, w_ref, o_ref, *, norm):
    x = x_ref[...]
    if norm:
        x = x * lax.rsqrt(jnp.mean(x * x, axis=-1, keepdims=True) + EPS) * g_ref[...]
    o_ref[...] = jnp.dot(x.astype(jnp.bfloat16), w_ref[...], preferred_element_type=jnp.float32)


def _pick_tile(n, cands):
    for c in cands:
        if n % c == 0:
            return c
    return n


def _matmul(x, w, g=None):
    n, k = x.shape
    m = w.shape[1]
    tm = _pick_tile(n, (512, 256, 128))
    tn = _pick_tile(m, (1024, 512, 256, 128))
    norm = g is not None
    if g is None:
        g = jnp.ones((k,), jnp.float32)
    return pl.pallas_call(
        functools.partial(_mm_kernel, norm=norm),
        out_shape=jax.ShapeDtypeStruct((n, m), jnp.float32),
        grid=(n // tm, m // tn),
        in_specs=[pl.BlockSpec((tm, k), lambda i, j: (i, 0)),
                  pl.BlockSpec((1, k), lambda i, j: (0, 0)),
                  pl.BlockSpec((k, tn), lambda i, j: (0, j))],
        out_specs=pl.BlockSpec((tm, tn), lambda i, j: (i, j)),
        compiler_params=pltpu.CompilerParams(
            dimension_semantics=("parallel", "parallel"), vmem_limit_bytes=VMEM_LIMIT_BYTES),
        name="norm_matmul" if norm else "matmul",
    )(x, g.reshape(1, k).astype(jnp.float32), w.astype(jnp.bfloat16))


HALF_ROWS = CMP_BLOCK // CMP_STRIDE
assert HALF_ROWS == 2


def _rope_lanes(x, cos, sin_signed):
    w = x.shape[-1]
    half = HEAD_DIM // 2
    lane = lax.broadcasted_iota(jnp.int32, x.shape, x.ndim - 1)
    first = (lane % HEAD_DIM) < half
    partner = jnp.where(first, pltpu.roll(x, w - half, x.ndim - 1), pltpu.roll(x, half, x.ndim - 1))
    return x * cos + partner * sin_signed


def _compress_rows(load, w_ref, pe_ref, w1_ref, w2_ref, n_half):
    acc = jnp.zeros((n_half, 4 * 128), jnp.float32)
    for j in range(CMP_STRIDE):
        acc = acc + jnp.dot(load(j).astype(jnp.bfloat16), w_ref[j], preferred_element_type=jnp.float32)
    pa = acc[:, :256]
    pb = pltpu.roll(acc[:, 256:], n_half - 1, 0)
    bias = jnp.dot(pe_ref[...].astype(jnp.bfloat16), w1_ref[...], preferred_element_type=jnp.float32)[0:1]
    bias2 = jnp.concatenate([bias, bias], axis=1)
    hdn = jax.nn.gelu(pa + pb + bias2)
    return jnp.dot(hdn.astype(jnp.bfloat16), w2_ref[...], preferred_element_type=jnp.float32)


def _compress_kernel(xk_ref, xv_ref, wk_ref, wv_ref, pek_ref, pev_ref, w1k_ref, w1v_ref, w2k_ref, w2v_ref,
                     cos_ref, sin_ref, ko_ref, vo_ref, *, n_half, n_cmp):
    row = lax.broadcasted_iota(jnp.int32, (n_half, KV_W), 0)
    k = _compress_rows(lambda j: xk_ref[0, pl.ds(j, n_half, stride=CMP_STRIDE), :],
                       wk_ref, pek_ref, w1k_ref, w2k_ref, n_half)
    k = _rope_lanes(k, cos_ref[...], sin_ref[...])
    v = _compress_rows(lambda j: xv_ref[0, pl.ds(j, n_half, stride=CMP_STRIDE), :],
                       wv_ref, pev_ref, w1v_ref, w2v_ref, n_half)
    ko_ref[0] = jnp.where(row < n_cmp, k, 0.0).astype(ko_ref.dtype)
    vo_ref[0] = jnp.where(row < n_cmp, v, 0.0).astype(vo_ref.dtype)


def _blockdiag2(w):
    z = jnp.zeros_like(w)
    return jnp.concatenate([jnp.concatenate([w, z], axis=-1), jnp.concatenate([z, w], axis=-1)], axis=-2)


def _compress_weights(w1, pe, w2):
    bd = _blockdiag2(w1)
    wcat = jnp.concatenate([bd[:CMP_STRIDE], bd[CMP_STRIDE:]], axis=-1).astype(jnp.bfloat16)
    pe_flat = jnp.broadcast_to(pe.reshape(1, -1), (8, pe.size))
    w1_flat = w1.reshape(-1, w1.shape[-1]).astype(jnp.bfloat16)
    w2bd = _blockdiag2(w2).astype(jnp.bfloat16)
    return wcat, pe_flat, w1_flat, w2bd


def _rope_tables(pos, reps):
    half = HEAD_DIM // 2
    inv = jnp.float32(ROPE_THETA) ** (-jnp.arange(half, dtype=jnp.float32) / half)
    ang = pos.astype(jnp.float32)[:, None] * inv[None, :]
    cos = jnp.cos(ang)
    sin = jnp.sin(ang)
    return (jnp.tile(jnp.concatenate([cos, cos], axis=-1), (1, reps)),
            jnp.tile(jnp.concatenate([-sin, sin], axis=-1), (1, reps)))


def _compress_pallas(krows, vrows, cmpw):
    wk1, pek, wk2, wv1, pev, wv2 = cmpw
    B, L, _ = krows.shape
    n_half = L // CMP_STRIDE
    n_cmp = n_half - 1
    wk, pekf, w1k, w2k = _compress_weights(wk1, pek, wk2)
    wv, pevf, w1v, w2v = _compress_weights(wv1, pev, wv2)
    end = jnp.arange(n_half) * CMP_STRIDE + CMP_BLOCK - 1
    cos, sin = _rope_tables(end, N_KV_HEADS)
    full = lambda a: pl.BlockSpec(a.shape, lambda b: (0,) * a.ndim)
    consts = (wk, wv, pekf, pevf, w1k, w1v, w2k, w2v, cos, sin)
    return pl.pallas_call(
        functools.partial(_compress_kernel, n_half=n_half, n_cmp=n_cmp),
        out_shape=(jax.ShapeDtypeStruct((B, n_half, KV_W), jnp.bfloat16),) * 2,
        grid=(B,),
        in_specs=[pl.BlockSpec((1, L, KV_W), lambda b: (b, 0, 0))] * 2 + [full(a) for a in consts],
        out_specs=(pl.BlockSpec((1, n_half, KV_W), lambda b: (b, 0, 0)),) * 2,
        compiler_params=pltpu.CompilerParams(dimension_semantics=("parallel",), vmem_limit_bytes=VMEM_LIMIT_BYTES),
        name="compress",
    )(krows, vrows, *consts)


ATT_TQ = 128
SEL_KC = 512
BIG_NEG = -3.0e38


def _softmax_rows(s, mask):
    s = jnp.where(mask, s, NEG_INF)
    m = jnp.max(s, axis=-1, keepdims=True)
    e = jnp.where(mask, jnp.exp(s - m), 0.0)
    return e / jnp.maximum(jnp.sum(e, axis=-1, keepdims=True), 1e-30)


def _select_blocks(imp, tpos, n_sb):
    tq = imp.shape[0]
    jl = lax.broadcasted_iota(jnp.int32, (tq, 128), 1)
    jf = jl.astype(jnp.float32)
    forced = (jl == 0) | (jl == (tpos >> 6))
    imp = jnp.where(forced, FORCE_SCORE, imp)
    imp = jnp.where(jl * SEL_BLOCK <= tpos, imp, NEG_INF)
    imp = jnp.where(jl < n_sb, imp, BIG_NEG)
    sel = jnp.zeros((tq, 128), jnp.float32)
    for _ in range(N_SEL):
        m = jnp.max(imp, axis=-1, keepdims=True)
        first = jnp.min(jnp.where(imp == m, jf, 1e9), axis=-1, keepdims=True)
        hit = jf == first
        sel = jnp.where(hit & (m > 0.5 * NEG_INF), 1.0, sel)
        imp = jnp.where(hit, BIG_NEG, imp)
    return sel


def _online_step(q4, k, v, mask, m, l, acc):
    tq = mask.shape[0]
    s = lax.dot_general(q4, k, (((1,), (1,)), ((), ())), preferred_element_type=jnp.float32)
    s = jnp.where(mask[None], s.reshape(4, tq, -1), NEG_INF)
    m_new = jnp.maximum(m, jnp.max(s, axis=-1, keepdims=True))
    alpha = jnp.exp(m - m_new)
    p = jnp.exp(s - m_new)
    l = alpha * l + jnp.sum(p, axis=-1, keepdims=True)
    pv = jnp.dot(p.reshape(4 * tq, -1).astype(jnp.bfloat16), v, preferred_element_type=jnp.float32)
    acc = alpha * acc + pv.reshape(4, tq, HEAD_DIM)
    return m_new, l, acc


def _nsa_prompt_kernel(q_ref, kcmp_ref, vcmp_ref, ks_ref, vs_ref, kw_ref, vw_ref, gl_ref, ov_ref, o_ref,
                       *, tq, n_cmp, n_sb):
    i = pl.program_id(2)
    t0 = i * tq
    qf = q_ref[0] * SCALE
    q4 = jnp.concatenate([qf[:, g * HEAD_DIM:(g + 1) * HEAD_DIM] for g in range(GROUP)], axis=0).astype(jnp.bfloat16)
    tpos = t0 + lax.broadcasted_iota(jnp.int32, (tq, 1), 0)

    nl = lax.broadcasted_iota(jnp.int32, (tq, 128), 1)
    maskc = ((nl * CMP_STRIDE + (CMP_BLOCK - 1)) <= tpos) & (nl < n_cmp)
    sc = lax.dot_general(q4, kcmp_ref[0, 0], (((1,), (1,)), ((), ())), preferred_element_type=jnp.float32)
    pc = _softmax_rows(sc.reshape(GROUP, tq, 128), maskc[None])
    o_cmp = jnp.dot(pc.reshape(GROUP * tq, 128).astype(jnp.bfloat16), vcmp_ref[0, 0],
                    preferred_element_type=jnp.float32).reshape(GROUP, tq, HEAD_DIM)
    psum = pc[0] + pc[1] + pc[2] + pc[3]
    p_hi = psum.astype(jnp.bfloat16)
    p_lo = (psum - p_hi.astype(jnp.float32)).astype(jnp.bfloat16)
    imp = (jnp.dot(p_hi, ov_ref[...], preferred_element_type=jnp.float32)
           + jnp.dot(p_lo, ov_ref[...], preferred_element_type=jnp.float32))
    sel = _select_blocks(imp, tpos, n_sb).astype(jnp.bfloat16)

    init = (jnp.full((GROUP, tq, 1), NEG_INF, jnp.float32), jnp.zeros((GROUP, tq, 1), jnp.float32),
            jnp.zeros((GROUP, tq, HEAD_DIM), jnp.float32))

    def sel_step(c, carry):
        k0 = pl.multiple_of(c * SEL_KC, SEL_KC)
        kpos = k0 + lax.broadcasted_iota(jnp.int32, (1, SEL_KC), 1)
        jrow = lax.broadcasted_iota(jnp.int32, (128, 1), 0)
        expand = ((kpos >> 6) == jrow).astype(jnp.bfloat16)
        picked = jnp.dot(sel, expand, preferred_element_type=jnp.float32)
        mask = (picked > 0.5) & (kpos <= tpos)
        return _online_step(q4, ks_ref[0, 0, pl.ds(k0, SEL_KC), :], vs_ref[0, 0, pl.ds(k0, SEL_KC), :], mask, *carry)

    n_kc = (t0 + tq + SEL_KC - 1) // SEL_KC
    m_s, l_s, acc_s = lax.fori_loop(0, n_kc, sel_step, init)
    o_sel = acc_s / jnp.maximum(l_s, 1e-30)

    def win_step(c, carry):
        k0 = pl.multiple_of((i - c) * tq, tq)
        kpos = k0 + lax.broadcasted_iota(jnp.int32, (1, tq), 1)
        d = tpos - kpos
        mask = (d >= 0) & (d < WINDOW)
        return _online_step(q4, kw_ref[0, 0, pl.ds(k0, tq), :], vw_ref[0, 0, pl.ds(k0, tq), :], mask, *carry)

    n_wc = jnp.minimum(i, WINDOW // tq) + 1
    m_w, l_w, acc_w = lax.fori_loop(0, n_wc, win_step, init)
    o_win = acc_w / jnp.maximum(l_w, 1e-30)

    gate = jax.nn.sigmoid(gl_ref[0, 0])
    for g in range(GROUP):
        o = (gate[:, 3 * g:3 * g + 1] * o_cmp[g] + gate[:, 3 * g + 1:3 * g + 2] * o_sel[g]
             + gate[:, 3 * g + 2:3 * g + 3] * o_win[g])
        o_ref[0, :, g * HEAD_DIM:(g + 1) * HEAD_DIM] = o


def _overlap_matrix(n_cmp, n_sb):
    start = np.arange(128) * CMP_STRIDE
    end = start + CMP_BLOCK - 1
    sb = np.arange(128) * SEL_BLOCK
    ov = (start[:, None] < sb[None, :] + SEL_BLOCK) & (end[:, None] >= sb[None, :])
    ov &= (np.arange(128)[:, None] < n_cmp) & (np.arange(128)[None, :] < n_sb)
    return jnp.asarray(ov, jnp.bfloat16)


def _heads_major(x):
    B, T, W = x.shape
    return x.reshape(B, T, N_KV_HEADS, W // N_KV_HEADS).transpose(0, 2, 1, 3)


def _nsa_prompt_pallas(q, kcmp, vcmp, ks, vs, kw, vw, gl):
    B, T, _ = q.shape
    tq = ATT_TQ
    n_cmp = (T - CMP_BLOCK) // CMP_STRIDE + 1
    n_sb = -(-T // SEL_BLOCK)
    assert T % SEL_KC == 0 and kcmp.shape[1] == 128 and n_sb <= 128
    bf = lambda x: _heads_major(x.astype(jnp.bfloat16))
    kv_spec = pl.BlockSpec((1, 1, T, HEAD_DIM), lambda b, k, i: (b, k, 0, 0))
    cmp_spec = pl.BlockSpec((1, 1, 128, HEAD_DIM), lambda b, k, i: (b, k, 0, 0))
    return pl.pallas_call(
        functools.partial(_nsa_prompt_kernel, tq=tq, n_cmp=n_cmp, n_sb=n_sb),
        out_shape=jax.ShapeDtypeStruct((B, T, Q_W), jnp.float32),
        grid=(B, N_KV_HEADS, T // tq),
        in_specs=[pl.BlockSpec((1, tq, GROUP * HEAD_DIM), lambda b, k, i: (b, i, k)),
                  cmp_spec, cmp_spec, kv_spec, kv_spec, kv_spec, kv_spec,
                  pl.BlockSpec((1, 1, tq, 3 * GROUP), lambda b, k, i: (b, k, i, 0)),
                  pl.BlockSpec((128, 128), lambda b, k, i: (0, 0))],
        out_specs=pl.BlockSpec((1, tq, GROUP * HEAD_DIM), lambda b, k, i: (b, i, k)),
        compiler_params=pltpu.CompilerParams(
            dimension_semantics=("parallel", "parallel", "arbitrary"), vmem_limit_bytes=VMEM_LIMIT_BYTES),
        name="nsa_prompt",
    )(q, _heads_major(kcmp), _heads_major(vcmp), bf(ks), bf(vs), bf(kw), bf(vw), _heads_major(gl),
      _overlap_matrix(n_cmp, n_sb))


_NT = (((1,), (1,)), ((), ()))


def _decode_attend(s, mask, s_new, mask_new, pv_fn, v_new):
    sm = jnp.where(mask, s, NEG_INF)
    sn = jnp.where(mask_new, s_new, NEG_INF)
    m = jnp.maximum(jnp.max(sm, axis=1, keepdims=True), sn)
    e = jnp.where(mask, jnp.exp(sm - m), 0.0)
    en = jnp.where(mask_new, jnp.exp(sn - m), 0.0)
    l = jnp.sum(e, axis=1, keepdims=True) + en
    acc = pv_fn(e.astype(jnp.bfloat16)) + (en.astype(jnp.bfloat16).astype(jnp.float32)
                                            * v_new.astype(jnp.bfloat16).astype(jnp.float32))
    return acc / jnp.maximum(l, 1e-30)


def _nsa_sample_kernel(pt_ref, *refs, n_pages, n_sb):
    del pt_ref
    cmp_pages = refs[:n_pages]
    sel_pages = refs[n_pages:2 * n_pages]
    (win_ref, q_ref, gl_ref, ksn_ref, vsn_ref, kwn_ref, vwn_ref, kwc_ref, vwc_ref,
     wk_ref, wv_ref, pek_ref, pev_ref, w1k_ref, w1v_ref, w2k_ref, w2v_ref, cos_ref, sin_ref, ov_ref, ex_ref,
     o_ref, wout_ref, xk_s, xv_s) = refs[2 * n_pages:]
    past = n_pages * PAGE_SIZE
    qpos = past
    n_half = past // CMP_STRIDE
    n_cmp = (past + 1 - CMP_BLOCK) // CMP_STRIDE + 1
    wb = win_ref.shape[-1]
    bf = jnp.bfloat16

    for p in range(n_pages):
        xk_s[p * PAGE_SIZE:(p + 1) * PAGE_SIZE, :] = cmp_pages[p][0, 0, 0].T
        xv_s[p * PAGE_SIZE:(p + 1) * PAGE_SIZE, :] = cmp_pages[p][0, 0, 1].T
    row_c = lax.broadcasted_iota(jnp.int32, (n_half, KV_W), 0)
    kcmp = _compress_rows(lambda j: xk_s[pl.ds(j, n_half, stride=CMP_STRIDE), :],
                          wk_ref, pek_ref, w1k_ref, w2k_ref, n_half)
    kcmp = jnp.where(row_c < n_cmp, _rope_lanes(kcmp, cos_ref[...], sin_ref[...]), 0.0).astype(bf)
    vcmp = _compress_rows(lambda j: xv_s[pl.ds(j, n_half, stride=CMP_STRIDE), :],
                          wv_ref, pev_ref, w1v_ref, w2v_ref, n_half)
    vcmp = jnp.where(row_c < n_cmp, vcmp, 0.0).astype(bf)

    row8 = lax.broadcasted_iota(jnp.int32, (8, KV_W), 0)
    lane8 = lax.broadcasted_iota(jnp.int32, (8, KV_W), 1)
    top1 = lax.broadcasted_iota(jnp.int32, (8, 1), 0) < GROUP
    q8 = q_ref[0] * SCALE
    q2 = jnp.where((row8 < GROUP) == (lane8 < HEAD_DIM), jnp.concatenate([q8, q8], axis=1), 0.0).astype(bf)
    q2f = q2.astype(jnp.float32)

    def halves(x):
        return jnp.where(top1, x[:, :HEAD_DIM], x[:, HEAD_DIM:])

    def new_score(k_new):
        return jnp.sum(q2f * k_new.astype(bf).astype(jnp.float32), axis=1, keepdims=True)

    sc = lax.dot_general(q2, kcmp, _NT, preferred_element_type=jnp.float32)
    nl = lax.broadcasted_iota(jnp.int32, (8, n_half), 1)
    pc = _softmax_rows(sc, ((nl * CMP_STRIDE + (CMP_BLOCK - 1)) <= qpos) & (nl < n_cmp))
    o_cmp = halves(jnp.dot(pc.astype(bf), vcmp, preferred_element_type=jnp.float32))
    pk0 = jnp.sum(jnp.where(top1, pc, 0.0), axis=0, keepdims=True)
    pk1 = jnp.sum(jnp.where(top1, 0.0, pc), axis=0, keepdims=True)
    rown = lax.broadcasted_iota(jnp.int32, (8, n_half), 0)
    p2 = jnp.where(rown == 0, pk0, jnp.where(rown == 1, pk1, 0.0))
    p_hi = p2.astype(bf)
    p_lo = (p2 - p_hi.astype(jnp.float32)).astype(bf)
    imp = (jnp.dot(p_hi, ov_ref[...], preferred_element_type=jnp.float32)
           + jnp.dot(p_lo, ov_ref[...], preferred_element_type=jnp.float32))
    sel2 = _select_blocks(imp, jnp.full((8, 1), qpos, jnp.int32), n_sb)

    picked2 = jnp.dot(sel2.astype(bf), ex_ref[...], preferred_element_type=jnp.float32)
    mask_s = jnp.where(top1, picked2[0:1], picked2[1:2]) > 0.5
    seln = jnp.sum(jnp.where(lane8 == qpos // SEL_BLOCK, sel2, 0.0), axis=1, keepdims=True)
    mask_new = jnp.where(top1, seln[0:1], seln[1:2]) > 0.5
    s_s = jnp.concatenate([jnp.dot(q2, sel_pages[p][0, 0, 0].astype(bf), preferred_element_type=jnp.float32)
                           for p in range(n_pages)], axis=1)

    def pv_sel(e):
        acc = jnp.zeros((8, KV_W), jnp.float32)
        for p in range(n_pages):
            acc = acc + lax.dot_general(e[:, p * PAGE_SIZE:(p + 1) * PAGE_SIZE], sel_pages[p][0, 0, 1].astype(bf),
                                        _NT, preferred_element_type=jnp.float32)
        return acc

    o_sel = halves(_decode_attend(s_s, mask_s, new_score(ksn_ref[0]), mask_new, pv_sel, vsn_ref[0]))

    s_w = jnp.dot(q2, win_ref[0, 0, 0].astype(bf), preferred_element_type=jnp.float32)
    kpos = past - wb + lax.broadcasted_iota(jnp.int32, (8, wb), 1)
    mask_w = (qpos - kpos >= 0) & (qpos - kpos < WINDOW) & (kpos >= 0)
    pv_win = lambda e: lax.dot_general(e, win_ref[0, 0, 1].astype(bf), _NT, preferred_element_type=jnp.float32)
    o_win = halves(_decode_attend(s_w, mask_w, new_score(kwn_ref[0]), jnp.full((8, 1), True), pv_win, vwn_ref[0]))

    gate = jax.nn.sigmoid(gl_ref[0])
    o_ref[0] = gate[:, 0:1] * o_cmp + gate[:, 1:2] * o_sel + gate[:, 2:3] * o_win

    lane_w = lax.broadcasted_iota(jnp.int32, (KV_W, wb), 1)
    wout_ref[0, 0] = jnp.where(lane_w == wb - 1, kwc_ref[0], pltpu.roll(win_ref[0, 0, 0], wb - 1, 1))
    wout_ref[0, 1] = jnp.where(lane_w == wb - 1, vwc_ref[0], pltpu.roll(win_ref[0, 0, 1], wb - 1, 1))


def _cache_rows_on_lanes(c):
    nd = c.ndim
    c = jnp.moveaxis(c, nd - 4, nd - 1)
    return c.reshape(c.shape[:-3] + (c.shape[-3] * c.shape[-2], c.shape[-1]))


def _nsa_sample_pallas(layer, q, gl, ks, vs, kw, vw, cmpw, cmp_t, sel_t, win_t, page_table):
    B = q.shape[0]
    n_pages = page_table.shape[1]
    past = n_pages * PAGE_SIZE
    wb = win_t.shape[-1]
    n_half = past // CMP_STRIDE
    n_cmp = (past + 1 - CMP_BLOCK) // CMP_STRIDE + 1
    n_sb = -(-(past + 1) // SEL_BLOCK)
    wk1, pek, wk2, wv1, pev, wv2 = cmpw
    wk, pekf, w1k, w2k = _compress_weights(wk1, pek, wk2)
    wv, pevf, w1v, w2v = _compress_weights(wv1, pev, wv2)
    cos, sin = _rope_tables(jnp.arange(n_half) * CMP_STRIDE + CMP_BLOCK - 1, N_KV_HEADS)
    ov = _overlap_matrix(n_cmp, n_sb)[:n_half]
    ex = jnp.asarray((np.arange(past)[None, :] // SEL_BLOCK) == np.arange(128)[:, None], jnp.bfloat16)
    consts = (wk, wv, pekf, pevf, w1k, w1v, w2k, w2v, cos, sin, ov, ex)
    row3 = lambda x: x.reshape(B, 1, KV_W)
    col3 = lambda x: x.reshape(B, KV_W, 1)
    per_b = (q.reshape(B, N_HEADS, HEAD_DIM), gl.reshape(B, N_HEADS, 3), row3(ks), row3(vs), row3(kw), row3(vw),
             col3(kw), col3(vw))
    page_spec = lambda p: pl.BlockSpec((1, 1, 2, KV_W, PAGE_SIZE), lambda b, pt: (layer, pt[b, p], 0, 0, 0))
    b_spec = lambda a: pl.BlockSpec((1,) + a.shape[1:], lambda b, pt: (b,) + (0,) * (a.ndim - 1))
    full = lambda a: pl.BlockSpec(a.shape, lambda b, pt: (0,) * a.ndim)
    in_specs = ([page_spec(p) for p in range(n_pages)] * 2
                + [pl.BlockSpec((1, 1, 2, KV_W, wb), lambda b, pt: (layer, b, 0, 0, 0))]
                + [b_spec(a) for a in per_b] + [full(a) for a in consts])
    return pl.pallas_call(
        functools.partial(_nsa_sample_kernel, n_pages=n_pages, n_sb=n_sb),
        out_shape=(jax.ShapeDtypeStruct((B, N_HEADS, HEAD_DIM), jnp.float32),
                   jax.ShapeDtypeStruct((B, 2, KV_W, wb), jnp.float32)),
        grid_spec=pltpu.PrefetchScalarGridSpec(
            num_scalar_prefetch=1, grid=(B,), in_specs=in_specs,
            out_specs=(pl.BlockSpec((1, N_HEADS, HEAD_DIM), lambda b, pt: (b, 0, 0)),
                       pl.BlockSpec((1, 2, KV_W, wb), lambda b, pt: (b, 0, 0, 0))),
            scratch_shapes=[pltpu.VMEM((past, KV_W), jnp.float32), pltpu.VMEM((past, KV_W), jnp.float32)]),
        compiler_params=pltpu.CompilerParams(dimension_semantics=("arbitrary",), vmem_limit_bytes=VMEM_LIMIT_BYTES),
        name="nsa_sample",
    )(page_table, *([cmp_t] * n_pages), *([sel_t] * n_pages), win_t, *per_b, *consts)


SSM_N = SSM_GROUPS * SSM_STATE
SSM_LANE_BLK = 512
SSM_TL = 64


def _ssm_kernel(u_ref, h0r_ref, h0i_ref, ar_ref, ai_ref, bm_ref, cr_ref, ci_ref, d_ref,
                y_ref, hr_ref, hi_ref, xr_s, xi_s, *, tl, nb):
    c = pl.program_id(0)

    @pl.when(c == 0)
    def _():
        hr_ref[...] = h0r_ref[...]
        hi_ref[...] = h0i_ref[...]

    u = u_ref[...]
    ub = u.astype(jnp.bfloat16)
    n_grp = SSM_WIDTH // 128
    for j in range(n_grp):
        bu = jnp.dot(ub[:, 128 * j:128 * (j + 1)], bm_ref[j], preferred_element_type=jnp.float32)
        xr_s[:, 512 * j:512 * (j + 1)] = bu[:, :512]
        xi_s[:, 512 * j:512 * (j + 1)] = bu[:, 512:]

    for lb in range(SSM_N // SSM_LANE_BLK):
        sl = slice(lb * SSM_LANE_BLK, (lb + 1) * SSM_LANE_BLK)
        ar = jnp.broadcast_to(ar_ref[:, sl], (8, SSM_LANE_BLK))
        ai = jnp.broadcast_to(ai_ref[:, sl], (8, SSM_LANE_BLK))
        for r in range(nb // 8):
            def step(t, carry):
                hr, hi = carry
                row = pl.multiple_of(t * nb + r * 8, 8)
                xr = xr_s[pl.ds(row, 8), sl]
                xi = xi_s[pl.ds(row, 8), sl]
                nr = ar * hr - ai * hi + xr
                ni = ar * hi + ai * hr + xi
                xr_s[pl.ds(row, 8), sl] = nr
                xi_s[pl.ds(row, 8), sl] = ni
                return nr, ni

            hr, hi = lax.fori_loop(0, tl, step, (hr_ref[r * 8:(r + 1) * 8, sl], hi_ref[r * 8:(r + 1) * 8, sl]))
            hr_ref[r * 8:(r + 1) * 8, sl] = hr
            hi_ref[r * 8:(r + 1) * 8, sl] = hi

    for j in range(n_grp):
        yr = jnp.dot(xr_s[:, 512 * j:512 * (j + 1)].astype(jnp.bfloat16), cr_ref[j], preferred_element_type=jnp.float32)
        yi = jnp.dot(xi_s[:, 512 * j:512 * (j + 1)].astype(jnp.bfloat16), ci_ref[j], preferred_element_type=jnp.float32)
        y_ref[:, 128 * j:128 * (j + 1)] = yr - yi + d_ref[:, 128 * j:128 * (j + 1)] * u[:, 128 * j:128 * (j + 1)]


def _ssm_params(a_re, a_im, log_dt, b_re, b_im, c_re, c_im, d_skip):
    dt = jnp.exp(log_dt)[:, None]
    mag = jnp.exp(dt * a_re)
    ab_re = mag * jnp.cos(dt * a_im)
    ab_im = mag * jnp.sin(dt * a_im)
    den = a_re * a_re + a_im * a_im
    zr = ((ab_re - 1.0) * a_re + ab_im * a_im) / den
    zi = (ab_im * a_re - (ab_re - 1.0) * a_im) / den
    bb_re = zr[..., None] * b_re - zi[..., None] * b_im
    bb_im = zr[..., None] * b_im + zi[..., None] * b_re
    n_grp = SSM_WIDTH // 128
    gpl = 128 // SSM_GROUP
    eye = jnp.eye(gpl, dtype=jnp.float32)

    def b_blocks(bb):
        x = bb.reshape(n_grp, gpl, SSM_STATE, SSM_GROUP)
        return jnp.einsum('jgpc,gh->jgchp', x, eye).reshape(n_grp, 128, gpl * SSM_STATE)

    def c_blocks(cc):
        x = cc.reshape(n_grp, gpl, SSM_GROUP, SSM_STATE)
        return jnp.einsum('jgcp,gh->jgphc', x, eye).reshape(n_grp, gpl * SSM_STATE, 128)

    bm = jnp.concatenate([b_blocks(bb_re), b_blocks(bb_im)], axis=-1).astype(jnp.bfloat16)
    return (ab_re.reshape(1, SSM_N), ab_im.reshape(1, SSM_N), bm,
            c_blocks(c_re).astype(jnp.bfloat16), c_blocks(c_im).astype(jnp.bfloat16), d_skip.reshape(1, SSM_WIDTH))


def _ssm_pallas(u, h0_re, h0_im, params):
    B, T, _ = u.shape
    ab_re, ab_im, bm, cr, ci, d = params
    tl = _pick_tile(T, (SSM_TL,))
    u_tb = u.transpose(1, 0, 2).reshape(T * B, SSM_WIDTH)
    full = lambda a: pl.BlockSpec(a.shape, lambda c: (0,) * a.ndim)
    h0r = h0_re.reshape(B, SSM_N)
    h0i = h0_im.reshape(B, SSM_N)
    consts = (h0r, h0i, ab_re, ab_im, bm, cr, ci, d)
    y, hr, hi = pl.pallas_call(
        functools.partial(_ssm_kernel, tl=tl, nb=B),
        out_shape=(jax.ShapeDtypeStruct((T * B, SSM_WIDTH), jnp.float32),
                   jax.ShapeDtypeStruct((B, SSM_N), jnp.float32), jax.ShapeDtypeStruct((B, SSM_N), jnp.float32)),
        grid=(T // tl,),
        in_specs=[pl.BlockSpec((tl * B, SSM_WIDTH), lambda c: (c, 0))] + [full(a) for a in consts],
        out_specs=(pl.BlockSpec((tl * B, SSM_WIDTH), lambda c: (c, 0)),
                   pl.BlockSpec((B, SSM_N), lambda c: (0, 0)), pl.BlockSpec((B, SSM_N), lambda c: (0, 0))),
        scratch_shapes=[pltpu.VMEM((tl * B, SSM_N), jnp.float32), pltpu.VMEM((tl * B, SSM_N), jnp.float32)],
        compiler_params=pltpu.CompilerParams(dimension_semantics=("arbitrary",), vmem_limit_bytes=VMEM_LIMIT_BYTES),
        name="ssm",
    )(u_tb, *consts)
    y = y.reshape(T, B, SSM_WIDTH).transpose(1, 0, 2)
    return y, hr.reshape(B, SSM_GROUPS, SSM_STATE), hi.reshape(B, SSM_GROUPS, SSM_STATE)


def _rmsnorm(x, g):
    xf = x.astype(jnp.float32)
    y = xf * lax.rsqrt(jnp.mean(xf * xf, axis=-1, keepdims=True) + EPS)
    return (y * g.astype(jnp.float32)).astype(x.dtype)


def _rope(x, pos):
    half = HEAD_DIM // 2
    inv = jnp.float32(ROPE_THETA) ** (-jnp.arange(half, dtype=jnp.float32) / half)
    ang = pos.astype(jnp.float32)[:, None] * inv[None, :]
    cos = jnp.cos(ang)[:, None, :]
    sin = jnp.sin(ang)[:, None, :]
    x1, x2 = x[..., :half], x[..., half:]
    return jnp.concatenate([x1 * cos - x2 * sin, x2 * cos + x1 * sin], axis=-1)


def _masked_softmax(s, mask):
    s = jnp.where(mask, s.astype(jnp.float32), NEG_INF)
    m = jnp.max(s, axis=-1, keepdims=True)
    e = jnp.where(mask, jnp.exp(s - m), 0.0)
    return e / jnp.maximum(jnp.sum(e, axis=-1, keepdims=True), 1e-30)


def _attend(q, k, v, mask):
    s = jnp.einsum('btkgd,bskd->bkgts', q, k) * SCALE
    p = _masked_softmax(s, mask)
    o = jnp.einsum('bkgts,bskd->btkgd', p.astype(v.dtype), v)
    return o, p


def _window_mask(q_pos, k_pos):
    d = q_pos[:, None] - k_pos[None, :]
    return (d >= 0) & (d < WINDOW) & (k_pos[None, :] >= 0)


def _compress(x, w1, pe, w2):
    B, L, K, D = x.shape
    xt = x.transpose(0, 2, 1, 3).reshape(B * K, L, D)
    hdn = lax.conv_general_dilated(xt, w1.astype(xt.dtype), (CMP_STRIDE,), 'VALID',
                                   dimension_numbers=('NWC', 'WIO', 'NWC'))
    hdn = hdn + jnp.einsum('ld,ldf->f', pe, w1)
    out = jax.nn.gelu(hdn) @ w2
    n = out.shape[1]
    return out.reshape(B, K, n, D).transpose(0, 2, 1, 3)


def _cmp_branch(q, kc_full, vc_full, q_pos, wk1, pek, wk2, wv1, pev, wv2):
    kcmp = _compress(kc_full, wk1, pek, wk2)
    vcmp = _compress(vc_full, wv1, pev, wv2)
    n_cmp = kcmp.shape[1]
    start = jnp.arange(n_cmp) * CMP_STRIDE
    end = start + CMP_BLOCK - 1
    kcmp = _rope(kcmp, end)
    o, p = _attend(q, kcmp, vcmp, end[None, :] <= q_pos[:, None])
    L = kc_full.shape[1]
    n_sb = -(-L // SEL_BLOCK)
    sb_start = jnp.arange(n_sb) * SEL_BLOCK
    overlap = ((start[:, None] < sb_start[None, :] + SEL_BLOCK) & (end[:, None] >= sb_start[None, :])).astype(jnp.float32)
    imp = jnp.einsum('bkgtn,nj->btkj', p, overlap)
    blk = jnp.arange(n_sb)
    forced = (blk[None, :] == 0) | (blk[None, :] == (q_pos // SEL_BLOCK)[:, None])
    causal = sb_start[None, :] <= q_pos[:, None]
    imp = jnp.where(forced[None, :, None, :], FORCE_SCORE, imp)
    imp = jnp.where(causal[None, :, None, :], imp, NEG_INF)
    top_v, top_i = lax.top_k(imp, min(N_SEL, n_sb))
    return o, top_i, top_v > 0.5 * NEG_INF


def _sel_attend(q, idx, valid, q_pos, fetch):
    pos = idx[..., None] * SEL_BLOCK + jnp.arange(SEL_BLOCK)
    k, v = fetch(pos)
    B, Tc, K, N, S = pos.shape
    mask = (valid[..., None] & (pos <= q_pos[None, :, None, None, None])).reshape(B, Tc, K, N * S)
    k = k.reshape(B, Tc, K, N * S, HEAD_DIM)
    v = v.reshape(B, Tc, K, N * S, HEAD_DIM)
    s = jnp.einsum('btkgd,btksd->btkgs', q, k) * SCALE
    p = _masked_softmax(s, mask[:, :, :, None, :])
    return jnp.einsum('btkgs,btksd->btkgd', p.astype(v.dtype), v)


def _to_chunks(x, c):
    B, T = x.shape[:2]
    return x.reshape((B, T // c, c) + x.shape[2:]).swapaxes(0, 1)


def _from_chunks(x):
    nc, B, c = x.shape[:3]
    return x.swapaxes(0, 1).reshape((B, nc * c) + x.shape[3:])


def _batch_head_index(B):
    bi = jnp.arange(B)[:, None, None, None, None]
    hd = jnp.arange(N_KV_HEADS)[None, None, :, None, None]
    return bi, hd


def _nsa_prompt(q, kc, vc, ks, vs, kw, vw, cmpw):
    B, T = q.shape[:2]
    pos = jnp.arange(T)
    o_cmp, idx, valid = _cmp_branch(q, kc, vc, pos, *cmpw)
    bi, hd = _batch_head_index(B)

    def fetch(p):
        pc = jnp.clip(p, 0, T - 1)
        return ks[bi, pc, hd], vs[bi, pc, hd]

    qb = Q_BLOCK if T % Q_BLOCK == 0 else T
    o_sel = _from_chunks(lax.map(lambda a: _sel_attend(a[0], a[1], a[2], a[3], fetch),
                                 (_to_chunks(q, qb), _to_chunks(idx, qb), _to_chunks(valid, qb), pos.reshape(-1, qb))))
    kp = jnp.pad(kw, ((0, 0), (WINDOW, 0), (0, 0), (0, 0)))
    vp = jnp.pad(vw, ((0, 0), (WINDOW, 0), (0, 0), (0, 0)))

    def win_block(a):
        c, qc = a
        start = c * qb
        kb = lax.dynamic_slice_in_dim(kp, start, WINDOW + qb, axis=1)
        vb = lax.dynamic_slice_in_dim(vp, start, WINDOW + qb, axis=1)
        qpos = start + jnp.arange(qb)
        kpos = start - WINDOW + jnp.arange(WINDOW + qb)
        return _attend(qc, kb, vb, _window_mask(qpos, kpos))[0]

    o_win = _from_chunks(lax.map(win_block, (jnp.arange(T // qb), _to_chunks(q, qb))))
    n_keep = min(WINDOW, T)
    rows = (jnp.stack([kc, vc], axis=2), jnp.stack([ks, vs], axis=2), jnp.stack([kw, vw], axis=2)[:, T - n_keep:])
    return o_cmp, o_sel, o_win, rows


def _nsa_sample(q, kc, vc, ks, vs, kw, vw, cmpw, cmp_pool, sel_pool, win_buf, page_table):
    B, T = q.shape[:2]
    past = page_table.shape[1] * PAGE_SIZE
    pos = past + jnp.arange(T)
    past_cmp = cmp_pool[page_table].reshape((B, past) + cmp_pool.shape[2:])
    kc_full = jnp.concatenate([past_cmp[:, :, 0], kc], axis=1)
    vc_full = jnp.concatenate([past_cmp[:, :, 1], vc], axis=1)
    o_cmp, idx, valid = _cmp_branch(q, kc_full, vc_full, pos, *cmpw)
    sel_rows = sel_pool.reshape((-1,) + sel_pool.shape[2:])
    bi, hd = _batch_head_index(B)

    def fetch(p):
        pc = jnp.clip(p, 0, past - 1)
        phys = page_table[bi, pc // PAGE_SIZE] * PAGE_SIZE + pc % PAGE_SIZE
        pn = jnp.clip(p - past, 0, T - 1)
        in_past = (p < past)[..., None]
        k = jnp.where(in_past, sel_rows[phys, 0, hd], ks[bi, pn, hd])
        v = jnp.where(in_past, sel_rows[phys, 1, hd], vs[bi, pn, hd])
        return k, v

    o_sel = _sel_attend(q, idx, valid, pos, fetch)
    wb = win_buf.shape[1]
    win_all = jnp.concatenate([win_buf, jnp.stack([kw, vw], axis=2)], axis=1)
    kpos = past - wb + jnp.arange(wb + T)
    o_win, _ = _attend(q, win_all[:, :, 0], win_all[:, :, 1], _window_mask(pos, kpos))
    rows = (jnp.stack([kc, vc], axis=2), jnp.stack([ks, vs], axis=2), win_all[:, T:])
    return o_cmp, o_sel, o_win, rows


def _cmul_combine(e1, e2):
    ar1, ai1, br1, bi1 = e1
    ar2, ai2, br2, bi2 = e2
    return (ar2 * ar1 - ai2 * ai1, ar2 * ai1 + ai2 * ar1,
            ar2 * br1 - ai2 * bi1 + br2, ar2 * bi1 + ai2 * br1 + bi2)


def _ssm_scan(u, h0_re, h0_im, a_re, a_im, log_dt, b_re, b_im, c_re, c_im, d_skip):
    B, T, _ = u.shape
    uf = u.reshape(B, T, SSM_GROUPS, SSM_GROUP)
    ar = a_re
    ai = a_im
    dt = jnp.exp(log_dt)[:, None]
    mag = jnp.exp(dt * ar)
    ab_re = mag * jnp.cos(dt * ai)
    ab_im = mag * jnp.sin(dt * ai)
    den = ar * ar + ai * ai
    zr = ((ab_re - 1.0) * ar + ab_im * ai) / den
    zi = (ab_im * ar - (ab_re - 1.0) * ai) / den
    bb_re = zr[..., None] * b_re - zi[..., None] * b_im
    bb_im = zr[..., None] * b_im + zi[..., None] * b_re
    bu_re = jnp.einsum('gpc,btgc->tbgp', bb_re, uf)
    bu_im = jnp.einsum('gpc,btgc->tbgp', bb_im, uf)
    chunk = SSM_CHUNK if T % SSM_CHUNK == 0 else T
    nc = T // chunk
    shp = (chunk, B, SSM_GROUPS, SSM_STATE)
    bu_re = bu_re.reshape((nc,) + shp)
    bu_im = bu_im.reshape((nc,) + shp)
    a_re_c = jnp.broadcast_to(ab_re, shp)
    a_im_c = jnp.broadcast_to(ab_im, shp)

    def step(carry, xs):
        hr0, hi0 = carry
        xr, xi = xs
        pr, pim, sr, si = lax.associative_scan(_cmul_combine, (a_re_c, a_im_c, xr, xi), axis=0)
        hr = sr + pr * hr0 - pim * hi0
        hi = si + pr * hi0 + pim * hr0
        y = jnp.einsum('gcp,tbgp->tbgc', c_re, hr) - jnp.einsum('gcp,tbgp->tbgc', c_im, hi)
        return (hr[-1], hi[-1]), y

    (hr, hi), y = lax.scan(step, (h0_re, h0_im), (bu_re, bu_im))
    y = y.reshape(T, B, SSM_GROUPS, SSM_GROUP).transpose(1, 0, 2, 3) + d_skip * uf
    return y.reshape(B, T, SSM_WIDTH), hr, hi


def _block(h, p_l, pos, lw, sample, h0_re, h0_im, conv_prefix):
    (g_attn, w_in, g_q, g_kc, g_ks, g_kw, wk1, pek, wk2, wv1, pev, wv2,
     a_re, a_im, log_dt, b_re, b_im, c_re, c_im, d_skip,
     w_a, w_glu1, w_glu2, w_o, g_ffn, w_up, conv_w, conv_b, w_down,
     g_ple, w_ple_gate, w_ple) = lw
    B, T, _ = h.shape
    N = B * T

    def mm(x, w, g=None):
        return _matmul(x.reshape(N, x.shape[-1]), w, g).reshape(B, T, w.shape[1])

    z = mm(h, w_in, g_attn)
    pts = [int(s) for s in np.cumsum(SPLIT_SIZES)[:-1]]
    q, kc, vc, ks, vs, kw, vw, gl, u, ga, gb = jnp.split(z, pts, axis=-1)

    def heads(t):
        return t.reshape(B, T, -1, HEAD_DIM)

    q = _rope(_rmsnorm(heads(q), g_q), pos).reshape(B, T, N_KV_HEADS, GROUP, HEAD_DIM)
    kc = _rmsnorm(heads(kc), g_kc)
    ks = _rope(_rmsnorm(heads(ks), g_ks), pos)
    kw = _rope(_rmsnorm(heads(kw), g_kw), pos)
    vc, vs, vw = heads(vc), heads(vs), heads(vw)
    cmpw = (wk1, pek, wk2, wv1, pev, wv2)
    flat = lambda t: t.reshape(B, T, KV_W)
    if sample is None:
        kcmp, vcmp = _compress_pallas(flat(kc), flat(vc), cmpw)
        o = _nsa_prompt_pallas(q.reshape(B, T, Q_W), kcmp, vcmp, flat(ks), flat(vs), flat(kw), flat(vw), gl)
        n_keep = min(WINDOW, T)
        rows = (jnp.stack([kc, vc], axis=2), jnp.stack([ks, vs], axis=2),
                jnp.stack([kw, vw], axis=2)[:, T - n_keep:])
    else:
        assert T == 1
        layer, cmp_t, sel_t, win_t, page_table = sample
        f2 = lambda t: t.reshape(B, KV_W)
        o, wnew = _nsa_sample_pallas(layer, q.reshape(B, Q_W), gl.reshape(B, -1), f2(ks), f2(vs), f2(kw), f2(vw),
                                     cmpw, cmp_t, sel_t, win_t, page_table)
        wb = wnew.shape[-1]
        wnew = wnew.reshape(B, 2, N_KV_HEADS, HEAD_DIM, wb).transpose(0, 4, 1, 2, 3)
        rows = (jnp.stack([kc, vc], axis=2), jnp.stack([ks, vs], axis=2), wnew)
    a_out = mm(o.reshape(B, T, Q_W), w_a)
    y, hr, hi = _ssm_pallas(u, h0_re, h0_im, _ssm_params(a_re, a_im, log_dt, b_re, b_im, c_re, c_im, d_skip))
    yg = jax.nn.gelu(y)
    b_out = mm(yg, w_glu1) * jax.nn.sigmoid(mm(yg, w_glu2))
    h = h + mm(jax.nn.sigmoid(ga) * a_out + jax.nn.sigmoid(gb) * b_out, w_o)
    up = mm(h, w_up, g_ffn)
    gp, val = jnp.split(up, 2, axis=-1)
    ext = jnp.concatenate([conv_prefix, gp], axis=1)
    conv = conv_b
    for j in range(CONV_W):
        conv = conv + conv_w[j] * ext[:, j:j + T]
    h = h + mm(jax.nn.gelu(conv) * val, w_down)
    h = h + jax.nn.sigmoid(mm(h, w_ple_gate, g_ple)) * mm(p_l, w_ple)
    return h, rows, hr, hi, ext[:, T:]


def kernel(x_prompt, x_sample, cache_cmp, cache_sel, cache_win, state_ssm_re, state_ssm_im, state_conv, page_table, p_prompt, p_sample, g_attn, w_in, g_q, g_kc, g_ks, g_kw, cmp_wk1, cmp_pek, cmp_wk2, cmp_wv1, cmp_pev, cmp_wv2, ssm_a_re, ssm_a_im, ssm_log_dt, ssm_b_re, ssm_b_im, ssm_c_re, ssm_c_im, ssm_d, w_a, w_glu1, w_glu2, w_o, g_ffn, w_up, conv_w, conv_b, w_down, g_ple, w_ple_gate, w_ple):
    Bp, Tp = x_prompt.shape[:2]
    Ts = x_sample.shape[1]
    depth = w_in.shape[0]
    past = page_table.shape[1] * PAGE_SIZE
    pos_p = jnp.arange(Tp)
    pos_s = past + jnp.arange(Ts)
    zeros_h = jnp.zeros((Bp, SSM_GROUPS, SSM_STATE), x_prompt.dtype)
    zeros_c = jnp.zeros((Bp, CONV_W - 1, D_FF), x_prompt.dtype)
    layer_w = (g_attn, w_in, g_q, g_kc, g_ks, g_kw, cmp_wk1, cmp_pek, cmp_wk2, cmp_wv1, cmp_pev, cmp_wv2,
               ssm_a_re, ssm_a_im, ssm_log_dt, ssm_b_re, ssm_b_im, ssm_c_re, ssm_c_im, ssm_d,
               w_a, w_glu1, w_glu2, w_o, g_ffn, w_up, conv_w, conv_b, w_down, g_ple, w_ple_gate, w_ple)
    cmp_t, sel_t, win_t = (_cache_rows_on_lanes(c) for c in (cache_cmp, cache_sel, cache_win))
    st = [[] for _ in range(12)]
    hp, hs = x_prompt, x_sample
    for i in range(depth):
        lw = [w[i] for w in layer_w]
        hp, rows, hr, hi, cv = _block(hp, p_prompt[i], pos_p, lw, None, zeros_h, zeros_h, zeros_c)
        for j, a in enumerate(list(rows) + [hr, hi, cv]):
            st[j].append(a)
        hs, rows, hr, hi, cv = _block(hs, p_sample[i], pos_s, lw, (i, cmp_t, sel_t, win_t, page_table),
                                      state_ssm_re[i], state_ssm_im[i], state_conv[i])
        for j, a in enumerate(list(rows) + [hr, hi, cv]):
            st[6 + j].append(a)
    return (hp, hs) + tuple(jnp.stack(s) for s in st)
```

```python
import functools
import math

import numpy as np
import jax
import jax.numpy as jnp
from jax import lax
from jax.experimental import pallas as pl
from jax.experimental.pallas import tpu as pltpu

D_MODEL = 1024
N_HEADS = 8
N_KV_HEADS = 2
HEAD_DIM = 64
GROUP = N_HEADS // N_KV_HEADS
Q_W = N_HEADS * HEAD_DIM
KV_W = N_KV_HEADS * HEAD_DIM
CMP_BLOCK = 32
CMP_STRIDE = 16
SEL_BLOCK = 64
N_SEL = 8
WINDOW = 512
Q_BLOCK = 128
PAGE_SIZE = 128
ROPE_THETA = 10000.0
SSM_WIDTH = D_MODEL // 2
SSM_GROUP = 16
SSM_GROUPS = SSM_WIDTH // SSM_GROUP
SSM_STATE = 64
SSM_CHUNK = 128
D_FF = 11 * D_MODEL // 4
CONV_W = 3
EPS = 1e-6
NEG_INF = -1e30
FORCE_SCORE = 1e9
SCALE = HEAD_DIM ** -0.5
SPLIT_SIZES = (Q_W, KV_W, KV_W, KV_W, KV_W, KV_W, KV_W, 3 * N_HEADS, SSM_WIDTH, D_MODEL, D_MODEL)

VMEM_LIMIT_BYTES = 56 * 1024 * 1024


def _mm_kernel(x_ref, g_ref, w_ref, o_ref, *, norm):
    x = x_ref[...]
    if norm:
        x = x * lax.rsqrt(jnp.mean(x * x, axis=-1, keepdims=True) + EPS) * g_ref[...]
    o_ref[...] = jnp.dot(x.astype(jnp.bfloat16), w_ref[...], preferred_element_type=jnp.float32)


def _pick_tile(n, cands):
    for c in cands:
        if n % c == 0:
            return c
    return n


def _matmul(x, w, g=None):
    n, k = x.shape
    m = w.shape[1]
    tm = _pick_tile(n, (512, 256, 128))
    tn = _pick_tile(m, (1024, 512, 256, 128))
    norm = g is not None
    if g is None:
        g = jnp.ones((k,), jnp.float32)
    return pl.pallas_call(
        functools.partial(_mm_kernel, norm=norm),
        out_shape=jax.ShapeDtypeStruct((n, m), jnp.float32),
        grid=(n // tm, m // tn),
        in_specs=[pl.BlockSpec((tm, k), lambda i, j: (i, 0)),
                  pl.BlockSpec((1, k), lambda i, j: (0, 0)),
                  pl.BlockSpec((k, tn), lambda i, j: (0, j))],
        out_specs=pl.BlockSpec((tm, tn), lambda i, j: (i, j)),
        compiler_params=pltpu.CompilerParams(
            dimension_semantics=("parallel", "parallel"), vmem_limit_bytes=VMEM_LIMIT_BYTES),
        name="norm_matmul" if norm else "matmul",
    )(x, g.reshape(1, k).astype(jnp.float32), w.astype(jnp.bfloat16))


HALF_ROWS = CMP_BLOCK // CMP_STRIDE
assert HALF_ROWS == 2


def _rope_lanes(x, cos, sin_signed):
    w = x.shape[-1]
    half = HEAD_DIM // 2
    lane = lax.broadcasted_iota(jnp.int32, x.shape, x.ndim - 1)
    first = (lane % HEAD_DIM) < half
    partner = jnp.where(first, pltpu.roll(x, w - half, x.ndim - 1), pltpu.roll(x, half, x.ndim - 1))
    return x * cos + partner * sin_signed


def _compress_rows(load, w_ref, pe_ref, w1_ref, w2_ref, n_half):
    acc = jnp.zeros((n_half, 4 * 128), jnp.float32)
    for j in range(CMP_STRIDE):
        acc = acc + jnp.dot(load(j).astype(jnp.bfloat16), w_ref[j], preferred_element_type=jnp.float32)
    pa = acc[:, :256]
    pb = pltpu.roll(acc[:, 256:], n_half - 1, 0)
    bias = jnp.dot(pe_ref[...].astype(jnp.bfloat16), w1_ref[...], preferred_element_type=jnp.float32)[0:1]
    bias2 = jnp.concatenate([bias, bias], axis=1)
    hdn = jax.nn.gelu(pa + pb + bias2)
    return jnp.dot(hdn.astype(jnp.bfloat16), w2_ref[...], preferred_element_type=jnp.float32)


def _compress_kernel(xk_ref, xv_ref, wk_ref, wv_ref, pek_ref, pev_ref, w1k_ref, w1v_ref, w2k_ref, w2v_ref,
                     cos_ref, sin_ref, ko_ref, vo_ref, *, n_half, n_cmp):
    row = lax.broadcasted_iota(jnp.int32, (n_half, KV_W), 0)
    k = _compress_rows(lambda j: xk_ref[0, pl.ds(j, n_half, stride=CMP_STRIDE), :],
                       wk_ref, pek_ref, w1k_ref, w2k_ref, n_half)
    k = _rope_lanes(k, cos_ref[...], sin_ref[...])
    v = _compress_rows(lambda j: xv_ref[0, pl.ds(j, n_half, stride=CMP_STRIDE), :],
                       wv_ref, pev_ref, w1v_ref, w2v_ref, n_half)
    ko_ref[0] = jnp.where(row < n_cmp, k, 0.0).astype(ko_ref.dtype)
    vo_ref[0] = jnp.where(row < n_cmp, v, 0.0).astype(vo_ref.dtype)


def _blockdiag2(w):
    z = jnp.zeros_like(w)
    return jnp.concatenate([jnp.concatenate([w, z], axis=-1), jnp.concatenate([z, w], axis=-1)], axis=-2)


def _compress_weights(w1, pe, w2):
    bd = _blockdiag2(w1)
    wcat = jnp.concatenate([bd[:CMP_STRIDE], bd[CMP_STRIDE:]], axis=-1).astype(jnp.bfloat16)
    pe_flat = jnp.broadcast_to(pe.reshape(1, -1), (8, pe.size))
    w1_flat = w1.reshape(-1, w1.shape[-1]).astype(jnp.bfloat16)
    w2bd = _blockdiag2(w2).astype(jnp.bfloat16)
    return wcat, pe_flat, w1_flat, w2bd


def _rope_tables(pos, reps):
    half = HEAD_DIM // 2
    inv = jnp.float32(ROPE_THETA) ** (-jnp.arange(half, dtype=jnp.float32) / half)
    ang = pos.astype(jnp.float32)[:, None] * inv[None, :]
    cos = jnp.cos(ang)
    sin = jnp.sin(ang)
    return (jnp.tile(jnp.concatenate([cos, cos], axis=-1), (1, reps)),
            jnp.tile(jnp.concatenate([-sin, sin], axis=-1), (1, reps)))


def _compress_pallas(krows, vrows, cmpw):
    wk1, pek, wk2, wv1, pev, wv2 = cmpw
    B, L, _ = krows.shape
    n_half = L // CMP_STRIDE
    n_cmp = n_half - 1
    wk, pekf, w1k, w2k = _compress_weights(wk1, pek, wk2)
    wv, pevf, w1v, w2v = _compress_weights(wv1, pev, wv2)
    end = jnp.arange(n_half) * CMP_STRIDE + CMP_BLOCK - 1
    cos, sin = _rope_tables(end, N_KV_HEADS)
    full = lambda a: pl.BlockSpec(a.shape, lambda b: (0,) * a.ndim)
    consts = (wk, wv, pekf, pevf, w1k, w1v, w2k, w2v, cos, sin)
    return pl.pallas_call(
        functools.partial(_compress_kernel, n_half=n_half, n_cmp=n_cmp),
        out_shape=(jax.ShapeDtypeStruct((B, n_half, KV_W), jnp.bfloat16),) * 2,
        grid=(B,),
        in_specs=[pl.BlockSpec((1, L, KV_W), lambda b: (b, 0, 0))] * 2 + [full(a) for a in consts],
        out_specs=(pl.BlockSpec((1, n_half, KV_W), lambda b: (b, 0, 0)),) * 2,
        compiler_params=pltpu.CompilerParams(dimension_semantics=("parallel",), vmem_limit_bytes=VMEM_LIMIT_BYTES),
        name="compress",
    )(krows, vrows, *consts)


ATT_TQ = 128
SEL_KC = 512
BIG_NEG = -3.0e38


def _softmax_rows(s, mask):
    s = jnp.where(mask, s, NEG_INF)
    m = jnp.max(s, axis=-1, keepdims=True)
    e = jnp.where(mask, jnp.exp(s - m), 0.0)
    return e / jnp.maximum(jnp.sum(e, axis=-1, keepdims=True), 1e-30)


def _select_blocks(imp, tpos, n_sb):
    tq = imp.shape[0]
    jl = lax.broadcasted_iota(jnp.int32, (tq, 128), 1)
    jf = jl.astype(jnp.float32)
    forced = (jl == 0) | (jl == (tpos >> 6))
    imp = jnp.where(forced, FORCE_SCORE, imp)
    imp = jnp.where(jl * SEL_BLOCK <= tpos, imp, NEG_INF)
    imp = jnp.where(jl < n_sb, imp, BIG_NEG)
    sel = jnp.zeros((tq, 128), jnp.float32)
    for _ in range(N_SEL):
        m = jnp.max(imp, axis=-1, keepdims=True)
        first = jnp.min(jnp.where(imp == m, jf, 1e9), axis=-1, keepdims=True)
        hit = jf == first
        sel = jnp.where(hit & (m > 0.5 * NEG_INF), 1.0, sel)
        imp = jnp.where(hit, BIG_NEG, imp)
    return sel


def _select_blocks_t(imp, tpos, n_sb):
    nj, tq = imp.shape
    jr = lax.broadcasted_iota(jnp.int32, (nj, tq), 0)
    jf = jr.astype(jnp.float32)
    forced = (jr == 0) | (jr == (tpos >> 6))
    imp = jnp.where(forced, FORCE_SCORE, imp)
    imp = jnp.where(jr * SEL_BLOCK <= tpos, imp, NEG_INF)
    imp = jnp.where(jr < n_sb, imp, BIG_NEG)
    sel = jnp.zeros((nj, tq), jnp.float32)
    for _ in range(N_SEL):
        m = jnp.max(imp, axis=0, keepdims=True)
        first = jnp.min(jnp.where(imp == m, jf, 1e9), axis=0, keepdims=True)
        hit = jf == first
        sel = jnp.where(hit & (m > 0.5 * NEG_INF), 1.0, sel)
        imp = jnp.where(hit, BIG_NEG, imp)
    return sel


def _nsa_prompt_kernel(q_ref, kcmp_ref, vcmpt_ref, ks_ref, vst_ref, kw_ref, vwt_ref, glt_ref, ovt_ref, et_ref, o_ref,
                       *, tq, n_cmp, n_sb):
    i = pl.program_id(2)
    t0 = i * tq
    bf = jnp.bfloat16
    qf = q_ref[0] * SCALE
    q4 = jnp.concatenate([qf[:, g * HEAD_DIM:(g + 1) * HEAD_DIM] for g in range(GROUP)], axis=0).astype(bf)
    lanes4 = lambda x: jnp.concatenate([x] * GROUP, axis=1)
    tq_pos = t0 + lax.broadcasted_iota(jnp.int32, (1, tq), 1)
    tpos = lanes4(tq_pos)

    nr = lax.broadcasted_iota(jnp.int32, (128, 1), 0)
    maskc = ((nr * CMP_STRIDE + (CMP_BLOCK - 1)) <= tpos) & (nr < n_cmp)
    sc = jnp.where(maskc, lax.dot_general(kcmp_ref[0, 0], q4, _NT, preferred_element_type=jnp.float32), NEG_INF)
    ec = jnp.where(maskc, jnp.exp(sc - jnp.max(sc, axis=0, keepdims=True)), 0.0)
    pc = ec / jnp.maximum(jnp.sum(ec, axis=0, keepdims=True), 1e-30)
    o_cmp = jnp.dot(vcmpt_ref[0, 0], pc.astype(bf), preferred_element_type=jnp.float32)
    psum = pc[:, :tq] + pc[:, tq:2 * tq] + pc[:, 2 * tq:3 * tq] + pc[:, 3 * tq:]
    p_hi = psum.astype(bf)
    p_lo = (psum - p_hi.astype(jnp.float32)).astype(bf)
    imp = (jnp.dot(ovt_ref[...], p_hi, preferred_element_type=jnp.float32)
           + jnp.dot(ovt_ref[...], p_lo, preferred_element_type=jnp.float32))
    nj = -(-n_sb // 8) * 8
    sel = _select_blocks_t(imp[:nj], tq_pos, n_sb)
    sel = jnp.concatenate([sel, jnp.zeros((128 - nj, tq), jnp.float32)], axis=0).astype(bf)

    per_kc = SEL_KC // ATT_TQ

    def sel_step(c, carry):
        m, l, acc = carry
        k0 = pl.multiple_of(c * SEL_KC, SEL_KC)
        kpos = k0 + lax.broadcasted_iota(jnp.int32, (SEL_KC, 1), 0)
        picked = jnp.dot(et_ref[pl.ds(k0, SEL_KC), :], sel, preferred_element_type=jnp.float32)
        mask = lanes4((picked > 0.5) & (kpos <= tq_pos))
        s = lax.dot_general(ks_ref[0, 0, pl.ds(k0, SEL_KC), :], q4, _NT, preferred_element_type=jnp.float32)
        s = jnp.where(mask, s, NEG_INF)
        m_new = jnp.maximum(m, jnp.max(s, axis=0, keepdims=True))
        alpha = jnp.exp(m - m_new)
        p = jnp.exp(s - m_new)
        l = alpha * l + jnp.sum(p, axis=0, keepdims=True)
        vt = jnp.concatenate([vst_ref[0, 0, c * per_kc + r] for r in range(per_kc)], axis=1)
        acc = alpha * acc + jnp.dot(vt, p.astype(bf), preferred_element_type=jnp.float32)
        return m_new, l, acc

    nq = GROUP * tq
    init = (jnp.full((1, nq), NEG_INF, jnp.float32), jnp.zeros((1, nq), jnp.float32),
            jnp.zeros((HEAD_DIM, nq), jnp.float32))
    n_kc = (t0 + tq + SEL_KC - 1) // SEL_KC
    _, l_s, acc_s = lax.fori_loop(0, n_kc, sel_step, init)
    o_sel = acc_s / jnp.maximum(l_s, 1e-30)

    n_wc = WINDOW // tq + 1
    c0 = jnp.maximum(i - WINDOW // tq, 0)
    w0 = pl.multiple_of(c0 * tq, tq)
    d = tpos - (w0 + lax.broadcasted_iota(jnp.int32, (n_wc * tq, 1), 0))
    sw = lax.dot_general(kw_ref[0, 0, pl.ds(w0, n_wc * tq), :], q4, _NT, preferred_element_type=jnp.float32)
    sw = jnp.where((d >= 0) & (d < WINDOW), sw, NEG_INF)
    ew = jnp.exp(sw - jnp.max(sw, axis=0, keepdims=True))
    vwt = jnp.concatenate([vwt_ref[0, 0, c0 + r] for r in range(n_wc)], axis=1)
    o_win = (jnp.dot(vwt, ew.astype(bf), preferred_element_type=jnp.float32)
             / jnp.maximum(jnp.sum(ew, axis=0, keepdims=True), 1e-30))

    gate = jax.nn.sigmoid(glt_ref[0, 0])
    for g in range(GROUP):
        sl = slice(g * tq, (g + 1) * tq)
        ot = (gate[3 * g:3 * g + 1] * o_cmp[:, sl] + gate[3 * g + 1:3 * g + 2] * o_sel[:, sl]
              + gate[3 * g + 2:3 * g + 3] * o_win[:, sl])
        o_ref[0, :, g * HEAD_DIM:(g + 1) * HEAD_DIM] = ot.T


def _overlap_matrix(n_cmp, n_sb):
    start = np.arange(128) * CMP_STRIDE
    end = start + CMP_BLOCK - 1
    sb = np.arange(128) * SEL_BLOCK
    ov = (start[:, None] < sb[None, :] + SEL_BLOCK) & (end[:, None] >= sb[None, :])
    ov &= (np.arange(128)[:, None] < n_cmp) & (np.arange(128)[None, :] < n_sb)
    return jnp.asarray(ov, jnp.bfloat16)


def _heads_major(x):
    B, T, W = x.shape
    return x.reshape(B, T, N_KV_HEADS, W // N_KV_HEADS).transpose(0, 2, 1, 3)


def _nsa_prompt_pallas(q, kcmp, vcmp, ks, vs, kw, vw, gl):
    B, T, _ = q.shape
    tq = ATT_TQ
    n_cmp = (T - CMP_BLOCK) // CMP_STRIDE + 1
    n_sb = -(-T // SEL_BLOCK)
    assert T % SEL_KC == 0 and T >= WINDOW + tq and kcmp.shape[1] == 128 and n_sb <= 128
    k_rows = lambda x: _heads_major(x.astype(jnp.bfloat16))
    v_cols = lambda x: x.astype(jnp.bfloat16).reshape(B, T // tq, tq, N_KV_HEADS, HEAD_DIM).transpose(0, 3, 1, 4, 2)
    vcmp_t = vcmp.reshape(B, 128, N_KV_HEADS, HEAD_DIM).transpose(0, 2, 3, 1)
    gl_t = gl.reshape(B, T, N_KV_HEADS, 3 * GROUP).transpose(0, 2, 3, 1)
    et = jnp.asarray((np.arange(T)[:, None] // SEL_BLOCK) == np.arange(128)[None, :], jnp.bfloat16)
    k_spec = pl.BlockSpec((1, 1, T, HEAD_DIM), lambda b, k, i: (b, k, 0, 0))
    v_spec = pl.BlockSpec((1, 1, T // tq, HEAD_DIM, tq), lambda b, k, i: (b, k, 0, 0, 0))
    return pl.pallas_call(
        functools.partial(_nsa_prompt_kernel, tq=tq, n_cmp=n_cmp, n_sb=n_sb),
        out_shape=jax.ShapeDtypeStruct((B, T, Q_W), jnp.float32),
        grid=(B, N_KV_HEADS, T // tq),
        in_specs=[pl.BlockSpec((1, tq, GROUP * HEAD_DIM), lambda b, k, i: (b, i, k)),
                  pl.BlockSpec((1, 1, 128, HEAD_DIM), lambda b, k, i: (b, k, 0, 0)),
                  pl.BlockSpec((1, 1, HEAD_DIM, 128), lambda b, k, i: (b, k, 0, 0)),
                  k_spec, v_spec, k_spec, v_spec,
                  pl.BlockSpec((1, 1, 3 * GROUP, tq), lambda b, k, i: (b, k, 0, i)),
                  pl.BlockSpec((128, 128), lambda b, k, i: (0, 0)),
                  pl.BlockSpec((T, 128), lambda b, k, i: (0, 0))],
        out_specs=pl.BlockSpec((1, tq, GROUP * HEAD_DIM), lambda b, k, i: (b, i, k)),
        compiler_params=pltpu.CompilerParams(
            dimension_semantics=("parallel", "parallel", "arbitrary"), vmem_limit_bytes=VMEM_LIMIT_BYTES),
        name="nsa_prompt",
    )(q, _heads_major(kcmp), vcmp_t, k_rows(ks), v_cols(vs), k_rows(kw), v_cols(vw), gl_t,
      _overlap_matrix(n_cmp, n_sb).T, et)


_NT = (((1,), (1,)), ((), ()))


def _decode_attend(s, mask, s_new, mask_new, pv_fn, v_new):
    sm = jnp.where(mask, s, NEG_INF)
    sn = jnp.where(mask_new, s_new, NEG_INF)
    m = jnp.maximum(jnp.max(sm, axis=1, keepdims=True), sn)
    e = jnp.where(mask, jnp.exp(sm - m), 0.0)
    en = jnp.where(mask_new, jnp.exp(sn - m), 0.0)
    l = jnp.sum(e, axis=1, keepdims=True) + en
    acc = pv_fn(e.astype(jnp.bfloat16)) + (en.astype(jnp.bfloat16).astype(jnp.float32)
                                            * v_new.astype(jnp.bfloat16).astype(jnp.float32))
    return acc / jnp.maximum(l, 1e-30)


def _nsa_sample_kernel(pt_ref, *refs, n_pages, n_sb):
    del pt_ref
    cmp_pages = refs[:n_pages]
    sel_pages = refs[n_pages:2 * n_pages]
    (win_ref, q_ref, gl_ref, ksn_ref, vsn_ref, kwn_ref, vwn_ref, kwc_ref, vwc_ref,
     wk_ref, wv_ref, pek_ref, pev_ref, w1k_ref, w1v_ref, w2k_ref, w2v_ref, cos_ref, sin_ref, ov_ref, ex_ref,
     o_ref, wout_ref, xk_s, xv_s) = refs[2 * n_pages:]
    past = n_pages * PAGE_SIZE
    qpos = past
    n_half = past // CMP_STRIDE
    n_cmp = (past + 1 - CMP_BLOCK) // CMP_STRIDE + 1
    wb = win_ref.shape[-1]
    bf = jnp.bfloat16

    for p in range(n_pages):
        xk_s[p * PAGE_SIZE:(p + 1) * PAGE_SIZE, :] = cmp_pages[p][0, 0, 0].T
        xv_s[p * PAGE_SIZE:(p + 1) * PAGE_SIZE, :] = cmp_pages[p][0, 0, 1].T
    row_c = lax.broadcasted_iota(jnp.int32, (n_half, KV_W), 0)
    kcmp = _compress_rows(lambda j: xk_s[pl.ds(j, n_half, stride=CMP_STRIDE), :],
                          wk_ref, pek_ref, w1k_ref, w2k_ref, n_half)
    kcmp = jnp.where(row_c < n_cmp, _rope_lanes(kcmp, cos_ref[...], sin_ref[...]), 0.0).astype(bf)
    vcmp = _compress_rows(lambda j: xv_s[pl.ds(j, n_half, stride=CMP_STRIDE), :],
                          wv_ref, pev_ref, w1v_ref, w2v_ref, n_half)
    vcmp = jnp.where(row_c < n_cmp, vcmp, 0.0).astype(bf)

    row8 = lax.broadcasted_iota(jnp.int32, (8, KV_W), 0)
    lane8 = lax.broadcasted_iota(jnp.int32, (8, KV_W), 1)
    top1 = lax.broadcasted_iota(jnp.int32, (8, 1), 0) < GROUP
    q8 = q_ref[0] * SCALE
    q2 = jnp.where((row8 < GROUP) == (lane8 < HEAD_DIM), jnp.concatenate([q8, q8], axis=1), 0.0).astype(bf)
    q2f = q2.astype(jnp.float32)

    def halves(x):
        return jnp.where(top1, x[:, :HEAD_DIM], x[:, HEAD_DIM:])

    def new_score(k_new):
        return jnp.sum(q2f * k_new.astype(bf).astype(jnp.float32), axis=1, keepdims=True)

    sc = lax.dot_general(q2, kcmp, _NT, preferred_element_type=jnp.float32)
    nl = lax.broadcasted_iota(jnp.int32, (8, n_half), 1)
    pc = _softmax_rows(sc, ((nl * CMP_STRIDE + (CMP_BLOCK - 1)) <= qpos) & (nl < n_cmp))
    o_cmp = halves(jnp.dot(pc.astype(bf), vcmp, preferred_element_type=jnp.float32))
    pk0 = jnp.sum(jnp.where(top1, pc, 0.0), axis=0, keepdims=True)
    pk1 = jnp.sum(jnp.where(top1, 0.0, pc), axis=0, keepdims=True)
    rown = lax.broadcasted_iota(jnp.int32, (8, n_half), 0)
    p2 = jnp.where(rown == 0, pk0, jnp.where(rown == 1, pk1, 0.0))
    p_hi = p2.astype(bf)
    p_lo = (p2 - p_hi.astype(jnp.float32)).astype(bf)
    imp = (jnp.dot(p_hi, ov_ref[...], preferred_element_type=jnp.float32)
           + jnp.dot(p_lo, ov_ref[...], preferred_element_type=jnp.float32))
    sel2 = _select_blocks(imp, jnp.full((8, 1), qpos, jnp.int32), n_sb)

    picked2 = jnp.dot(sel2.astype(bf), ex_ref[...], preferred_element_type=jnp.float32)
    mask_s = jnp.where(top1, picked2[0:1], picked2[1:2]) > 0.5
    seln = jnp.sum(jnp.where(lane8 == qpos // SEL_BLOCK, sel2, 0.0), axis=1, keepdims=True)
    mask_new = jnp.where(top1, seln[0:1], seln[1:2]) > 0.5
    s_s = jnp.concatenate([jnp.dot(q2, sel_pages[p][0, 0, 0].astype(bf), preferred_element_type=jnp.float32)
                           for p in range(n_pages)], axis=1)

    def pv_sel(e):
        acc = jnp.zeros((8, KV_W), jnp.float32)
        for p in range(n_pages):
            acc = acc + lax.dot_general(e[:, p * PAGE_SIZE:(p + 1) * PAGE_SIZE], sel_pages[p][0, 0, 1].astype(bf),
                                        _NT, preferred_element_type=jnp.float32)
        return acc

    o_sel = halves(_decode_attend(s_s, mask_s, new_score(ksn_ref[0]), mask_new, pv_sel, vsn_ref[0]))

    s_w = jnp.dot(q2, win_ref[0, 0, 0].astype(bf), preferred_element_type=jnp.float32)
    kpos = past - wb + lax.broadcasted_iota(jnp.int32, (8, wb), 1)
    mask_w = (qpos - kpos >= 0) & (qpos - kpos < WINDOW) & (kpos >= 0)
    pv_win = lambda e: lax.dot_general(e, win_ref[0, 0, 1].astype(bf), _NT, preferred_element_type=jnp.float32)
    o_win = halves(_decode_attend(s_w, mask_w, new_score(kwn_ref[0]), jnp.full((8, 1), True), pv_win, vwn_ref[0]))

    gate = jax.nn.sigmoid(gl_ref[0])
    o_ref[0] = gate[:, 0:1] * o_cmp + gate[:, 1:2] * o_sel + gate[:, 2:3] * o_win

    lane_w = lax.broadcasted_iota(jnp.int32, (KV_W, wb), 1)
    wout_ref[0, 0] = jnp.where(lane_w == wb - 1, kwc_ref[0], pltpu.roll(win_ref[0, 0, 0], wb - 1, 1))
    wout_ref[0, 1] = jnp.where(lane_w == wb - 1, vwc_ref[0], pltpu.roll(win_ref[0, 0, 1], wb - 1, 1))


def _cache_rows_on_lanes(c):
    nd = c.ndim
    c = jnp.moveaxis(c, nd - 4, nd - 1)
    return c.reshape(c.shape[:-3] + (c.shape[-3] * c.shape[-2], c.shape[-1]))


def _nsa_sample_pallas(layer, q, gl, ks, vs, kw, vw, cmpw, cmp_t, sel_t, win_t, page_table):
    B = q.shape[0]
    n_pages = page_table.shape[1]
    past = n_pages * PAGE_SIZE
    wb = win_t.shape[-1]
    n_half = past // CMP_STRIDE
    n_cmp = (past + 1 - CMP_BLOCK) // CMP_STRIDE + 1
    n_sb = -(-(past + 1) // SEL_BLOCK)
    wk1, pek, wk2, wv1, pev, wv2 = cmpw
    wk, pekf, w1k, w2k = _compress_weights(wk1, pek, wk2)
    wv, pevf, w1v, w2v = _compress_weights(wv1, pev, wv2)
    cos, sin = _rope_tables(jnp.arange(n_half) * CMP_STRIDE + CMP_BLOCK - 1, N_KV_HEADS)
    ov = _overlap_matrix(n_cmp, n_sb)[:n_half]
    ex = jnp.asarray((np.arange(past)[None, :] // SEL_BLOCK) == np.arange(128)[:, None], jnp.bfloat16)
    consts = (wk, wv, pekf, pevf, w1k, w1v, w2k, w2v, cos, sin, ov, ex)
    row3 = lambda x: x.reshape(B, 1, KV_W)
    col3 = lambda x: x.reshape(B, KV_W, 1)
    per_b = (q.reshape(B, N_HEADS, HEAD_DIM), gl.reshape(B, N_HEADS, 3), row3(ks), row3(vs), row3(kw), row3(vw),
             col3(kw), col3(vw))
    page_spec = lambda p: pl.BlockSpec((1, 1, 2, KV_W, PAGE_SIZE), lambda b, pt: (layer, pt[b, p], 0, 0, 0))
    b_spec = lambda a: pl.BlockSpec((1,) + a.shape[1:], lambda b, pt: (b,) + (0,) * (a.ndim - 1))
    full = lambda a: pl.BlockSpec(a.shape, lambda b, pt: (0,) * a.ndim)
    in_specs = ([page_spec(p) for p in range(n_pages)] * 2
                + [pl.BlockSpec((1, 1, 2, KV_W, wb), lambda b, pt: (layer, b, 0, 0, 0))]
                + [b_spec(a) for a in per_b] + [full(a) for a in consts])
    return pl.pallas_call(
        functools.partial(_nsa_sample_kernel, n_pages=n_pages, n_sb=n_sb),
        out_shape=(jax.ShapeDtypeStruct((B, N_HEADS, HEAD_DIM), jnp.float32),
                   jax.ShapeDtypeStruct((B, 2, KV_W, wb), jnp.float32)),
        grid_spec=pltpu.PrefetchScalarGridSpec(
            num_scalar_prefetch=1, grid=(B,), in_specs=in_specs,
            out_specs=(pl.BlockSpec((1, N_HEADS, HEAD_DIM), lambda b, pt: (b, 0, 0)),
                       pl.BlockSpec((1, 2, KV_W, wb), lambda b, pt: (b, 0, 0, 0))),
            scratch_shapes=[pltpu.VMEM((past, KV_W), jnp.float32), pltpu.VMEM((past, KV_W), jnp.float32)]),
        compiler_params=pltpu.CompilerParams(dimension_semantics=("arbitrary",), vmem_limit_bytes=VMEM_LIMIT_BYTES),
        name="nsa_sample",
    )(page_table, *([cmp_t] * n_pages), *([sel_t] * n_pages), win_t, *per_b, *consts)


SSM_N = SSM_GROUPS * SSM_STATE
SSM_LANE_BLK = 512
SSM_TL = 64


def _ssm_kernel(u_ref, h0r_ref, h0i_ref, ar_ref, ai_ref, bm_ref, cr_ref, ci_ref, d_ref,
                y_ref, hr_ref, hi_ref, xr_s, xi_s, *, tl, nb):
    c = pl.program_id(0)

    @pl.when(c == 0)
    def _():
        hr_ref[...] = h0r_ref[...]
        hi_ref[...] = h0i_ref[...]

    u = u_ref[...]
    ub = u.astype(jnp.bfloat16)
    n_grp = SSM_WIDTH // 128
    for j in range(n_grp):
        bu = jnp.dot(ub[:, 128 * j:128 * (j + 1)], bm_ref[j], preferred_element_type=jnp.float32)
        xr_s[:, 512 * j:512 * (j + 1)] = bu[:, :512]
        xi_s[:, 512 * j:512 * (j + 1)] = bu[:, 512:]

    for lb in range(SSM_N // SSM_LANE_BLK):
        sl = slice(lb * SSM_LANE_BLK, (lb + 1) * SSM_LANE_BLK)
        ar = jnp.broadcast_to(ar_ref[:, sl], (8, SSM_LANE_BLK))
        ai = jnp.broadcast_to(ai_ref[:, sl], (8, SSM_LANE_BLK))
        for r in range(nb // 8):
            def step(t, carry):
                hr, hi = carry
                row = pl.multiple_of(t * nb + r * 8, 8)
                xr = xr_s[pl.ds(row, 8), sl]
                xi = xi_s[pl.ds(row, 8), sl]
                nr = ar * hr - ai * hi + xr
                ni = ar * hi + ai * hr + xi
                xr_s[pl.ds(row, 8), sl] = nr
                xi_s[pl.ds(row, 8), sl] = ni
                return nr, ni

            hr, hi = lax.fori_loop(0, tl, step, (hr_ref[r * 8:(r + 1) * 8, sl], hi_ref[r * 8:(r + 1) * 8, sl]))
            hr_ref[r * 8:(r + 1) * 8, sl] = hr
            hi_ref[r * 8:(r + 1) * 8, sl] = hi

    for j in range(n_grp):
        yr = jnp.dot(xr_s[:, 512 * j:512 * (j + 1)].astype(jnp.bfloat16), cr_ref[j], preferred_element_type=jnp.float32)
        yi = jnp.dot(xi_s[:, 512 * j:512 * (j + 1)].astype(jnp.bfloat16), ci_ref[j], preferred_element_type=jnp.float32)
        y_ref[:, 128 * j:128 * (j + 1)] = yr - yi + d_ref[:, 128 * j:128 * (j + 1)] * u[:, 128 * j:128 * (j + 1)]


def _ssm_params(a_re, a_im, log_dt, b_re, b_im, c_re, c_im, d_skip):
    dt = jnp.exp(log_dt)[:, None]
    mag = jnp.exp(dt * a_re)
    ab_re = mag * jnp.cos(dt * a_im)
    ab_im = mag * jnp.sin(dt * a_im)
    den = a_re * a_re + a_im * a_im
    zr = ((ab_re - 1.0) * a_re + ab_im * a_im) / den
    zi = (ab_im * a_re - (ab_re - 1.0) * a_im) / den
    bb_re = zr[..., None] * b_re - zi[..., None] * b_im
    bb_im = zr[..., None] * b_im + zi[..., None] * b_re
    n_grp = SSM_WIDTH // 128
    gpl = 128 // SSM_GROUP
    eye = jnp.eye(gpl, dtype=jnp.float32)

    def b_blocks(bb):
        x = bb.reshape(n_grp, gpl, SSM_STATE, SSM_GROUP)
        return jnp.einsum('jgpc,gh->jgchp', x, eye).reshape(n_grp, 128, gpl * SSM_STATE)

    def c_blocks(cc):
        x = cc.reshape(n_grp, gpl, SSM_GROUP, SSM_STATE)
        return jnp.einsum('jgcp,gh->jgphc', x, eye).reshape(n_grp, gpl * SSM_STATE, 128)

    bm = jnp.concatenate([b_blocks(bb_re), b_blocks(bb_im)], axis=-1).astype(jnp.bfloat16)
    return (ab_re.reshape(1, SSM_N), ab_im.reshape(1, SSM_N), bm,
            c_blocks(c_re).astype(jnp.bfloat16), c_blocks(c_im).astype(jnp.bfloat16), d_skip.reshape(1, SSM_WIDTH))


def _ssm_pallas(u, h0_re, h0_im, params):
    B, T, _ = u.shape
    ab_re, ab_im, bm, cr, ci, d = params
    tl = _pick_tile(T, (SSM_TL,))
    u_tb = u.transpose(1, 0, 2).reshape(T * B, SSM_WIDTH)
    full = lambda a: pl.BlockSpec(a.shape, lambda c: (0,) * a.ndim)
    h0r = h0_re.reshape(B, SSM_N)
    h0i = h0_im.reshape(B, SSM_N)
    consts = (h0r, h0i, ab_re, ab_im, bm, cr, ci, d)
    y, hr, hi = pl.pallas_call(
        functools.partial(_ssm_kernel, tl=tl, nb=B),
        out_shape=(jax.ShapeDtypeStruct((T * B, SSM_WIDTH), jnp.float32),
                   jax.ShapeDtypeStruct((B, SSM_N), jnp.float32), jax.ShapeDtypeStruct((B, SSM_N), jnp.float32)),
        grid=(T // tl,),
        in_specs=[pl.BlockSpec((tl * B, SSM_WIDTH), lambda c: (c, 0))] + [full(a) for a in consts],
        out_specs=(pl.BlockSpec((tl * B, SSM_WIDTH), lambda c: (c, 0)),
                   pl.BlockSpec((B, SSM_N), lambda c: (0, 0)), pl.BlockSpec((B, SSM_N), lambda c: (0, 0))),
        scratch_shapes=[pltpu.VMEM((tl * B, SSM_N), jnp.float32), pltpu.VMEM((tl * B, SSM_N), jnp.float32)],
        compiler_params=pltpu.CompilerParams(dimension_semantics=("arbitrary",), vmem_limit_bytes=VMEM_LIMIT_BYTES),
        name="ssm",
    )(u_tb, *consts)
    y = y.reshape(T, B, SSM_WIDTH).transpose(1, 0, 2)
    return y, hr.reshape(B, SSM_GROUPS, SSM_STATE), hi.reshape(B, SSM_GROUPS, SSM_STATE)


def _rmsnorm(x, g):
    xf = x.astype(jnp.float32)
    y = xf * lax.rsqrt(jnp.mean(xf * xf, axis=-1, keepdims=True) + EPS)
    return (y * g.astype(jnp.float32)).astype(x.dtype)


def _rope(x, pos):
    half = HEAD_DIM // 2
    inv = jnp.float32(ROPE_THETA) ** (-jnp.arange(half, dtype=jnp.float32) / half)
    ang = pos.astype(jnp.float32)[:, None] * inv[None, :]
    cos = jnp.cos(ang)[:, None, :]
    sin = jnp.sin(ang)[:, None, :]
    x1, x2 = x[..., :half], x[..., half:]
    return jnp.concatenate([x1 * cos - x2 * sin, x2 * cos + x1 * sin], axis=-1)


def _masked_softmax(s, mask):
    s = jnp.where(mask, s.astype(jnp.float32), NEG_INF)
    m = jnp.max(s, axis=-1, keepdims=True)
    e = jnp.where(mask, jnp.exp(s - m), 0.0)
    return e / jnp.maximum(jnp.sum(e, axis=-1, keepdims=True), 1e-30)


def _attend(q, k, v, mask):
    s = jnp.einsum('btkgd,bskd->bkgts', q, k) * SCALE
    p = _masked_softmax(s, mask)
    o = jnp.einsum('bkgts,bskd->btkgd', p.astype(v.dtype), v)
    return o, p


def _window_mask(q_pos, k_pos):
    d = q_pos[:, None] - k_pos[None, :]
    return (d >= 0) & (d < WINDOW) & (k_pos[None, :] >= 0)


def _compress(x, w1, pe, w2):
    B, L, K, D = x.shape
    xt = x.transpose(0, 2, 1, 3).reshape(B * K, L, D)
    hdn = lax.conv_general_dilated(xt, w1.astype(xt.dtype), (CMP_STRIDE,), 'VALID',
                                   dimension_numbers=('NWC', 'WIO', 'NWC'))
    hdn = hdn + jnp.einsum('ld,ldf->f', pe, w1)
    out = jax.nn.gelu(hdn) @ w2
    n = out.shape[1]
    return out.reshape(B, K, n, D).transpose(0, 2, 1, 3)


def _cmp_branch(q, kc_full, vc_full, q_pos, wk1, pek, wk2, wv1, pev, wv2):
    kcmp = _compress(kc_full, wk1, pek, wk2)
    vcmp = _compress(vc_full, wv1, pev, wv2)
    n_cmp = kcmp.shape[1]
    start = jnp.arange(n_cmp) * CMP_STRIDE
    end = start + CMP_BLOCK - 1
    kcmp = _rope(kcmp, end)
    o, p = _attend(q, kcmp, vcmp, end[None, :] <= q_pos[:, None])
    L = kc_full.shape[1]
    n_sb = -(-L // SEL_BLOCK)
    sb_start = jnp.arange(n_sb) * SEL_BLOCK
    overlap = ((start[:, None] < sb_start[None, :] + SEL_BLOCK) & (end[:, None] >= sb_start[None, :])).astype(jnp.float32)
    imp = jnp.einsum('bkgtn,nj->btkj', p, overlap)
    blk = jnp.arange(n_sb)
    forced = (blk[None, :] == 0) | (blk[None, :] == (q_pos // SEL_BLOCK)[:, None])
    causal = sb_start[None, :] <= q_pos[:, None]
    imp = jnp.where(forced[None, :, None, :], FORCE_SCORE, imp)
    imp = jnp.where(causal[None, :, None, :], imp, NEG_INF)
    top_v, top_i = lax.top_k(imp, min(N_SEL, n_sb))
    return o, top_i, top_v > 0.5 * NEG_INF


def _sel_attend(q, idx, valid, q_pos, fetch):
    pos = idx[..., None] * SEL_BLOCK + jnp.arange(SEL_BLOCK)
    k, v = fetch(pos)
    B, Tc, K, N, S = pos.shape
    mask = (valid[..., None] & (pos <= q_pos[None, :, None, None, None])).reshape(B, Tc, K, N * S)
    k = k.reshape(B, Tc, K, N * S, HEAD_DIM)
    v = v.reshape(B, Tc, K, N * S, HEAD_DIM)
    s = jnp.einsum('btkgd,btksd->btkgs', q, k) * SCALE
    p = _masked_softmax(s, mask[:, :, :, None, :])
    return jnp.einsum('btkgs,btksd->btkgd', p.astype(v.dtype), v)


def _to_chunks(x, c):
    B, T = x.shape[:2]
    return x.reshape((B, T // c, c) + x.shape[2:]).swapaxes(0, 1)


def _from_chunks(x):
    nc, B, c = x.shape[:3]
    return x.swapaxes(0, 1).reshape((B, nc * c) + x.shape[3:])


def _batch_head_index(B):
    bi = jnp.arange(B)[:, None, None, None, None]
    hd = jnp.arange(N_KV_HEADS)[None, None, :, None, None]
    return bi, hd


def _nsa_prompt(q, kc, vc, ks, vs, kw, vw, cmpw):
    B, T = q.shape[:2]
    pos = jnp.arange(T)
    o_cmp, idx, valid = _cmp_branch(q, kc, vc, pos, *cmpw)
    bi, hd = _batch_head_index(B)

    def fetch(p):
        pc = jnp.clip(p, 0, T - 1)
        return ks[bi, pc, hd], vs[bi, pc, hd]

    qb = Q_BLOCK if T % Q_BLOCK == 0 else T
    o_sel = _from_chunks(lax.map(lambda a: _sel_attend(a[0], a[1], a[2], a[3], fetch),
                                 (_to_chunks(q, qb), _to_chunks(idx, qb), _to_chunks(valid, qb), pos.reshape(-1, qb))))
    kp = jnp.pad(kw, ((0, 0), (WINDOW, 0), (0, 0), (0, 0)))
    vp = jnp.pad(vw, ((0, 0), (WINDOW, 0), (0, 0), (0, 0)))

    def win_block(a):
        c, qc = a
        start = c * qb
        kb = lax.dynamic_slice_in_dim(kp, start, WINDOW + qb, axis=1)
        vb = lax.dynamic_slice_in_dim(vp, start, WINDOW + qb, axis=1)
        qpos = start + jnp.arange(qb)
        kpos = start - WINDOW + jnp.arange(WINDOW + qb)
        return _attend(qc, kb, vb, _window_mask(qpos, kpos))[0]

    o_win = _from_chunks(lax.map(win_block, (jnp.arange(T // qb), _to_chunks(q, qb))))
    n_keep = min(WINDOW, T)
    rows = (jnp.stack([kc, vc], axis=2), jnp.stack([ks, vs], axis=2), jnp.stack([kw, vw], axis=2)[:, T - n_keep:])
    return o_cmp, o_sel, o_win, rows


def _nsa_sample(q, kc, vc, ks, vs, kw, vw, cmpw, cmp_pool, sel_pool, win_buf, page_table):
    B, T = q.shape[:2]
    past = page_table.shape[1] * PAGE_SIZE
    pos = past + jnp.arange(T)
    past_cmp = cmp_pool[page_table].reshape((B, past) + cmp_pool.shape[2:])
    kc_full = jnp.concatenate([past_cmp[:, :, 0], kc], axis=1)
    vc_full = jnp.concatenate([past_cmp[:, :, 1], vc], axis=1)
    o_cmp, idx, valid = _cmp_branch(q, kc_full, vc_full, pos, *cmpw)
    sel_rows = sel_pool.reshape((-1,) + sel_pool.shape[2:])
    bi, hd = _batch_head_index(B)

    def fetch(p):
        pc = jnp.clip(p, 0, past - 1)
        phys = page_table[bi, pc // PAGE_SIZE] * PAGE_SIZE + pc % PAGE_SIZE
        pn = jnp.clip(p - past, 0, T - 1)
        in_past = (p < past)[..., None]
        k = jnp.where(in_past, sel_rows[phys, 0, hd], ks[bi, pn, hd])
        v = jnp.where(in_past, sel_rows[phys, 1, hd], vs[bi, pn, hd])
        return k, v

    o_sel = _sel_attend(q, idx, valid, pos, fetch)
    wb = win_buf.shape[1]
    win_all = jnp.concatenate([win_buf, jnp.stack([kw, vw], axis=2)], axis=1)
    kpos = past - wb + jnp.arange(wb + T)
    o_win, _ = _attend(q, win_all[:, :, 0], win_all[:, :, 1], _window_mask(pos, kpos))
    rows = (jnp.stack([kc, vc], axis=2), jnp.stack([ks, vs], axis=2), win_all[:, T:])
    return o_cmp, o_sel, o_win, rows


def _cmul_combine(e1, e2):
    ar1, ai1, br1, bi1 = e1
    ar2, ai2, br2, bi2 = e2
    return (ar2 * ar1 - ai2 * ai1, ar2 * ai1 + ai2 * ar1,
            ar2 * br1 - ai2 * bi1 + br2, ar2 * bi1 + ai2 * br1 + bi2)


def _ssm_scan(u, h0_re, h0_im, a_re, a_im, log_dt, b_re, b_im, c_re, c_im, d_skip):
    B, T, _ = u.shape
    uf = u.reshape(B, T, SSM_GROUPS, SSM_GROUP)
    ar = a_re
    ai = a_im
    dt = jnp.exp(log_dt)[:, None]
    mag = jnp.exp(dt * ar)
    ab_re = mag * jnp.cos(dt * ai)
    ab_im = mag * jnp.sin(dt * ai)
    den = ar * ar + ai * ai
    zr = ((ab_re - 1.0) * ar + ab_im * ai) / den
    zi = (ab_im * ar - (ab_re - 1.0) * ai) / den
    bb_re = zr[..., None] * b_re - zi[..., None] * b_im
    bb_im = zr[..., None] * b_im + zi[..., None] * b_re
    bu_re = jnp.einsum('gpc,btgc->tbgp', bb_re, uf)
    bu_im = jnp.einsum('gpc,btgc->tbgp', bb_im, uf)
    chunk = SSM_CHUNK if T % SSM_CHUNK == 0 else T
    nc = T // chunk
    shp = (chunk, B, SSM_GROUPS, SSM_STATE)
    bu_re = bu_re.reshape((nc,) + shp)
    bu_im = bu_im.reshape((nc,) + shp)
    a_re_c = jnp.broadcast_to(ab_re, shp)
    a_im_c = jnp.broadcast_to(ab_im, shp)

    def step(carry, xs):
        hr0, hi0 = carry
        xr, xi = xs
        pr, pim, sr, si = lax.associative_scan(_cmul_combine, (a_re_c, a_im_c, xr, xi), axis=0)
        hr = sr + pr * hr0 - pim * hi0
        hi = si + pr * hi0 + pim * hr0
        y = jnp.einsum('gcp,tbgp->tbgc', c_re, hr) - jnp.einsum('gcp,tbgp->tbgc', c_im, hi)
        return (hr[-1], hi[-1]), y

    (hr, hi), y = lax.scan(step, (h0_re, h0_im), (bu_re, bu_im))
    y = y.reshape(T, B, SSM_GROUPS, SSM_GROUP).transpose(1, 0, 2, 3) + d_skip * uf
    return y.reshape(B, T, SSM_WIDTH), hr, hi


def _block(h, p_l, pos, lw, sample, h0_re, h0_im, conv_prefix):
    (g_attn, w_in, g_q, g_kc, g_ks, g_kw, wk1, pek, wk2, wv1, pev, wv2,
     a_re, a_im, log_dt, b_re, b_im, c_re, c_im, d_skip,
     w_a, w_glu1, w_glu2, w_o, g_ffn, w_up, conv_w, conv_b, w_down,
     g_ple, w_ple_gate, w_ple) = lw
    B, T, _ = h.shape
    N = B * T

    def mm(x, w, g=None):
        return _matmul(x.reshape(N, x.shape[-1]), w, g).reshape(B, T, w.shape[1])

    z = mm(h, w_in, g_attn)
    pts = [int(s) for s in np.cumsum(SPLIT_SIZES)[:-1]]
    q, kc, vc, ks, vs, kw, vw, gl, u, ga, gb = jnp.split(z, pts, axis=-1)

    def heads(t):
        return t.reshape(B, T, -1, HEAD_DIM)

    q = _rope(_rmsnorm(heads(q), g_q), pos).reshape(B, T, N_KV_HEADS, GROUP, HEAD_DIM)
    kc = _rmsnorm(heads(kc), g_kc)
    ks = _rope(_rmsnorm(heads(ks), g_ks), pos)
    kw = _rope(_rmsnorm(heads(kw), g_kw), pos)
    vc, vs, vw = heads(vc), heads(vs), heads(vw)
    cmpw = (wk1, pek, wk2, wv1, pev, wv2)
    flat = lambda t: t.reshape(B, T, KV_W)
    if sample is None:
        kcmp, vcmp = _compress_pallas(flat(kc), flat(vc), cmpw)
        o = _nsa_prompt_pallas(q.reshape(B, T, Q_W), kcmp, vcmp, flat(ks), flat(vs), flat(kw), flat(vw), gl)
        n_keep = min(WINDOW, T)
        rows = (jnp.stack([kc, vc], axis=2), jnp.stack([ks, vs], axis=2),
                jnp.stack([kw, vw], axis=2)[:, T - n_keep:])
    else:
        assert T == 1
        layer, cmp_t, sel_t, win_t, page_table = sample
        f2 = lambda t: t.reshape(B, KV_W)
        o, wnew = _nsa_sample_pallas(layer, q.reshape(B, Q_W), gl.reshape(B, -1), f2(ks), f2(vs), f2(kw), f2(vw),
                                     cmpw, cmp_t, sel_t, win_t, page_table)
        wb = wnew.shape[-1]
        wnew = wnew.reshape(B, 2, N_KV_HEADS, HEAD_DIM, wb).transpose(0, 4, 1, 2, 3)
        rows = (jnp.stack([kc, vc], axis=2), jnp.stack([ks, vs], axis=2), wnew)
    a_out = mm(o.reshape(B, T, Q_W), w_a)
    y, hr, hi = _ssm_pallas(u, h0_re, h0_im, _ssm_params(a_re, a_im, log_dt, b_re, b_im, c_re, c_im, d_skip))
    yg = jax.nn.gelu(y)
    b_out = mm(yg, w_glu1) * jax.nn.sigmoid(mm(yg, w_glu2))
    h = h + mm(jax.nn.sigmoid(ga) * a_out + jax.nn.sigmoid(gb) * b_out, w_o)
    up = mm(h, w_up, g_ffn)
    gp, val = jnp.split(up, 2, axis=-1)
    ext = jnp.concatenate([conv_prefix, gp], axis=1)
    conv = conv_b
    for j in range(CONV_W):
        conv = conv + conv_w[j] * ext[:, j:j + T]
    h = h + mm(jax.nn.gelu(conv) * val, w_down)
    h = h + jax.nn.sigmoid(mm(h, w_ple_gate, g_ple)) * mm(p_l, w_ple)
    return h, rows, hr, hi, ext[:, T:]


def kernel(x_prompt, x_sample, cache_cmp, cache_sel, cache_win, state_ssm_re, state_ssm_im, state_conv, page_table, p_prompt, p_sample, g_attn, w_in, g_q, g_kc, g_ks, g_kw, cmp_wk1, cmp_pek, cmp_wk2, cmp_wv1, cmp_pev, cmp_wv2, ssm_a_re, ssm_a_im, ssm_log_dt, ssm_b_re, ssm_b_im, ssm_c_re, ssm_c_im, ssm_d, w_a, w_glu1, w_glu2, w_o, g_ffn, w_up, conv_w, conv_b, w_down, g_ple, w_ple_gate, w_ple):
    Bp, Tp = x_prompt.shape[:2]
    Ts = x_sample.shape[1]
    depth = w_in.shape[0]
    past = page_table.shape[1] * PAGE_SIZE
    pos_p = jnp.arange(Tp)
    pos_s = past + jnp.arange(Ts)
    zeros_h = jnp.zeros((Bp, SSM_GROUPS, SSM_STATE), x_prompt.dtype)
    zeros_c = jnp.zeros((Bp, CONV_W - 1, D_FF), x_prompt.dtype)
    layer_w = (g_attn, w_in, g_q, g_kc, g_ks, g_kw, cmp_wk1, cmp_pek, cmp_wk2, cmp_wv1, cmp_pev, cmp_wv2,
               ssm_a_re, ssm_a_im, ssm_log_dt, ssm_b_re, ssm_b_im, ssm_c_re, ssm_c_im, ssm_d,
               w_a, w_glu1, w_glu2, w_o, g_ffn, w_up, conv_w, conv_b, w_down, g_ple, w_ple_gate, w_ple)
    cmp_t, sel_t, win_t = (_cache_rows_on_lanes(c) for c in (cache_cmp, cache_sel, cache_win))
    st = [[] for _ in range(12)]
    hp, hs = x_prompt, x_sample
    for i in range(depth):
        lw = [w[i] for w in layer_w]
        hp, rows, hr, hi, cv = _block(hp, p_prompt[i], pos_p, lw, None, zeros_h, zeros_h, zeros_c)
        for j, a in enumerate(list(rows) + [hr, hi, cv]):
            st[j].append(a)
        hs, rows, hr, hi, cv = _block(hs, p_sample[i], pos_s, lw, (i, cmp_t, sel_t, win_t, page_table),
                                      state_ssm_re[i], state_ssm_im[i], state_conv[i])
        for j, a in enumerate(list(rows) + [hr, hi, cv]):
            st[6 + j].append(a)
    return (hp, hs) + tuple(jnp.stack(s) for s in st)
```

```python
import functools
import math

import numpy as np
import jax
import jax.numpy as jnp
from jax import lax
from jax.experimental import pallas as pl
from jax.experimental.pallas import tpu as pltpu

D_MODEL = 1024
N_HEADS = 8
N_KV_HEADS = 2
HEAD_DIM = 64
GROUP = N_HEADS // N_KV_HEADS
Q_W = N_HEADS * HEAD_DIM
KV_W = N_KV_HEADS * HEAD_DIM
CMP_BLOCK = 32
CMP_STRIDE = 16
SEL_BLOCK = 64
N_SEL = 8
WINDOW = 512
Q_BLOCK = 128
PAGE_SIZE = 128
ROPE_THETA = 10000.0
SSM_WIDTH = D_MODEL // 2
SSM_GROUP = 16
SSM_GROUPS = SSM_WIDTH // SSM_GROUP
SSM_STATE = 64
SSM_CHUNK = 128
D_FF = 11 * D_MODEL // 4
CONV_W = 3
EPS = 1e-6
NEG_INF = -1e30
FORCE_SCORE = 1e9
SCALE = HEAD_DIM ** -0.5
SPLIT_SIZES = (Q_W, KV_W, KV_W, KV_W, KV_W, KV_W, KV_W, 3 * N_HEADS, SSM_WIDTH, D_MODEL, D_MODEL)

VMEM_LIMIT_BYTES = 56 * 1024 * 1024


def _pick_tile(n, cands):
    for c in cands:
        if n % c == 0:
            return c
    return n


def _rms(x, g):
    return x * lax.rsqrt(jnp.mean(x * x, axis=-1, keepdims=True) + EPS) * g


def _bdot(a, b):
    return jnp.dot(a.astype(jnp.bfloat16), b, preferred_element_type=jnp.float32)


GL_PAD = 128
_IN_WIDTHS = (Q_W, 6 * KV_W, GL_PAD, SSM_WIDTH, 2 * D_MODEL)
_IN_OFFS = tuple(int(v) for v in np.cumsum((0,) + _IN_WIDTHS))
ROW_TILE = 256


def _in_proj_kernel(h_ref, g_ref, w_ref, gq_ref, gk_ref, ones_ref, cos_ref, sin_ref,
                    q_ref, cmp_ref, sel_ref, win_ref, gl_ref, u_ref, gab_ref):
    xn = _rms(h_ref[...], g_ref[...]).astype(jnp.bfloat16)
    seg = lambda s: jnp.dot(xn, w_ref[:, _IN_OFFS[s]:_IN_OFFS[s + 1]], preferred_element_type=jnp.float32)
    cos, sin = cos_ref[...], sin_ref[...]
    ones = ones_ref[...]

    def head_norm(x, gain):
        x2 = x * x
        hi = x2.astype(jnp.bfloat16)
        lo = (x2 - hi.astype(jnp.float32)).astype(jnp.bfloat16)
        ss = (jnp.dot(hi, ones, preferred_element_type=jnp.float32)
              + jnp.dot(lo, ones, preferred_element_type=jnp.float32))
        return x * lax.rsqrt(ss * (1.0 / HEAD_DIM) + EPS) * gain

    zq = seg(0)
    for c in range(Q_W // KV_W):
        qn = _rope_lanes(head_norm(zq[:, c * KV_W:(c + 1) * KV_W], gq_ref[...]), cos, sin)
        q_ref[:, c * KV_W:(c + 1) * KV_W] = (qn * SCALE).astype(q_ref.dtype)
    zkv = seg(1)
    part = lambda c: zkv[:, c * KV_W:(c + 1) * KV_W]
    cmp_ref[:, :KV_W] = head_norm(part(0), gk_ref[0:1])
    cmp_ref[:, KV_W:] = part(1)
    sel_ref[:, :KV_W] = _rope_lanes(head_norm(part(2), gk_ref[1:2]), cos, sin)
    sel_ref[:, KV_W:] = part(3)
    win_ref[:, :KV_W] = _rope_lanes(head_norm(part(4), gk_ref[2:3]), cos, sin)
    win_ref[:, KV_W:] = part(5)
    gl_ref[...] = seg(2)
    u_ref[...] = seg(3)
    gab_ref[...] = seg(4)


def _pad_w_in(w_in):
    a = Q_W + 6 * KV_W + 3 * N_HEADS
    pad = jnp.zeros(w_in.shape[:-1] + (GL_PAD - 3 * N_HEADS,), w_in.dtype)
    return jnp.concatenate([w_in[..., :a], pad, w_in[..., a:]], axis=-1).astype(jnp.bfloat16)


def _in_proj_pallas(h2d, w_in_p, g_attn, g_q, g_kc, g_ks, g_kw, pos, seq_len):
    N = h2d.shape[0]
    T = seq_len
    B = N // T
    tm = _pick_tile(N, (ROW_TILE, 128))
    nt = max(T // tm, 1)
    assert T == 1 or T % tm == 0
    cos, sin = _rope_tables(pos, N_KV_HEADS)
    if T == 1:
        cos, sin = (jnp.broadcast_to(t, (tm, KV_W)) for t in (cos, sin))
    tile2 = lambda g: jnp.tile(g.reshape(1, HEAD_DIM), (1, N_KV_HEADS))
    gk = jnp.concatenate([tile2(g_kc), tile2(g_ks), tile2(g_kw), jnp.zeros((5, KV_W), jnp.float32)], axis=0)
    ones = jnp.asarray(np.kron(np.eye(N_KV_HEADS), np.ones((HEAD_DIM, HEAD_DIM))), jnp.bfloat16)
    row = lambda w: pl.BlockSpec((tm, w), lambda i: (i, 0))
    const = lambda a: pl.BlockSpec(a.shape, lambda i: (0,) * a.ndim, pipeline_mode=pl.Buffered(1))
    tab = pl.BlockSpec((tm, KV_W), lambda i: (i % nt, 0))
    if T == 1:
        u_shape, u_spec = (N, SSM_WIDTH), row(SSM_WIDTH)
    else:
        u_shape, u_spec = (T, B * SSM_WIDTH), pl.BlockSpec((tm, SSM_WIDTH), lambda i: (i % nt, i // nt))
    consts = (g_attn.reshape(1, D_MODEL), w_in_p, tile2(g_q), gk, ones)
    f32 = jnp.float32
    return pl.pallas_call(
        _in_proj_kernel,
        out_shape=(jax.ShapeDtypeStruct((N, Q_W), jnp.bfloat16),) + (jax.ShapeDtypeStruct((N, 2 * KV_W), f32),) * 3
        + (jax.ShapeDtypeStruct((N, GL_PAD), f32), jax.ShapeDtypeStruct(u_shape, f32),
           jax.ShapeDtypeStruct((N, 2 * D_MODEL), f32)),
        grid=(N // tm,),
        in_specs=[row(D_MODEL)] + [const(a) for a in consts] + [tab, tab],
        out_specs=(row(Q_W), row(2 * KV_W), row(2 * KV_W), row(2 * KV_W), row(GL_PAD), u_spec, row(2 * D_MODEL)),
        compiler_params=pltpu.CompilerParams(dimension_semantics=("parallel",), vmem_limit_bytes=VMEM_LIMIT_BYTES),
        name="in_proj",
    )(h2d, *consts, cos, sin)


FF_CHUNK = D_FF // 2
assert FF_CHUNK % 128 == 0


def _mix_ffn_kernel(h_ref, o_ref, y_ref, gab_ref, p_ref, pre0_ref, pre1_ref,
                    wa_ref, wg1_ref, wg2_ref, wo_ref, gffn_ref, wup_ref, cw_ref, cb_ref, wdn_ref,
                    gple_ref, wpg_ref, wpl_ref, hout_ref, cs0_ref, cs1_ref, carry_s, *, seq_tiles):
    tm = h_ref.shape[0]
    a_out = jnp.dot(o_ref[...], wa_ref[...], preferred_element_type=jnp.float32)
    yg = jax.nn.gelu(y_ref[...]).astype(jnp.bfloat16)
    b_out = (jnp.dot(yg, wg1_ref[...], preferred_element_type=jnp.float32)
             * jax.nn.sigmoid(jnp.dot(yg, wg2_ref[...], preferred_element_type=jnp.float32)))
    mixed = (jax.nn.sigmoid(gab_ref[:, :D_MODEL]) * a_out + jax.nn.sigmoid(gab_ref[:, D_MODEL:]) * b_out)
    h1 = h_ref[...] + _bdot(mixed, wo_ref[...])

    xn = _rms(h1, gffn_ref[...]).astype(jnp.bfloat16)
    if seq_tiles:
        @pl.when(pl.program_id(0) % seq_tiles == 0)
        def _():
            carry_s[...] = jnp.zeros_like(carry_s)
        row = lax.broadcasted_iota(jnp.int32, (tm, 1), 0)
    ffn = jnp.zeros((tm, D_MODEL), jnp.float32)
    for c in range(D_FF // FF_CHUNK):
        sl = slice(c * FF_CHUNK, (c + 1) * FF_CHUNK)
        gp = jnp.dot(xn, wup_ref[:, sl], preferred_element_type=jnp.float32)
        val = jnp.dot(xn, wup_ref[:, D_FF + c * FF_CHUNK:D_FF + (c + 1) * FF_CHUNK], preferred_element_type=jnp.float32)
        if seq_tiles:
            old1, old2 = carry_s[7:8, sl], carry_s[6:7, sl]
            prev1 = jnp.where(row == 0, old1, pltpu.roll(gp, 1, 0))
            prev2 = jnp.where(row == 0, old2, jnp.where(row == 1, old1, pltpu.roll(gp, 2, 0)))
            carry_s[:, sl] = gp[tm - 8:, :]
            cs0_ref[0, :, sl] = gp[tm - 2:tm - 1, :]
            cs1_ref[0, :, sl] = gp[tm - 1:tm, :]
        else:
            prev2, prev1 = pre0_ref[:, sl], pre1_ref[:, sl]
            cs0_ref[:, sl] = prev1
            cs1_ref[:, sl] = gp
        conv = cb_ref[:, sl] + cw_ref[0:1, sl] * prev2 + cw_ref[1:2, sl] * prev1 + cw_ref[2:3, sl] * gp
        ffn = ffn + _bdot(jax.nn.gelu(conv) * val, wdn_ref[sl, :])
    h2 = h1 + ffn

    gate = jax.nn.sigmoid(_bdot(_rms(h2, gple_ref[...]), wpg_ref[...]))
    hout_ref[...] = h2 + gate * _bdot(p_ref[...], wpl_ref[...])


def _mix_ffn_pallas(h2d, o2d, y, gab, p2d, prefix, w, seq_len):
    (w_a, w_glu1, w_glu2, w_o, g_ffn, w_up, conv_w, conv_b, w_down, g_ple, w_ple_gate, w_ple) = w
    N = h2d.shape[0]
    T = seq_len
    B = N // T
    tm = _pick_tile(N, (ROW_TILE, 128))
    nt = max(T // tm, 1)
    seq = prefix is None
    assert (seq and T % tm == 0 and tm >= 8) or (not seq and T == 1)
    row = lambda wd: pl.BlockSpec((tm, wd), lambda i: (i, 0))
    const = lambda a: pl.BlockSpec(a.shape, lambda i: (0,) * a.ndim, pipeline_mode=pl.Buffered(1))
    f32 = jnp.float32
    if seq:
        y_spec = pl.BlockSpec((tm, SSM_WIDTH), lambda i: (i % nt, i // nt))
        pre = (jnp.zeros((8, D_FF), f32),) * 2
        pre_spec = const(pre[0])
        cs_shape = jax.ShapeDtypeStruct((B, 1, D_FF), f32)
        cs_spec = pl.BlockSpec((1, 1, D_FF), lambda i: (i // nt, 0, 0))
    else:
        y_spec = row(SSM_WIDTH)
        pre = (prefix[:, 0], prefix[:, 1])
        pre_spec = row(D_FF)
        cs_shape = jax.ShapeDtypeStruct((N, D_FF), f32)
        cs_spec = row(D_FF)
    bf = lambda a: a.astype(jnp.bfloat16)
    vec = lambda a: a.reshape(1, -1)
    cw8 = jnp.concatenate([conv_w, jnp.zeros((8 - CONV_W, D_FF), f32)], axis=0)
    consts = (bf(w_a), bf(w_glu1), bf(w_glu2), bf(w_o), vec(g_ffn), bf(w_up), cw8, vec(conv_b), bf(w_down),
              vec(g_ple), bf(w_ple_gate), bf(w_ple))
    hout, cs0, cs1 = pl.pallas_call(
        functools.partial(_mix_ffn_kernel, seq_tiles=nt if seq else 0),
        out_shape=(jax.ShapeDtypeStruct((N, D_MODEL), f32), cs_shape, cs_shape),
        grid=(N // tm,),
        in_specs=[row(D_MODEL), row(Q_W), y_spec, row(2 * D_MODEL), row(p2d.shape[1]), pre_spec, pre_spec]
        + [const(a) for a in consts],
        out_specs=(row(D_MODEL), cs_spec, cs_spec),
        scratch_shapes=[pltpu.VMEM((8, D_FF), f32)],
        compiler_params=pltpu.CompilerParams(dimension_semantics=("arbitrary",), vmem_limit_bytes=VMEM_LIMIT_BYTES),
        name="mix_ffn",
    )(h2d, o2d, y, gab, p2d, *pre, *consts)
    return hout, jnp.stack([cs0.reshape(B, D_FF), cs1.reshape(B, D_FF)], axis=1)


HALF_ROWS = CMP_BLOCK // CMP_STRIDE
assert HALF_ROWS == 2


def _rope_lanes(x, cos, sin_signed):
    w = x.shape[-1]
    half = HEAD_DIM // 2
    lane = lax.broadcasted_iota(jnp.int32, x.shape, x.ndim - 1)
    first = (lane % HEAD_DIM) < half
    partner = jnp.where(first, pltpu.roll(x, w - half, x.ndim - 1), pltpu.roll(x, half, x.ndim - 1))
    return x * cos + partner * sin_signed


def _compress_rows(load, w_ref, pe_ref, w1_ref, w2_ref, n_half):
    acc = jnp.zeros((n_half, 4 * 128), jnp.float32)
    for j in range(CMP_STRIDE):
        acc = acc + jnp.dot(load(j).astype(jnp.bfloat16), w_ref[j], preferred_element_type=jnp.float32)
    pa = acc[:, :256]
    pb = pltpu.roll(acc[:, 256:], n_half - 1, 0)
    bias = jnp.dot(pe_ref[...].astype(jnp.bfloat16), w1_ref[...], preferred_element_type=jnp.float32)[0:1]
    bias2 = jnp.concatenate([bias, bias], axis=1)
    hdn = jax.nn.gelu(pa + pb + bias2)
    return jnp.dot(hdn.astype(jnp.bfloat16), w2_ref[...], preferred_element_type=jnp.float32)


def _compress_kernel(xk_ref, xv_ref, wk_ref, wv_ref, pek_ref, pev_ref, w1k_ref, w1v_ref, w2k_ref, w2v_ref,
                     cos_ref, sin_ref, ko_ref, vo_ref, *, n_half, n_cmp):
    row = lax.broadcasted_iota(jnp.int32, (n_half, KV_W), 0)
    k = _compress_rows(lambda j: xk_ref[0, pl.ds(j, n_half, stride=CMP_STRIDE), :],
                       wk_ref, pek_ref, w1k_ref, w2k_ref, n_half)
    k = _rope_lanes(k, cos_ref[...], sin_ref[...])
    v = _compress_rows(lambda j: xv_ref[0, pl.ds(j, n_half, stride=CMP_STRIDE), :],
                       wv_ref, pev_ref, w1v_ref, w2v_ref, n_half)
    ko_ref[0] = jnp.where(row < n_cmp, k, 0.0).astype(ko_ref.dtype)
    vo_ref[0] = jnp.where(row < n_cmp, v, 0.0).astype(vo_ref.dtype)


def _blockdiag2(w):
    z = jnp.zeros_like(w)
    return jnp.concatenate([jnp.concatenate([w, z], axis=-1), jnp.concatenate([z, w], axis=-1)], axis=-2)


def _compress_weights(w1, pe, w2):
    bd = _blockdiag2(w1)
    wcat = jnp.concatenate([bd[:CMP_STRIDE], bd[CMP_STRIDE:]], axis=-1).astype(jnp.bfloat16)
    pe_flat = jnp.broadcast_to(pe.reshape(1, -1), (8, pe.size))
    w1_flat = w1.reshape(-1, w1.shape[-1]).astype(jnp.bfloat16)
    w2bd = _blockdiag2(w2).astype(jnp.bfloat16)
    return wcat, pe_flat, w1_flat, w2bd


def _rope_tables(pos, reps):
    half = HEAD_DIM // 2
    inv = jnp.float32(ROPE_THETA) ** (-jnp.arange(half, dtype=jnp.float32) / half)
    ang = pos.astype(jnp.float32)[:, None] * inv[None, :]
    cos = jnp.cos(ang)
    sin = jnp.sin(ang)
    return (jnp.tile(jnp.concatenate([cos, cos], axis=-1), (1, reps)),
            jnp.tile(jnp.concatenate([-sin, sin], axis=-1), (1, reps)))


def _compress_pallas(rows, cmpw):
    wk1, pek, wk2, wv1, pev, wv2 = cmpw
    B, L, _ = rows.shape
    n_half = L // CMP_STRIDE
    n_cmp = n_half - 1
    wk, pekf, w1k, w2k = _compress_weights(wk1, pek, wk2)
    wv, pevf, w1v, w2v = _compress_weights(wv1, pev, wv2)
    end = jnp.arange(n_half) * CMP_STRIDE + CMP_BLOCK - 1
    cos, sin = _rope_tables(end, N_KV_HEADS)
    full = lambda a: pl.BlockSpec(a.shape, lambda b: (0,) * a.ndim)
    consts = (wk, wv, pekf, pevf, w1k, w1v, w2k, w2v, cos, sin)
    return pl.pallas_call(
        functools.partial(_compress_kernel, n_half=n_half, n_cmp=n_cmp),
        out_shape=(jax.ShapeDtypeStruct((B, n_half, KV_W), jnp.bfloat16),) * 2,
        grid=(B,),
        in_specs=[pl.BlockSpec((1, L, KV_W), lambda b: (b, 0, 0)), pl.BlockSpec((1, L, KV_W), lambda b: (b, 0, 1))]
        + [full(a) for a in consts],
        out_specs=(pl.BlockSpec((1, n_half, KV_W), lambda b: (b, 0, 0)),) * 2,
        compiler_params=pltpu.CompilerParams(dimension_semantics=("parallel",), vmem_limit_bytes=VMEM_LIMIT_BYTES),
        name="compress",
    )(rows, rows, *consts)


ATT_TQ = 128
SEL_KC = 512
BIG_NEG = -3.0e38


def _softmax_rows(s, mask):
    s = jnp.where(mask, s, NEG_INF)
    m = jnp.max(s, axis=-1, keepdims=True)
    e = jnp.where(mask, jnp.exp(s - m), 0.0)
    return e / jnp.maximum(jnp.sum(e, axis=-1, keepdims=True), 1e-30)


def _select_blocks(imp, tpos, n_sb):
    tq = imp.shape[0]
    jl = lax.broadcasted_iota(jnp.int32, (tq, 128), 1)
    jf = jl.astype(jnp.float32)
    forced = (jl == 0) | (jl == (tpos >> 6))
    imp = jnp.where(forced, FORCE_SCORE, imp)
    imp = jnp.where(jl * SEL_BLOCK <= tpos, imp, NEG_INF)
    imp = jnp.where(jl < n_sb, imp, BIG_NEG)
    sel = jnp.zeros((tq, 128), jnp.float32)
    for _ in range(N_SEL):
        m = jnp.max(imp, axis=-1, keepdims=True)
        first = jnp.min(jnp.where(imp == m, jf, 1e9), axis=-1, keepdims=True)
        hit = jf == first
        sel = jnp.where(hit & (m > 0.5 * NEG_INF), 1.0, sel)
        imp = jnp.where(hit, BIG_NEG, imp)
    return sel


def _select_blocks_t(imp, tpos, n_sb):
    nj, tq = imp.shape
    jr = lax.broadcasted_iota(jnp.int32, (nj, tq), 0)
    jf = jr.astype(jnp.float32)
    forced = (jr == 0) | (jr == (tpos >> 6))
    imp = jnp.where(forced, FORCE_SCORE, imp)
    imp = jnp.where(jr * SEL_BLOCK <= tpos, imp, NEG_INF)
    imp = jnp.where(jr < n_sb, imp, BIG_NEG)
    sel = jnp.zeros((nj, tq), jnp.float32)
    for _ in range(N_SEL):
        m = jnp.max(imp, axis=0, keepdims=True)
        first = jnp.min(jnp.where(imp == m, jf, 1e9), axis=0, keepdims=True)
        hit = jf == first
        sel = jnp.where(hit & (m > 0.5 * NEG_INF), 1.0, sel)
        imp = jnp.where(hit, BIG_NEG, imp)
    return sel


def _nsa_prompt_kernel(q_ref, kcmp_ref, vcmpt_ref, ks_ref, vst_ref, kw_ref, vwt_ref, glt_ref, ovt_ref, et_ref, o_ref,
                       *, tq, n_cmp, n_sb):
    i = pl.program_id(2)
    t0 = i * tq
    bf = jnp.bfloat16
    qf = q_ref[0]
    q4 = jnp.concatenate([qf[:, g * HEAD_DIM:(g + 1) * HEAD_DIM] for g in range(GROUP)], axis=0)
    lanes4 = lambda x: jnp.concatenate([x] * GROUP, axis=1)
    tq_pos = t0 + lax.broadcasted_iota(jnp.int32, (1, tq), 1)
    tpos = lanes4(tq_pos)

    nr = lax.broadcasted_iota(jnp.int32, (128, 1), 0)
    maskc = ((nr * CMP_STRIDE + (CMP_BLOCK - 1)) <= tpos) & (nr < n_cmp)
    sc = jnp.where(maskc, lax.dot_general(kcmp_ref[0, 0], q4, _NT, preferred_element_type=jnp.float32), NEG_INF)
    ec = jnp.where(maskc, jnp.exp(sc - jnp.max(sc, axis=0, keepdims=True)), 0.0)
    pc = ec / jnp.maximum(jnp.sum(ec, axis=0, keepdims=True), 1e-30)
    o_cmp = jnp.dot(vcmpt_ref[0, 0], pc.astype(bf), preferred_element_type=jnp.float32)
    psum = pc[:, :tq] + pc[:, tq:2 * tq] + pc[:, 2 * tq:3 * tq] + pc[:, 3 * tq:]
    p_hi = psum.astype(bf)
    p_lo = (psum - p_hi.astype(jnp.float32)).astype(bf)
    imp = (jnp.dot(ovt_ref[...], p_hi, preferred_element_type=jnp.float32)
           + jnp.dot(ovt_ref[...], p_lo, preferred_element_type=jnp.float32))
    nj = -(-n_sb // 8) * 8
    sel = _select_blocks_t(imp[:nj], tq_pos, n_sb)
    sel = jnp.concatenate([sel, jnp.zeros((128 - nj, tq), jnp.float32)], axis=0).astype(bf)

    per_kc = SEL_KC // ATT_TQ

    def sel_step(c, carry):
        m, l, acc = carry
        k0 = pl.multiple_of(c * SEL_KC, SEL_KC)
        kpos = k0 + lax.broadcasted_iota(jnp.int32, (SEL_KC, 1), 0)
        picked = jnp.dot(et_ref[pl.ds(k0, SEL_KC), :], sel, preferred_element_type=jnp.float32)
        mask = lanes4((picked > 0.5) & (kpos <= tq_pos))
        s = lax.dot_general(ks_ref[0, 0, pl.ds(k0, SEL_KC), :], q4, _NT, preferred_element_type=jnp.float32)
        s = jnp.where(mask, s, NEG_INF)
        m_new = jnp.maximum(m, jnp.max(s, axis=0, keepdims=True))
        alpha = jnp.exp(m - m_new)
        p = jnp.exp(s - m_new)
        l = alpha * l + jnp.sum(p, axis=0, keepdims=True)
        vt = jnp.concatenate([vst_ref[0, 0, c * per_kc + r] for r in range(per_kc)], axis=1)
        acc = alpha * acc + jnp.dot(vt, p.astype(bf), preferred_element_type=jnp.float32)
        return m_new, l, acc

    nq = GROUP * tq
    init = (jnp.full((1, nq), NEG_INF, jnp.float32), jnp.zeros((1, nq), jnp.float32),
            jnp.zeros((HEAD_DIM, nq), jnp.float32))
    n_kc = (t0 + tq + SEL_KC - 1) // SEL_KC
    _, l_s, acc_s = lax.fori_loop(0, n_kc, sel_step, init)
    o_sel = acc_s / jnp.maximum(l_s, 1e-30)

    n_wc = WINDOW // tq + 1
    c0 = jnp.maximum(i - WINDOW // tq, 0)
    w0 = pl.multiple_of(c0 * tq, tq)
    d = tpos - (w0 + lax.broadcasted_iota(jnp.int32, (n_wc * tq, 1), 0))
    sw = lax.dot_general(kw_ref[0, 0, pl.ds(w0, n_wc * tq), :], q4, _NT, preferred_element_type=jnp.float32)
    sw = jnp.where((d >= 0) & (d < WINDOW), sw, NEG_INF)
    ew = jnp.exp(sw - jnp.max(sw, axis=0, keepdims=True))
    vwt = jnp.concatenate([vwt_ref[0, 0, c0 + r] for r in range(n_wc)], axis=1)
    o_win = (jnp.dot(vwt, ew.astype(bf), preferred_element_type=jnp.float32)
             / jnp.maximum(jnp.sum(ew, axis=0, keepdims=True), 1e-30))

    gate = jax.nn.sigmoid(glt_ref[0, 0])
    for g in range(GROUP):
        sl = slice(g * tq, (g + 1) * tq)
        ot = (gate[3 * g:3 * g + 1] * o_cmp[:, sl] + gate[3 * g + 1:3 * g + 2] * o_sel[:, sl]
              + gate[3 * g + 2:3 * g + 3] * o_win[:, sl])
        o_ref[0, :, g * HEAD_DIM:(g + 1) * HEAD_DIM] = ot.T.astype(o_ref.dtype)


def _overlap_matrix(n_cmp, n_sb):
    start = np.arange(128) * CMP_STRIDE
    end = start + CMP_BLOCK - 1
    sb = np.arange(128) * SEL_BLOCK
    ov = (start[:, None] < sb[None, :] + SEL_BLOCK) & (end[:, None] >= sb[None, :])
    ov &= (np.arange(128)[:, None] < n_cmp) & (np.arange(128)[None, :] < n_sb)
    return jnp.asarray(ov, jnp.bfloat16)


def _heads_major(x):
    B, T, W = x.shape
    return x.reshape(B, T, N_KV_HEADS, W // N_KV_HEADS).transpose(0, 2, 1, 3)


def _nsa_prompt_pallas(q, kcmp, vcmp, ks, vs, kw, vw, gl):
    B, T, _ = q.shape
    tq = ATT_TQ
    n_cmp = (T - CMP_BLOCK) // CMP_STRIDE + 1
    n_sb = -(-T // SEL_BLOCK)
    assert T % SEL_KC == 0 and T >= WINDOW + tq and kcmp.shape[1] <= 128 and n_sb <= 128
    kcmp, vcmp = (jnp.pad(a, ((0, 0), (0, 128 - a.shape[1]), (0, 0))) for a in (kcmp, vcmp))
    k_rows = lambda x: _heads_major(x.astype(jnp.bfloat16))
    v_cols = lambda x: x.astype(jnp.bfloat16).reshape(B, T // tq, tq, N_KV_HEADS, HEAD_DIM).transpose(0, 3, 1, 4, 2)
    vcmp_t = vcmp.reshape(B, 128, N_KV_HEADS, HEAD_DIM).transpose(0, 2, 3, 1)
    gl_t = gl.reshape(B, T, N_KV_HEADS, 3 * GROUP).transpose(0, 2, 3, 1)
    et = jnp.asarray((np.arange(T)[:, None] // SEL_BLOCK) == np.arange(128)[None, :], jnp.bfloat16)
    k_spec = pl.BlockSpec((1, 1, T, HEAD_DIM), lambda b, k, i: (b, k, 0, 0))
    v_spec = pl.BlockSpec((1, 1, T // tq, HEAD_DIM, tq), lambda b, k, i: (b, k, 0, 0, 0))
    return pl.pallas_call(
        functools.partial(_nsa_prompt_kernel, tq=tq, n_cmp=n_cmp, n_sb=n_sb),
        out_shape=jax.ShapeDtypeStruct((B, T, Q_W), jnp.bfloat16),
        grid=(B, N_KV_HEADS, T // tq),
        in_specs=[pl.BlockSpec((1, tq, GROUP * HEAD_DIM), lambda b, k, i: (b, i, k)),
                  pl.BlockSpec((1, 1, 128, HEAD_DIM), lambda b, k, i: (b, k, 0, 0)),
                  pl.BlockSpec((1, 1, HEAD_DIM, 128), lambda b, k, i: (b, k, 0, 0)),
                  k_spec, v_spec, k_spec, v_spec,
                  pl.BlockSpec((1, 1, 3 * GROUP, tq), lambda b, k, i: (b, k, 0, i)),
                  pl.BlockSpec((128, 128), lambda b, k, i: (0, 0)),
                  pl.BlockSpec((T, 128), lambda b, k, i: (0, 0))],
        out_specs=pl.BlockSpec((1, tq, GROUP * HEAD_DIM), lambda b, k, i: (b, i, k)),
        compiler_params=pltpu.CompilerParams(
            dimension_semantics=("parallel", "parallel", "arbitrary"), vmem_limit_bytes=VMEM_LIMIT_BYTES),
        name="nsa_prompt",
    )(q, _heads_major(kcmp), vcmp_t, k_rows(ks), v_cols(vs), k_rows(kw), v_cols(vw), gl_t,
      _overlap_matrix(n_cmp, n_sb).T, et)


_NT = (((1,), (1,)), ((), ()))


def _decode_attend(s, mask, s_new, mask_new, pv_fn, v_new):
    sm = jnp.where(mask, s, NEG_INF)
    sn = jnp.where(mask_new, s_new, NEG_INF)
    m = jnp.maximum(jnp.max(sm, axis=1, keepdims=True), sn)
    e = jnp.where(mask, jnp.exp(sm - m), 0.0)
    en = jnp.where(mask_new, jnp.exp(sn - m), 0.0)
    l = jnp.sum(e, axis=1, keepdims=True) + en
    acc = pv_fn(e.astype(jnp.bfloat16)) + (en.astype(jnp.bfloat16).astype(jnp.float32)
                                            * v_new.astype(jnp.bfloat16).astype(jnp.float32))
    return acc / jnp.maximum(l, 1e-30)


def _nsa_sample_kernel(pt_ref, *refs, n_pages, n_sb):
    del pt_ref
    cmp_pages = refs[:n_pages]
    sel_pages = refs[n_pages:2 * n_pages]
    (win_ref, q_ref, gl_ref, ksn_ref, vsn_ref, kwn_ref, vwn_ref, kwc_ref, vwc_ref,
     wk_ref, wv_ref, pek_ref, pev_ref, w1k_ref, w1v_ref, w2k_ref, w2v_ref, cos_ref, sin_ref, ov_ref, ex_ref,
     o_ref, wout_ref, xk_s, xv_s) = refs[2 * n_pages:]
    past = n_pages * PAGE_SIZE
    qpos = past
    n_half = past // CMP_STRIDE
    n_cmp = (past + 1 - CMP_BLOCK) // CMP_STRIDE + 1
    wb = win_ref.shape[-1]
    bf = jnp.bfloat16

    for p in range(n_pages):
        xk_s[p * PAGE_SIZE:(p + 1) * PAGE_SIZE, :] = cmp_pages[p][0, 0, 0].T
        xv_s[p * PAGE_SIZE:(p + 1) * PAGE_SIZE, :] = cmp_pages[p][0, 0, 1].T
    row_c = lax.broadcasted_iota(jnp.int32, (n_half, KV_W), 0)
    kcmp = _compress_rows(lambda j: xk_s[pl.ds(j, n_half, stride=CMP_STRIDE), :],
                          wk_ref, pek_ref, w1k_ref, w2k_ref, n_half)
    kcmp = jnp.where(row_c < n_cmp, _rope_lanes(kcmp, cos_ref[...], sin_ref[...]), 0.0).astype(bf)
    vcmp = _compress_rows(lambda j: xv_s[pl.ds(j, n_half, stride=CMP_STRIDE), :],
                          wv_ref, pev_ref, w1v_ref, w2v_ref, n_half)
    vcmp = jnp.where(row_c < n_cmp, vcmp, 0.0).astype(bf)

    row8 = lax.broadcasted_iota(jnp.int32, (8, KV_W), 0)
    lane8 = lax.broadcasted_iota(jnp.int32, (8, KV_W), 1)
    top1 = lax.broadcasted_iota(jnp.int32, (8, 1), 0) < GROUP
    q8 = q_ref[0].astype(jnp.float32)
    q2 = jnp.where((row8 < GROUP) == (lane8 < HEAD_DIM), jnp.concatenate([q8, q8], axis=1), 0.0).astype(bf)
    q2f = q2.astype(jnp.float32)

    def halves(x):
        return jnp.where(top1, x[:, :HEAD_DIM], x[:, HEAD_DIM:])

    def new_score(k_new):
        return jnp.sum(q2f * k_new.astype(bf).astype(jnp.float32), axis=1, keepdims=True)

    sc = lax.dot_general(q2, kcmp, _NT, preferred_element_type=jnp.float32)
    nl = lax.broadcasted_iota(jnp.int32, (8, n_half), 1)
    pc = _softmax_rows(sc, ((nl * CMP_STRIDE + (CMP_BLOCK - 1)) <= qpos) & (nl < n_cmp))
    o_cmp = halves(jnp.dot(pc.astype(bf), vcmp, preferred_element_type=jnp.float32))
    pk0 = jnp.sum(jnp.where(top1, pc, 0.0), axis=0, keepdims=True)
    pk1 = jnp.sum(jnp.where(top1, 0.0, pc), axis=0, keepdims=True)
    rown = lax.broadcasted_iota(jnp.int32, (8, n_half), 0)
    p2 = jnp.where(rown == 0, pk0, jnp.where(rown == 1, pk1, 0.0))
    p_hi = p2.astype(bf)
    p_lo = (p2 - p_hi.astype(jnp.float32)).astype(bf)
    imp = (jnp.dot(p_hi, ov_ref[...], preferred_element_type=jnp.float32)
           + jnp.dot(p_lo, ov_ref[...], preferred_element_type=jnp.float32))
    sel2 = _select_blocks(imp, jnp.full((8, 1), qpos, jnp.int32), n_sb)

    picked2 = jnp.dot(sel2.astype(bf), ex_ref[...], preferred_element_type=jnp.float32)
    mask_s = jnp.where(top1, picked2[0:1], picked2[1:2]) > 0.5
    seln = jnp.sum(jnp.where(lane8 == qpos // SEL_BLOCK, sel2, 0.0), axis=1, keepdims=True)
    mask_new = jnp.where(top1, seln[0:1], seln[1:2]) > 0.5
    s_s = jnp.concatenate([jnp.dot(q2, sel_pages[p][0, 0, 0].astype(bf), preferred_element_type=jnp.float32)
                           for p in range(n_pages)], axis=1)

    def pv_sel(e):
        acc = jnp.zeros((8, KV_W), jnp.float32)
        for p in range(n_pages):
            acc = acc + lax.dot_general(e[:, p * PAGE_SIZE:(p + 1) * PAGE_SIZE], sel_pages[p][0, 0, 1].astype(bf),
                                        _NT, preferred_element_type=jnp.float32)
        return acc

    o_sel = halves(_decode_attend(s_s, mask_s, new_score(ksn_ref[0]), mask_new, pv_sel, vsn_ref[0]))

    s_w = jnp.dot(q2, win_ref[0, 0, 0].astype(bf), preferred_element_type=jnp.float32)
    kpos = past - wb + lax.broadcasted_iota(jnp.int32, (8, wb), 1)
    mask_w = (qpos - kpos >= 0) & (qpos - kpos < WINDOW) & (kpos >= 0)
    pv_win = lambda e: lax.dot_general(e, win_ref[0, 0, 1].astype(bf), _NT, preferred_element_type=jnp.float32)
    o_win = halves(_decode_attend(s_w, mask_w, new_score(kwn_ref[0]), jnp.full((8, 1), True), pv_win, vwn_ref[0]))

    gate = jax.nn.sigmoid(gl_ref[0])
    o_ref[0] = (gate[:, 0:1] * o_cmp + gate[:, 1:2] * o_sel + gate[:, 2:3] * o_win).astype(o_ref.dtype)

    lane_w = lax.broadcasted_iota(jnp.int32, (KV_W, wb), 1)
    wout_ref[0, 0] = jnp.where(lane_w == wb - 1, kwc_ref[0], pltpu.roll(win_ref[0, 0, 0], wb - 1, 1))
    wout_ref[0, 1] = jnp.where(lane_w == wb - 1, vwc_ref[0], pltpu.roll(win_ref[0, 0, 1], wb - 1, 1))


def _cache_rows_on_lanes(c):
    nd = c.ndim
    c = jnp.moveaxis(c, nd - 4, nd - 1)
    return c.reshape(c.shape[:-3] + (c.shape[-3] * c.shape[-2], c.shape[-1]))


def _nsa_sample_pallas(layer, q, gl, ks, vs, kw, vw, cmpw, cmp_t, sel_t, win_t, page_table):
    B = q.shape[0]
    n_pages = page_table.shape[1]
    past = n_pages * PAGE_SIZE
    wb = win_t.shape[-1]
    n_half = past // CMP_STRIDE
    n_cmp = (past + 1 - CMP_BLOCK) // CMP_STRIDE + 1
    n_sb = -(-(past + 1) // SEL_BLOCK)
    wk1, pek, wk2, wv1, pev, wv2 = cmpw
    wk, pekf, w1k, w2k = _compress_weights(wk1, pek, wk2)
    wv, pevf, w1v, w2v = _compress_weights(wv1, pev, wv2)
    cos, sin = _rope_tables(jnp.arange(n_half) * CMP_STRIDE + CMP_BLOCK - 1, N_KV_HEADS)
    ov = _overlap_matrix(n_cmp, n_sb)[:n_half]
    ex = jnp.asarray((np.arange(past)[None, :] // SEL_BLOCK) == np.arange(128)[:, None], jnp.bfloat16)
    consts = (wk, wv, pekf, pevf, w1k, w1v, w2k, w2v, cos, sin, ov, ex)
    row3 = lambda x: x.reshape(B, 1, KV_W)
    col3 = lambda x: x.reshape(B, KV_W, 1)
    per_b = (q.reshape(B, N_HEADS, HEAD_DIM), gl.reshape(B, N_HEADS, 3), row3(ks), row3(vs), row3(kw), row3(vw),
             col3(kw), col3(vw))
    page_spec = lambda p: pl.BlockSpec((1, 1, 2, KV_W, PAGE_SIZE), lambda b, pt: (layer, pt[b, p], 0, 0, 0))
    b_spec = lambda a: pl.BlockSpec((1,) + a.shape[1:], lambda b, pt: (b,) + (0,) * (a.ndim - 1))
    full = lambda a: pl.BlockSpec(a.shape, lambda b, pt: (0,) * a.ndim)
    in_specs = ([page_spec(p) for p in range(n_pages)] * 2
                + [pl.BlockSpec((1, 1, 2, KV_W, wb), lambda b, pt: (layer, b, 0, 0, 0))]
                + [b_spec(a) for a in per_b] + [full(a) for a in consts])
    return pl.pallas_call(
        functools.partial(_nsa_sample_kernel, n_pages=n_pages, n_sb=n_sb),
        out_shape=(jax.ShapeDtypeStruct((B, N_HEADS, HEAD_DIM), jnp.bfloat16),
                   jax.ShapeDtypeStruct((B, 2, KV_W, wb), jnp.float32)),
        grid_spec=pltpu.PrefetchScalarGridSpec(
            num_scalar_prefetch=1, grid=(B,), in_specs=in_specs,
            out_specs=(pl.BlockSpec((1, N_HEADS, HEAD_DIM), lambda b, pt: (b, 0, 0)),
                       pl.BlockSpec((1, 2, KV_W, wb), lambda b, pt: (b, 0, 0, 0))),
            scratch_shapes=[pltpu.VMEM((past, KV_W), jnp.float32), pltpu.VMEM((past, KV_W), jnp.float32)]),
        compiler_params=pltpu.CompilerParams(dimension_semantics=("arbitrary",), vmem_limit_bytes=VMEM_LIMIT_BYTES),
        name="nsa_sample",
    )(page_table, *([cmp_t] * n_pages), *([sel_t] * n_pages), win_t, *per_b, *consts)


SSM_N = SSM_GROUPS * SSM_STATE
SSM_LANE_BLK = 512
SSM_TL = 64


def _ssm_kernel(u_ref, h0r_ref, h0i_ref, ar_ref, ai_ref, bm_ref, cr_ref, ci_ref, d_ref,
                y_ref, hr_ref, hi_ref, xr_s, xi_s, *, tl, nb):
    c = pl.program_id(0)

    @pl.when(c == 0)
    def _():
        hr_ref[...] = h0r_ref[...]
        hi_ref[...] = h0i_ref[...]

    u = u_ref[...]
    ub = u.astype(jnp.bfloat16)
    n_grp = SSM_WIDTH // 128
    for j in range(n_grp):
        bu = jnp.dot(ub[:, 128 * j:128 * (j + 1)], bm_ref[j], preferred_element_type=jnp.float32)
        xr_s[:, 512 * j:512 * (j + 1)] = bu[:, :512]
        xi_s[:, 512 * j:512 * (j + 1)] = bu[:, 512:]

    for lb in range(SSM_N // SSM_LANE_BLK):
        sl = slice(lb * SSM_LANE_BLK, (lb + 1) * SSM_LANE_BLK)
        ar = jnp.broadcast_to(ar_ref[:, sl], (8, SSM_LANE_BLK))
        ai = jnp.broadcast_to(ai_ref[:, sl], (8, SSM_LANE_BLK))
        for r in range(nb // 8):
            def step(t, carry):
                hr, hi = carry
                row = pl.multiple_of(t * nb + r * 8, 8)
                xr = xr_s[pl.ds(row, 8), sl]
                xi = xi_s[pl.ds(row, 8), sl]
                nr = ar * hr - ai * hi + xr
                ni = ar * hi + ai * hr + xi
                xr_s[pl.ds(row, 8), sl] = nr
                xi_s[pl.ds(row, 8), sl] = ni
                return nr, ni

            hr, hi = lax.fori_loop(0, tl, step, (hr_ref[r * 8:(r + 1) * 8, sl], hi_ref[r * 8:(r + 1) * 8, sl]))
            hr_ref[r * 8:(r + 1) * 8, sl] = hr
            hi_ref[r * 8:(r + 1) * 8, sl] = hi

    for j in range(n_grp):
        yr = jnp.dot(xr_s[:, 512 * j:512 * (j + 1)].astype(jnp.bfloat16), cr_ref[j], preferred_element_type=jnp.float32)
        yi = jnp.dot(xi_s[:, 512 * j:512 * (j + 1)].astype(jnp.bfloat16), ci_ref[j], preferred_element_type=jnp.float32)
        y_ref[:, 128 * j:128 * (j + 1)] = yr - yi + d_ref[:, 128 * j:128 * (j + 1)] * u[:, 128 * j:128 * (j + 1)]


def _ssm_params(a_re, a_im, log_dt, b_re, b_im, c_re, c_im, d_skip):
    dt = jnp.exp(log_dt)[:, None]
    mag = jnp.exp(dt * a_re)
    ab_re = mag * jnp.cos(dt * a_im)
    ab_im = mag * jnp.sin(dt * a_im)
    den = a_re * a_re + a_im * a_im
    zr = ((ab_re - 1.0) * a_re + ab_im * a_im) / den
    zi = (ab_im * a_re - (ab_re - 1.0) * a_im) / den
    bb_re = zr[..., None] * b_re - zi[..., None] * b_im
    bb_im = zr[..., None] * b_im + zi[..., None] * b_re
    n_grp = SSM_WIDTH // 128
    gpl = 128 // SSM_GROUP
    eye = jnp.eye(gpl, dtype=jnp.float32)

    def b_blocks(bb):
        x = bb.reshape(n_grp, gpl, SSM_STATE, SSM_GROUP)
        return jnp.einsum('jgpc,gh->jgchp', x, eye).reshape(n_grp, 128, gpl * SSM_STATE)

    def c_blocks(cc):
        x = cc.reshape(n_grp, gpl, SSM_GROUP, SSM_STATE)
        return jnp.einsum('jgcp,gh->jgphc', x, eye).reshape(n_grp, gpl * SSM_STATE, 128)

    bm = jnp.concatenate([b_blocks(bb_re), b_blocks(bb_im)], axis=-1).astype(jnp.bfloat16)
    return (ab_re.reshape(1, SSM_N), ab_im.reshape(1, SSM_N), bm,
            c_blocks(c_re).astype(jnp.bfloat16), c_blocks(c_im).astype(jnp.bfloat16), d_skip.reshape(1, SSM_WIDTH))


def _ssm_pallas(u_tb, B, T, h0_re, h0_im, params):
    ab_re, ab_im, bm, cr, ci, d = params
    tl = _pick_tile(T, (SSM_TL,))
    full = lambda a: pl.BlockSpec(a.shape, lambda c: (0,) * a.ndim)
    h0r = h0_re.reshape(B, SSM_N)
    h0i = h0_im.reshape(B, SSM_N)
    consts = (h0r, h0i, ab_re, ab_im, bm, cr, ci, d)
    y, hr, hi = pl.pallas_call(
        functools.partial(_ssm_kernel, tl=tl, nb=B),
        out_shape=(jax.ShapeDtypeStruct((T * B, SSM_WIDTH), jnp.float32),
                   jax.ShapeDtypeStruct((B, SSM_N), jnp.float32), jax.ShapeDtypeStruct((B, SSM_N), jnp.float32)),
        grid=(T // tl,),
        in_specs=[pl.BlockSpec((tl * B, SSM_WIDTH), lambda c: (c, 0))] + [full(a) for a in consts],
        out_specs=(pl.BlockSpec((tl * B, SSM_WIDTH), lambda c: (c, 0)),
                   pl.BlockSpec((B, SSM_N), lambda c: (0, 0)), pl.BlockSpec((B, SSM_N), lambda c: (0, 0))),
        scratch_shapes=[pltpu.VMEM((tl * B, SSM_N), jnp.float32), pltpu.VMEM((tl * B, SSM_N), jnp.float32)],
        compiler_params=pltpu.CompilerParams(dimension_semantics=("arbitrary",), vmem_limit_bytes=VMEM_LIMIT_BYTES),
        name="ssm",
    )(u_tb, *consts)
    return y, hr.reshape(B, SSM_GROUPS, SSM_STATE), hi.reshape(B, SSM_GROUPS, SSM_STATE)


def _rmsnorm(x, g):
    xf = x.astype(jnp.float32)
    y = xf * lax.rsqrt(jnp.mean(xf * xf, axis=-1, keepdims=True) + EPS)
    return (y * g.astype(jnp.float32)).astype(x.dtype)


def _rope(x, pos):
    half = HEAD_DIM // 2
    inv = jnp.float32(ROPE_THETA) ** (-jnp.arange(half, dtype=jnp.float32) / half)
    ang = pos.astype(jnp.float32)[:, None] * inv[None, :]
    cos = jnp.cos(ang)[:, None, :]
    sin = jnp.sin(ang)[:, None, :]
    x1, x2 = x[..., :half], x[..., half:]
    return jnp.concatenate([x1 * cos - x2 * sin, x2 * cos + x1 * sin], axis=-1)


def _masked_softmax(s, mask):
    s = jnp.where(mask, s.astype(jnp.float32), NEG_INF)
    m = jnp.max(s, axis=-1, keepdims=True)
    e = jnp.where(mask, jnp.exp(s - m), 0.0)
    return e / jnp.maximum(jnp.sum(e, axis=-1, keepdims=True), 1e-30)


def _attend(q, k, v, mask):
    s = jnp.einsum('btkgd,bskd->bkgts', q, k) * SCALE
    p = _masked_softmax(s, mask)
    o = jnp.einsum('bkgts,bskd->btkgd', p.astype(v.dtype), v)
    return o, p


def _window_mask(q_pos, k_pos):
    d = q_pos[:, None] - k_pos[None, :]
    return (d >= 0) & (d < WINDOW) & (k_pos[None, :] >= 0)


def _compress(x, w1, pe, w2):
    B, L, K, D = x.shape
    xt = x.transpose(0, 2, 1, 3).reshape(B * K, L, D)
    hdn = lax.conv_general_dilated(xt, w1.astype(xt.dtype), (CMP_STRIDE,), 'VALID',
                                   dimension_numbers=('NWC', 'WIO', 'NWC'))
    hdn = hdn + jnp.einsum('ld,ldf->f', pe, w1)
    out = jax.nn.gelu(hdn) @ w2
    n = out.shape[1]
    return out.reshape(B, K, n, D).transpose(0, 2, 1, 3)


def _cmp_branch(q, kc_full, vc_full, q_pos, wk1, pek, wk2, wv1, pev, wv2):
    kcmp = _compress(kc_full, wk1, pek, wk2)
    vcmp = _compress(vc_full, wv1, pev, wv2)
    n_cmp = kcmp.shape[1]
    start = jnp.arange(n_cmp) * CMP_STRIDE
    end = start + CMP_BLOCK - 1
    kcmp = _rope(kcmp, end)
    o, p = _attend(q, kcmp, vcmp, end[None, :] <= q_pos[:, None])
    L = kc_full.shape[1]
    n_sb = -(-L // SEL_BLOCK)
    sb_start = jnp.arange(n_sb) * SEL_BLOCK
    overlap = ((start[:, None] < sb_start[None, :] + SEL_BLOCK) & (end[:, None] >= sb_start[None, :])).astype(jnp.float32)
    imp = jnp.einsum('bkgtn,nj->btkj', p, overlap)
    blk = jnp.arange(n_sb)
    forced = (blk[None, :] == 0) | (blk[None, :] == (q_pos // SEL_BLOCK)[:, None])
    causal = sb_start[None, :] <= q_pos[:, None]
    imp = jnp.where(forced[None, :, None, :], FORCE_SCORE, imp)
    imp = jnp.where(causal[None, :, None, :], imp, NEG_INF)
    top_v, top_i = lax.top_k(imp, min(N_SEL, n_sb))
    return o, top_i, top_v > 0.5 * NEG_INF


def _sel_attend(q, idx, valid, q_pos, fetch):
    pos = idx[..., None] * SEL_BLOCK + jnp.arange(SEL_BLOCK)
    k, v = fetch(pos)
    B, Tc, K, N, S = pos.shape
    mask = (valid[..., None] & (pos <= q_pos[None, :, None, None, None])).reshape(B, Tc, K, N * S)
    k = k.reshape(B, Tc, K, N * S, HEAD_DIM)
    v = v.reshape(B, Tc, K, N * S, HEAD_DIM)
    s = jnp.einsum('btkgd,btksd->btkgs', q, k) * SCALE
    p = _masked_softmax(s, mask[:, :, :, None, :])
    return jnp.einsum('btkgs,btksd->btkgd', p.astype(v.dtype), v)


def _to_chunks(x, c):
    B, T = x.shape[:2]
    return x.reshape((B, T // c, c) + x.shape[2:]).swapaxes(0, 1)


def _from_chunks(x):
    nc, B, c = x.shape[:3]
    return x.swapaxes(0, 1).reshape((B, nc * c) + x.shape[3:])


def _batch_head_index(B):
    bi = jnp.arange(B)[:, None, None, None, None]
    hd = jnp.arange(N_KV_HEADS)[None, None, :, None, None]
    return bi, hd


def _nsa_prompt(q, kc, vc, ks, vs, kw, vw, cmpw):
    B, T = q.shape[:2]
    pos = jnp.arange(T)
    o_cmp, idx, valid = _cmp_branch(q, kc, vc, pos, *cmpw)
    bi, hd = _batch_head_index(B)

    def fetch(p):
        pc = jnp.clip(p, 0, T - 1)
        return ks[bi, pc, hd], vs[bi, pc, hd]

    qb = Q_BLOCK if T % Q_BLOCK == 0 else T
    o_sel = _from_chunks(lax.map(lambda a: _sel_attend(a[0], a[1], a[2], a[3], fetch),
                                 (_to_chunks(q, qb), _to_chunks(idx, qb), _to_chunks(valid, qb), pos.reshape(-1, qb))))
    kp = jnp.pad(kw, ((0, 0), (WINDOW, 0), (0, 0), (0, 0)))
    vp = jnp.pad(vw, ((0, 0), (WINDOW, 0), (0, 0), (0, 0)))

    def win_block(a):
        c, qc = a
        start = c * qb
        kb = lax.dynamic_slice_in_dim(kp, start, WINDOW + qb, axis=1)
        vb = lax.dynamic_slice_in_dim(vp, start, WINDOW + qb, axis=1)
        qpos = start + jnp.arange(qb)
        kpos = start - WINDOW + jnp.arange(WINDOW + qb)
        return _attend(qc, kb, vb, _window_mask(qpos, kpos))[0]

    o_win = _from_chunks(lax.map(win_block, (jnp.arange(T // qb), _to_chunks(q, qb))))
    n_keep = min(WINDOW, T)
    rows = (jnp.stack([kc, vc], axis=2), jnp.stack([ks, vs], axis=2), jnp.stack([kw, vw], axis=2)[:, T - n_keep:])
    return o_cmp, o_sel, o_win, rows


def _nsa_sample(q, kc, vc, ks, vs, kw, vw, cmpw, cmp_pool, sel_pool, win_buf, page_table):
    B, T = q.shape[:2]
    past = page_table.shape[1] * PAGE_SIZE
    pos = past + jnp.arange(T)
    past_cmp = cmp_pool[page_table].reshape((B, past) + cmp_pool.shape[2:])
    kc_full = jnp.concatenate([past_cmp[:, :, 0], kc], axis=1)
    vc_full = jnp.concatenate([past_cmp[:, :, 1], vc], axis=1)
    o_cmp, idx, valid = _cmp_branch(q, kc_full, vc_full, pos, *cmpw)
    sel_rows = sel_pool.reshape((-1,) + sel_pool.shape[2:])
    bi, hd = _batch_head_index(B)

    def fetch(p):
        pc = jnp.clip(p, 0, past - 1)
        phys = page_table[bi, pc // PAGE_SIZE] * PAGE_SIZE + pc % PAGE_SIZE
        pn = jnp.clip(p - past, 0, T - 1)
        in_past = (p < past)[..., None]
        k = jnp.where(in_past, sel_rows[phys, 0, hd], ks[bi, pn, hd])
        v = jnp.where(in_past, sel_rows[phys, 1, hd], vs[bi, pn, hd])
        return k, v

    o_sel = _sel_attend(q, idx, valid, pos, fetch)
    wb = win_buf.shape[1]
    win_all = jnp.concatenate([win_buf, jnp.stack([kw, vw], axis=2)], axis=1)
    kpos = past - wb + jnp.arange(wb + T)
    o_win, _ = _attend(q, win_all[:, :, 0], win_all[:, :, 1], _window_mask(pos, kpos))
    rows = (jnp.stack([kc, vc], axis=2), jnp.stack([ks, vs], axis=2), win_all[:, T:])
    return o_cmp, o_sel, o_win, rows


def _cmul_combine(e1, e2):
    ar1, ai1, br1, bi1 = e1
    ar2, ai2, br2, bi2 = e2
    return (ar2 * ar1 - ai2 * ai1, ar2 * ai1 + ai2 * ar1,
            ar2 * br1 - ai2 * bi1 + br2, ar2 * bi1 + ai2 * br1 + bi2)


def _ssm_scan(u, h0_re, h0_im, a_re, a_im, log_dt, b_re, b_im, c_re, c_im, d_skip):
    B, T, _ = u.shape
    uf = u.reshape(B, T, SSM_GROUPS, SSM_GROUP)
    ar = a_re
    ai = a_im
    dt = jnp.exp(log_dt)[:, None]
    mag = jnp.exp(dt * ar)
    ab_re = mag * jnp.cos(dt * ai)
    ab_im = mag * jnp.sin(dt * ai)
    den = ar * ar + ai * ai
    zr = ((ab_re - 1.0) * ar + ab_im * ai) / den
    zi = (ab_im * ar - (ab_re - 1.0) * ai) / den
    bb_re = zr[..., None] * b_re - zi[..., None] * b_im
    bb_im = zr[..., None] * b_im + zi[..., None] * b_re
    bu_re = jnp.einsum('gpc,btgc->tbgp', bb_re, uf)
    bu_im = jnp.einsum('gpc,btgc->tbgp', bb_im, uf)
    chunk = SSM_CHUNK if T % SSM_CHUNK == 0 else T
    nc = T // chunk
    shp = (chunk, B, SSM_GROUPS, SSM_STATE)
    bu_re = bu_re.reshape((nc,) + shp)
    bu_im = bu_im.reshape((nc,) + shp)
    a_re_c = jnp.broadcast_to(ab_re, shp)
    a_im_c = jnp.broadcast_to(ab_im, shp)

    def step(carry, xs):
        hr0, hi0 = carry
        xr, xi = xs
        pr, pim, sr, si = lax.associative_scan(_cmul_combine, (a_re_c, a_im_c, xr, xi), axis=0)
        hr = sr + pr * hr0 - pim * hi0
        hi = si + pr * hi0 + pim * hr0
        y = jnp.einsum('gcp,tbgp->tbgc', c_re, hr) - jnp.einsum('gcp,tbgp->tbgc', c_im, hi)
        return (hr[-1], hi[-1]), y

    (hr, hi), y = lax.scan(step, (h0_re, h0_im), (bu_re, bu_im))
    y = y.reshape(T, B, SSM_GROUPS, SSM_GROUP).transpose(1, 0, 2, 3) + d_skip * uf
    return y.reshape(B, T, SSM_WIDTH), hr, hi


def _block(h, p_l, pos, lw, sample, h0_re, h0_im, conv_prefix):
    (g_attn, w_in, g_q, g_kc, g_ks, g_kw, wk1, pek, wk2, wv1, pev, wv2,
     a_re, a_im, log_dt, b_re, b_im, c_re, c_im, d_skip,
     w_a, w_glu1, w_glu2, w_o, g_ffn, w_up, conv_w, conv_b, w_down,
     g_ple, w_ple_gate, w_ple) = lw
    B, T = p_l.shape[:2]
    N = B * T
    qs, rows_cmp, rows_sel, rows_win, gl, u, gab = _in_proj_pallas(h, w_in, g_attn, g_q, g_kc, g_ks, g_kw, pos, T)
    gl = gl[:, :3 * N_HEADS]
    cmpw = (wk1, pek, wk2, wv1, pev, wv2)
    as_rows = lambda a: a.reshape(B, T, 2, N_KV_HEADS, HEAD_DIM)
    if sample is None:
        r3 = lambda a: a.reshape(B, T, a.shape[-1])
        kcmp, vcmp = _compress_pallas(r3(rows_cmp), cmpw)
        o = _nsa_prompt_pallas(r3(qs), kcmp, vcmp, r3(rows_sel[:, :KV_W]), r3(rows_sel[:, KV_W:]),
                               r3(rows_win[:, :KV_W]), r3(rows_win[:, KV_W:]), r3(gl)).reshape(N, Q_W)
        n_keep = min(WINDOW, T)
        rows = (as_rows(rows_cmp), as_rows(rows_sel), as_rows(rows_win)[:, T - n_keep:])
    else:
        assert T == 1
        layer, cmp_t, sel_t, win_t, page_table = sample
        o, wnew = _nsa_sample_pallas(layer, qs, gl, rows_sel[:, :KV_W], rows_sel[:, KV_W:],
                                     rows_win[:, :KV_W], rows_win[:, KV_W:], cmpw, cmp_t, sel_t, win_t, page_table)
        o = o.reshape(N, Q_W)
        wb = wnew.shape[-1]
        wnew = wnew.reshape(B, 2, N_KV_HEADS, HEAD_DIM, wb).transpose(0, 4, 1, 2, 3)
        rows = (as_rows(rows_cmp), as_rows(rows_sel), wnew)
    y, hr, hi = _ssm_pallas(u.reshape(T * B, SSM_WIDTH), B, T, h0_re, h0_im,
                            _ssm_params(a_re, a_im, log_dt, b_re, b_im, c_re, c_im, d_skip))
    h, conv_rows = _mix_ffn_pallas(h, o, y.reshape(u.shape), gab, p_l.reshape(N, -1), conv_prefix,
                                   (w_a, w_glu1, w_glu2, w_o, g_ffn, w_up, conv_w, conv_b, w_down,
                                    g_ple, w_ple_gate, w_ple), T)
    return h, rows, hr, hi, conv_rows


def kernel(x_prompt, x_sample, cache_cmp, cache_sel, cache_win, state_ssm_re, state_ssm_im, state_conv, page_table, p_prompt, p_sample, g_attn, w_in, g_q, g_kc, g_ks, g_kw, cmp_wk1, cmp_pek, cmp_wk2, cmp_wv1, cmp_pev, cmp_wv2, ssm_a_re, ssm_a_im, ssm_log_dt, ssm_b_re, ssm_b_im, ssm_c_re, ssm_c_im, ssm_d, w_a, w_glu1, w_glu2, w_o, g_ffn, w_up, conv_w, conv_b, w_down, g_ple, w_ple_gate, w_ple):
    Bp, Tp = x_prompt.shape[:2]
    Ts = x_sample.shape[1]
    depth = w_in.shape[0]
    past = page_table.shape[1] * PAGE_SIZE
    pos_p = jnp.arange(Tp)
    pos_s = past + jnp.arange(Ts)
    zeros_h = jnp.zeros((Bp, SSM_GROUPS, SSM_STATE), x_prompt.dtype)
    layer_w = (g_attn, _pad_w_in(w_in), g_q, g_kc, g_ks, g_kw, cmp_wk1, cmp_pek, cmp_wk2, cmp_wv1, cmp_pev, cmp_wv2,
               ssm_a_re, ssm_a_im, ssm_log_dt, ssm_b_re, ssm_b_im, ssm_c_re, ssm_c_im, ssm_d,
               w_a, w_glu1, w_glu2, w_o, g_ffn, w_up, conv_w, conv_b, w_down, g_ple, w_ple_gate, w_ple)
    cmp_t, sel_t, win_t = (_cache_rows_on_lanes(c) for c in (cache_cmp, cache_sel, cache_win))
    st = [[] for _ in range(12)]
    hp, hs = x_prompt.reshape(Bp * Tp, D_MODEL), x_sample.reshape(-1, D_MODEL)
    for i in range(depth):
        lw = [w[i] for w in layer_w]
        hp, rows, hr, hi, cv = _block(hp, p_prompt[i], pos_p, lw, None, zeros_h, zeros_h, None)
        for j, a in enumerate(list(rows) + [hr, hi, cv]):
            st[j].append(a)
        hs, rows, hr, hi, cv = _block(hs, p_sample[i], pos_s, lw, (i, cmp_t, sel_t, win_t, page_table),
                                      state_ssm_re[i], state_ssm_im[i], state_conv[i])
        for j, a in enumerate(list(rows) + [hr, hi, cv]):
            st[6 + j].append(a)
    return (hp.reshape(x_prompt.shape), hs.reshape(x_sample.shape)) + tuple(jnp.stack(s) for s in st)
```

```python
import functools
import math

import numpy as np
import jax
import jax.numpy as jnp
from jax import lax
from jax.experimental import pallas as pl
from jax.experimental.pallas import tpu as pltpu

D_MODEL = 1024
N_HEADS = 8
N_KV_HEADS = 2
HEAD_DIM = 64
GROUP = N_HEADS // N_KV_HEADS
Q_W = N_HEADS * HEAD_DIM
KV_W = N_KV_HEADS * HEAD_DIM
CMP_BLOCK = 32
CMP_STRIDE = 16
SEL_BLOCK = 64
N_SEL = 8
WINDOW = 512
Q_BLOCK = 128
PAGE_SIZE = 128
ROPE_THETA = 10000.0
SSM_WIDTH = D_MODEL // 2
SSM_GROUP = 16
SSM_GROUPS = SSM_WIDTH // SSM_GROUP
SSM_STATE = 64
SSM_CHUNK = 128
D_FF = 11 * D_MODEL // 4
CONV_W = 3
EPS = 1e-6
NEG_INF = -1e30
FORCE_SCORE = 1e9
SCALE = HEAD_DIM ** -0.5
SPLIT_SIZES = (Q_W, KV_W, KV_W, KV_W, KV_W, KV_W, KV_W, 3 * N_HEADS, SSM_WIDTH, D_MODEL, D_MODEL)

VMEM_LIMIT_BYTES = 56 * 1024 * 1024


def _pick_tile(n, cands):
    for c in cands:
        if n % c == 0:
            return c
    return n


def _rms(x, g):
    return x * lax.rsqrt(jnp.mean(x * x, axis=-1, keepdims=True) + EPS) * g


def _bdot(a, b):
    return jnp.dot(a.astype(jnp.bfloat16), b, preferred_element_type=jnp.float32)


GL_PAD = 128
_IN_WIDTHS = (Q_W, 6 * KV_W, GL_PAD, SSM_WIDTH, 2 * D_MODEL)
_IN_OFFS = tuple(int(v) for v in np.cumsum((0,) + _IN_WIDTHS))
ROW_TILE = 256
IN_TILE = 512


def _in_proj_kernel(h_ref, g_ref, w_ref, gq_ref, gk_ref, ones_ref, cos_ref, sin_ref,
                    q_ref, cmp_ref, sel_ref, win_ref, gl_ref, u_ref, gab_ref, *attn_refs):
    xn = _rms(h_ref[...], g_ref[...]).astype(jnp.bfloat16)
    seg = lambda s: jnp.dot(xn, w_ref[:, _IN_OFFS[s]:_IN_OFFS[s + 1]], preferred_element_type=jnp.float32)
    cos, sin = cos_ref[...], sin_ref[...]
    ones = ones_ref[...]

    def head_norm(x, gain):
        x2 = x * x
        hi = x2.astype(jnp.bfloat16)
        lo = (x2 - hi.astype(jnp.float32)).astype(jnp.bfloat16)
        ss = (jnp.dot(hi, ones, preferred_element_type=jnp.float32)
              + jnp.dot(lo, ones, preferred_element_type=jnp.float32))
        return x * lax.rsqrt(ss * (1.0 / HEAD_DIM) + EPS) * gain

    zq = seg(0)
    for c in range(Q_W // KV_W):
        qn = _rope_lanes(head_norm(zq[:, c * KV_W:(c + 1) * KV_W], gq_ref[...]), cos, sin)
        q_ref[:, c * KV_W:(c + 1) * KV_W] = (qn * SCALE).astype(q_ref.dtype)
    zkv = seg(1)
    part = lambda c: zkv[:, c * KV_W:(c + 1) * KV_W]
    cmp_ref[:, :KV_W] = head_norm(part(0), gk_ref[0:1])
    cmp_ref[:, KV_W:] = part(1)
    sel_ref[:, :KV_W] = _rope_lanes(head_norm(part(2), gk_ref[1:2]), cos, sin)
    sel_ref[:, KV_W:] = part(3)
    win_ref[:, :KV_W] = _rope_lanes(head_norm(part(4), gk_ref[2:3]), cos, sin)
    win_ref[:, KV_W:] = part(5)
    gl = seg(2)
    gl_ref[...] = gl
    u_ref[...] = seg(3)
    gab_ref[...] = seg(4)
    if attn_refs:
        ksb_ref, vst_ref, kwb_ref, vwt_ref, glt_ref = attn_refs
        tq = vst_ref.shape[-1]
        for k_ref, v_ref, src in ((ksb_ref, vst_ref, sel_ref), (kwb_ref, vwt_ref, win_ref)):
            kb = src[:, :KV_W].astype(jnp.bfloat16)
            for hd in range(N_KV_HEADS):
                k_ref[0, hd] = kb[:, hd * HEAD_DIM:(hd + 1) * HEAD_DIM]
            for r in range(src.shape[0] // tq):
                vt = src[r * tq:(r + 1) * tq, KV_W:].T.astype(jnp.bfloat16)
                for hd in range(N_KV_HEADS):
                    v_ref[0, hd, r] = vt[hd * HEAD_DIM:(hd + 1) * HEAD_DIM]
        glt = gl.T
        for hd in range(N_KV_HEADS):
            glt_ref[0, hd] = glt[hd * 3 * GROUP:(hd + 1) * 3 * GROUP]


def _pad_w_in(w_in):
    a = Q_W + 6 * KV_W + 3 * N_HEADS
    pad = jnp.zeros(w_in.shape[:-1] + (GL_PAD - 3 * N_HEADS,), w_in.dtype)
    return jnp.concatenate([w_in[..., :a], pad, w_in[..., a:]], axis=-1).astype(jnp.bfloat16)


def _in_proj_pallas(h2d, w_in_p, g_attn, g_q, g_kc, g_ks, g_kw, pos, seq_len):
    N = h2d.shape[0]
    T = seq_len
    B = N // T
    tm = _pick_tile(N, (IN_TILE, ROW_TILE, 128))
    nt = max(T // tm, 1)
    assert T == 1 or T % tm == 0
    cos, sin = _rope_tables(pos, N_KV_HEADS)
    if T == 1:
        cos, sin = (jnp.broadcast_to(t, (tm, KV_W)) for t in (cos, sin))
    tile2 = lambda g: jnp.tile(g.reshape(1, HEAD_DIM), (1, N_KV_HEADS))
    gk = jnp.concatenate([tile2(g_kc), tile2(g_ks), tile2(g_kw), jnp.zeros((5, KV_W), jnp.float32)], axis=0)
    ones = jnp.asarray(np.kron(np.eye(N_KV_HEADS), np.ones((HEAD_DIM, HEAD_DIM))), jnp.bfloat16)
    row = lambda w: pl.BlockSpec((tm, w), lambda i: (i, 0))
    const = lambda a: pl.BlockSpec(a.shape, lambda i: (0,) * a.ndim, pipeline_mode=pl.Buffered(1))
    tab = pl.BlockSpec((tm, KV_W), lambda i: (i % nt, 0))
    if T == 1:
        u_shape, u_spec = (N, SSM_WIDTH), row(SSM_WIDTH)
    else:
        u_shape, u_spec = (T, B * SSM_WIDTH), pl.BlockSpec((tm, SSM_WIDTH), lambda i: (i % nt, i // nt))
    consts = (g_attn.reshape(1, D_MODEL), w_in_p, tile2(g_q), gk, ones)
    f32, bf = jnp.float32, jnp.bfloat16
    out_shape = ((jax.ShapeDtypeStruct((N, Q_W), bf),) + (jax.ShapeDtypeStruct((N, 2 * KV_W), f32),) * 3
                 + (jax.ShapeDtypeStruct((N, GL_PAD), f32), jax.ShapeDtypeStruct(u_shape, f32),
                    jax.ShapeDtypeStruct((N, 2 * D_MODEL), f32)))
    out_specs = (row(Q_W), row(2 * KV_W), row(2 * KV_W), row(2 * KV_W), row(GL_PAD), u_spec, row(2 * D_MODEL))
    if T > 1:
        tq = ATT_TQ
        assert tm % tq == 0
        k_shape = jax.ShapeDtypeStruct((B, N_KV_HEADS, T, HEAD_DIM), bf)
        k_spec = pl.BlockSpec((1, N_KV_HEADS, tm, HEAD_DIM), lambda i: (i // nt, 0, i % nt, 0))
        v_shape = jax.ShapeDtypeStruct((B, N_KV_HEADS, T // tq, HEAD_DIM, tq), bf)
        v_spec = pl.BlockSpec((1, N_KV_HEADS, tm // tq, HEAD_DIM, tq), lambda i: (i // nt, 0, i % nt, 0, 0))
        g_shape = jax.ShapeDtypeStruct((B, N_KV_HEADS, 3 * GROUP, T), f32)
        g_spec = pl.BlockSpec((1, N_KV_HEADS, 3 * GROUP, tm), lambda i: (i // nt, 0, 0, i % nt))
        out_shape += (k_shape, v_shape, k_shape, v_shape, g_shape)
        out_specs += (k_spec, v_spec, k_spec, v_spec, g_spec)
    return pl.pallas_call(
        _in_proj_kernel,
        out_shape=out_shape,
        grid=(N // tm,),
        in_specs=[row(D_MODEL)] + [const(a) for a in consts] + [tab, tab],
        out_specs=out_specs,
        compiler_params=pltpu.CompilerParams(dimension_semantics=("parallel",), vmem_limit_bytes=VMEM_LIMIT_BYTES),
        name="in_proj",
    )(h2d, *consts, cos, sin)


FF_CHUNK = D_FF // 2
assert FF_CHUNK % 128 == 0


def _mix_ffn_kernel(h_ref, o_ref, y_ref, gab_ref, p_ref, pre0_ref, pre1_ref,
                    wa_ref, wg1_ref, wg2_ref, wo_ref, gffn_ref, wup_ref, cw_ref, cb_ref, wdn_ref,
                    gple_ref, wpg_ref, wpl_ref, hout_ref, cs0_ref, cs1_ref, carry_s, *, seq_tiles):
    tm = h_ref.shape[0]
    a_out = jnp.dot(o_ref[...], wa_ref[...], preferred_element_type=jnp.float32)
    yg = jax.nn.gelu(y_ref[...]).astype(jnp.bfloat16)
    b_out = (jnp.dot(yg, wg1_ref[...], preferred_element_type=jnp.float32)
             * jax.nn.sigmoid(jnp.dot(yg, wg2_ref[...], preferred_element_type=jnp.float32)))
    mixed = (jax.nn.sigmoid(gab_ref[:, :D_MODEL]) * a_out + jax.nn.sigmoid(gab_ref[:, D_MODEL:]) * b_out)
    h1 = h_ref[...] + _bdot(mixed, wo_ref[...])

    xn = _rms(h1, gffn_ref[...]).astype(jnp.bfloat16)
    if seq_tiles:
        @pl.when(pl.program_id(0) % seq_tiles == 0)
        def _():
            carry_s[...] = jnp.zeros_like(carry_s)
        row = lax.broadcasted_iota(jnp.int32, (tm, 1), 0)
    ffn = jnp.zeros((tm, D_MODEL), jnp.float32)
    for c in range(D_FF // FF_CHUNK):
        sl = slice(c * FF_CHUNK, (c + 1) * FF_CHUNK)
        gp = jnp.dot(xn, wup_ref[:, sl], preferred_element_type=jnp.float32)
        val = jnp.dot(xn, wup_ref[:, D_FF + c * FF_CHUNK:D_FF + (c + 1) * FF_CHUNK], preferred_element_type=jnp.float32)
        if seq_tiles:
            old1, old2 = carry_s[7:8, sl], carry_s[6:7, sl]
            prev1 = jnp.where(row == 0, old1, pltpu.roll(gp, 1, 0))
            prev2 = jnp.where(row == 0, old2, jnp.where(row == 1, old1, pltpu.roll(gp, 2, 0)))
            carry_s[:, sl] = gp[tm - 8:, :]
            cs0_ref[0, :, sl] = gp[tm - 2:tm - 1, :]
            cs1_ref[0, :, sl] = gp[tm - 1:tm, :]
        else:
            prev2, prev1 = pre0_ref[:, sl], pre1_ref[:, sl]
            cs0_ref[:, sl] = prev1
            cs1_ref[:, sl] = gp
        conv = cb_ref[:, sl] + cw_ref[0:1, sl] * prev2 + cw_ref[1:2, sl] * prev1 + cw_ref[2:3, sl] * gp
        ffn = ffn + _bdot(jax.nn.gelu(conv) * val, wdn_ref[sl, :])
    h2 = h1 + ffn

    gate = jax.nn.sigmoid(_bdot(_rms(h2, gple_ref[...]), wpg_ref[...]))
    hout_ref[...] = h2 + gate * _bdot(p_ref[...], wpl_ref[...])


def _mix_ffn_pallas(h2d, o2d, y, gab, p2d, prefix, w, seq_len):
    (w_a, w_glu1, w_glu2, w_o, g_ffn, w_up, conv_w, conv_b, w_down, g_ple, w_ple_gate, w_ple) = w
    N = h2d.shape[0]
    T = seq_len
    B = N // T
    tm = _pick_tile(N, (ROW_TILE, 128))
    nt = max(T // tm, 1)
    seq = prefix is None
    assert (seq and T % tm == 0 and tm >= 8) or (not seq and T == 1)
    row = lambda wd: pl.BlockSpec((tm, wd), lambda i: (i, 0))
    const = lambda a: pl.BlockSpec(a.shape, lambda i: (0,) * a.ndim, pipeline_mode=pl.Buffered(1))
    f32 = jnp.float32
    if seq:
        y_spec = pl.BlockSpec((tm, SSM_WIDTH), lambda i: (i % nt, i // nt))
        pre = (jnp.zeros((8, D_FF), f32),) * 2
        pre_spec = const(pre[0])
        cs_shape = jax.ShapeDtypeStruct((B, 1, D_FF), f32)
        cs_spec = pl.BlockSpec((1, 1, D_FF), lambda i: (i // nt, 0, 0))
    else:
        y_spec = row(SSM_WIDTH)
        pre = (prefix[:, 0], prefix[:, 1])
        pre_spec = row(D_FF)
        cs_shape = jax.ShapeDtypeStruct((N, D_FF), f32)
        cs_spec = row(D_FF)
    bf = lambda a: a.astype(jnp.bfloat16)
    vec = lambda a: a.reshape(1, -1)
    cw8 = jnp.concatenate([conv_w, jnp.zeros((8 - CONV_W, D_FF), f32)], axis=0)
    consts = (bf(w_a), bf(w_glu1), bf(w_glu2), bf(w_o), vec(g_ffn), bf(w_up), cw8, vec(conv_b), bf(w_down),
              vec(g_ple), bf(w_ple_gate), bf(w_ple))
    hout, cs0, cs1 = pl.pallas_call(
        functools.partial(_mix_ffn_kernel, seq_tiles=nt if seq else 0),
        out_shape=(jax.ShapeDtypeStruct((N, D_MODEL), f32), cs_shape, cs_shape),
        grid=(N // tm,),
        in_specs=[row(D_MODEL), row(Q_W), y_spec, row(2 * D_MODEL), row(p2d.shape[1]), pre_spec, pre_spec]
        + [const(a) for a in consts],
        out_specs=(row(D_MODEL), cs_spec, cs_spec),
        scratch_shapes=[pltpu.VMEM((8, D_FF), f32)],
        compiler_params=pltpu.CompilerParams(dimension_semantics=("arbitrary",), vmem_limit_bytes=VMEM_LIMIT_BYTES),
        name="mix_ffn",
    )(h2d, o2d, y, gab, p2d, *pre, *consts)
    return hout, jnp.stack([cs0.reshape(B, D_FF), cs1.reshape(B, D_FF)], axis=1)


HALF_ROWS = CMP_BLOCK // CMP_STRIDE
assert HALF_ROWS == 2


def _rope_lanes(x, cos, sin_signed):
    w = x.shape[-1]
    half = HEAD_DIM // 2
    lane = lax.broadcasted_iota(jnp.int32, x.shape, x.ndim - 1)
    first = (lane % HEAD_DIM) < half
    partner = jnp.where(first, pltpu.roll(x, w - half, x.ndim - 1), pltpu.roll(x, half, x.ndim - 1))
    return x * cos + partner * sin_signed


def _compress_rows(load, w_ref, pe_ref, w1_ref, w2_ref, n_half):
    acc = jnp.zeros((n_half, 4 * 128), jnp.float32)
    for j in range(CMP_STRIDE):
        acc = acc + jnp.dot(load(j).astype(jnp.bfloat16), w_ref[j], preferred_element_type=jnp.float32)
    pa = acc[:, :256]
    pb = pltpu.roll(acc[:, 256:], n_half - 1, 0)
    bias = jnp.dot(pe_ref[...].astype(jnp.bfloat16), w1_ref[...], preferred_element_type=jnp.float32)[0:1]
    bias2 = jnp.concatenate([bias, bias], axis=1)
    hdn = jax.nn.gelu(pa + pb + bias2)
    return jnp.dot(hdn.astype(jnp.bfloat16), w2_ref[...], preferred_element_type=jnp.float32)


def _compress_kernel(xk_ref, xv_ref, wk_ref, wv_ref, pek_ref, pev_ref, w1k_ref, w1v_ref, w2k_ref, w2v_ref,
                     cos_ref, sin_ref, ko_ref, vo_ref, *, n_half, n_cmp):
    row = lax.broadcasted_iota(jnp.int32, (n_half, KV_W), 0)
    k = _compress_rows(lambda j: xk_ref[0, pl.ds(j, n_half, stride=CMP_STRIDE), :],
                       wk_ref, pek_ref, w1k_ref, w2k_ref, n_half)
    k = _rope_lanes(k, cos_ref[...], sin_ref[...])
    v = _compress_rows(lambda j: xv_ref[0, pl.ds(j, n_half, stride=CMP_STRIDE), :],
                       wv_ref, pev_ref, w1v_ref, w2v_ref, n_half)
    ko_ref[0] = jnp.where(row < n_cmp, k, 0.0).astype(ko_ref.dtype)
    vo_ref[0] = jnp.where(row < n_cmp, v, 0.0).astype(vo_ref.dtype)


def _blockdiag2(w):
    z = jnp.zeros_like(w)
    return jnp.concatenate([jnp.concatenate([w, z], axis=-1), jnp.concatenate([z, w], axis=-1)], axis=-2)


def _compress_weights(w1, pe, w2):
    bd = _blockdiag2(w1)
    wcat = jnp.concatenate([bd[:CMP_STRIDE], bd[CMP_STRIDE:]], axis=-1).astype(jnp.bfloat16)
    pe_flat = jnp.broadcast_to(pe.reshape(1, -1), (8, pe.size))
    w1_flat = w1.reshape(-1, w1.shape[-1]).astype(jnp.bfloat16)
    w2bd = _blockdiag2(w2).astype(jnp.bfloat16)
    return wcat, pe_flat, w1_flat, w2bd


def _rope_tables(pos, reps):
    half = HEAD_DIM // 2
    inv = jnp.float32(ROPE_THETA) ** (-jnp.arange(half, dtype=jnp.float32) / half)
    ang = pos.astype(jnp.float32)[:, None] * inv[None, :]
    cos = jnp.cos(ang)
    sin = jnp.sin(ang)
    return (jnp.tile(jnp.concatenate([cos, cos], axis=-1), (1, reps)),
            jnp.tile(jnp.concatenate([-sin, sin], axis=-1), (1, reps)))


def _compress_pallas(rows, cmpw):
    wk1, pek, wk2, wv1, pev, wv2 = cmpw
    B, L, _ = rows.shape
    n_half = L // CMP_STRIDE
    n_cmp = n_half - 1
    wk, pekf, w1k, w2k = _compress_weights(wk1, pek, wk2)
    wv, pevf, w1v, w2v = _compress_weights(wv1, pev, wv2)
    end = jnp.arange(n_half) * CMP_STRIDE + CMP_BLOCK - 1
    cos, sin = _rope_tables(end, N_KV_HEADS)
    full = lambda a: pl.BlockSpec(a.shape, lambda b: (0,) * a.ndim)
    consts = (wk, wv, pekf, pevf, w1k, w1v, w2k, w2v, cos, sin)
    return pl.pallas_call(
        functools.partial(_compress_kernel, n_half=n_half, n_cmp=n_cmp),
        out_shape=(jax.ShapeDtypeStruct((B, n_half, KV_W), jnp.bfloat16),) * 2,
        grid=(B,),
        in_specs=[pl.BlockSpec((1, L, KV_W), lambda b: (b, 0, 0)), pl.BlockSpec((1, L, KV_W), lambda b: (b, 0, 1))]
        + [full(a) for a in consts],
        out_specs=(pl.BlockSpec((1, n_half, KV_W), lambda b: (b, 0, 0)),) * 2,
        compiler_params=pltpu.CompilerParams(dimension_semantics=("parallel",), vmem_limit_bytes=VMEM_LIMIT_BYTES),
        name="compress",
    )(rows, rows, *consts)


ATT_TQ = 256
SEL_KC = 512
BIG_NEG = -3.0e38


def _softmax_rows(s, mask):
    s = jnp.where(mask, s, NEG_INF)
    m = jnp.max(s, axis=-1, keepdims=True)
    e = jnp.where(mask, jnp.exp(s - m), 0.0)
    return e / jnp.maximum(jnp.sum(e, axis=-1, keepdims=True), 1e-30)


def _select_blocks(imp, tpos, n_sb):
    tq = imp.shape[0]
    jl = lax.broadcasted_iota(jnp.int32, (tq, 128), 1)
    jf = jl.astype(jnp.float32)
    forced = (jl == 0) | (jl == (tpos >> 6))
    imp = jnp.where(forced, FORCE_SCORE, imp)
    imp = jnp.where(jl * SEL_BLOCK <= tpos, imp, NEG_INF)
    imp = jnp.where(jl < n_sb, imp, BIG_NEG)
    sel = jnp.zeros((tq, 128), jnp.float32)
    for _ in range(N_SEL):
        m = jnp.max(imp, axis=-1, keepdims=True)
        first = jnp.min(jnp.where(imp == m, jf, 1e9), axis=-1, keepdims=True)
        hit = jf == first
        sel = jnp.where(hit & (m > 0.5 * NEG_INF), 1.0, sel)
        imp = jnp.where(hit, BIG_NEG, imp)
    return sel


def _select_blocks_t(imp, tpos, n_sb):
    nj, tq = imp.shape
    jr = lax.broadcasted_iota(jnp.int32, (nj, tq), 0)
    jf = jr.astype(jnp.float32)
    forced = (jr == 0) | (jr == (tpos >> 6))
    imp = jnp.where(forced, FORCE_SCORE, imp)
    imp = jnp.where(jr * SEL_BLOCK <= tpos, imp, NEG_INF)
    imp = jnp.where(jr < n_sb, imp, BIG_NEG)
    sel = jnp.zeros((nj, tq), jnp.float32)
    for _ in range(N_SEL):
        m = jnp.max(imp, axis=0, keepdims=True)
        first = jnp.min(jnp.where(imp == m, jf, 1e9), axis=0, keepdims=True)
        hit = jf == first
        sel = jnp.where(hit & (m > 0.5 * NEG_INF), 1.0, sel)
        imp = jnp.where(hit, BIG_NEG, imp)
    return sel


def _nsa_prompt_kernel(q_ref, kcmp_ref, vcmpt_ref, ks_ref, vst_ref, kw_ref, vwt_ref, glt_ref, ovt_ref, et_ref, o_ref,
                       *, tq, n_cmp, n_sb):
    i = pl.program_id(2)
    t0 = i * tq
    bf = jnp.bfloat16
    qf = q_ref[0]
    q4 = jnp.concatenate([qf[:, g * HEAD_DIM:(g + 1) * HEAD_DIM] for g in range(GROUP)], axis=0)
    lanes4 = lambda x: jnp.concatenate([x] * GROUP, axis=1)
    tq_pos = t0 + lax.broadcasted_iota(jnp.int32, (1, tq), 1)
    tpos = lanes4(tq_pos)

    nr = lax.broadcasted_iota(jnp.int32, (128, 1), 0)
    maskc = ((nr * CMP_STRIDE + (CMP_BLOCK - 1)) <= tpos) & (nr < n_cmp)
    sc = jnp.where(maskc, lax.dot_general(kcmp_ref[0, 0], q4, _NT, preferred_element_type=jnp.float32), NEG_INF)
    ec = jnp.where(maskc, jnp.exp(sc - jnp.max(sc, axis=0, keepdims=True)), 0.0)
    pc = ec / jnp.maximum(jnp.sum(ec, axis=0, keepdims=True), 1e-30)
    o_cmp = jnp.dot(vcmpt_ref[0, 0], pc.astype(bf), preferred_element_type=jnp.float32)
    psum = pc[:, :tq] + pc[:, tq:2 * tq] + pc[:, 2 * tq:3 * tq] + pc[:, 3 * tq:]
    p_hi = psum.astype(bf)
    p_lo = (psum - p_hi.astype(jnp.float32)).astype(bf)
    imp = (jnp.dot(ovt_ref[...], p_hi, preferred_element_type=jnp.float32)
           + jnp.dot(ovt_ref[...], p_lo, preferred_element_type=jnp.float32))
    nj = -(-n_sb // 8) * 8
    sel = _select_blocks_t(imp[:nj], tq_pos, n_sb)
    sel = jnp.concatenate([sel, jnp.zeros((128 - nj, tq), jnp.float32)], axis=0).astype(bf)

    per_kc = SEL_KC // tq

    def sel_step(c, carry):
        m, l, acc = carry
        k0 = pl.multiple_of(c * SEL_KC, SEL_KC)
        kpos = k0 + lax.broadcasted_iota(jnp.int32, (SEL_KC, 1), 0)
        picked = jnp.dot(et_ref[pl.ds(k0, SEL_KC), :], sel, preferred_element_type=jnp.float32)
        mask = lanes4((picked > 0.5) & (kpos <= tq_pos))
        s = lax.dot_general(ks_ref[0, 0, pl.ds(k0, SEL_KC), :], q4, _NT, preferred_element_type=jnp.float32)
        s = jnp.where(mask, s, NEG_INF)
        m_new = jnp.maximum(m, jnp.max(s, axis=0, keepdims=True))
        alpha = jnp.exp(m - m_new)
        p = jnp.exp(s - m_new)
        l = alpha * l + jnp.sum(p, axis=0, keepdims=True)
        vt = jnp.concatenate([vst_ref[0, 0, c * per_kc + r] for r in range(per_kc)], axis=1)
        acc = alpha * acc + jnp.dot(vt, p.astype(bf), preferred_element_type=jnp.float32)
        return m_new, l, acc

    nq = GROUP * tq
    init = (jnp.full((1, nq), NEG_INF, jnp.float32), jnp.zeros((1, nq), jnp.float32),
            jnp.zeros((HEAD_DIM, nq), jnp.float32))
    n_kc = (t0 + tq + SEL_KC - 1) // SEL_KC
    _, l_s, acc_s = lax.fori_loop(0, n_kc, sel_step, init)
    o_sel = acc_s / jnp.maximum(l_s, 1e-30)

    n_wc = WINDOW // tq + 1
    c0 = jnp.maximum(i - WINDOW // tq, 0)
    w0 = pl.multiple_of(c0 * tq, tq)
    d = tpos - (w0 + lax.broadcasted_iota(jnp.int32, (n_wc * tq, 1), 0))
    sw = lax.dot_general(kw_ref[0, 0, pl.ds(w0, n_wc * tq), :], q4, _NT, preferred_element_type=jnp.float32)
    sw = jnp.where((d >= 0) & (d < WINDOW), sw, NEG_INF)
    ew = jnp.exp(sw - jnp.max(sw, axis=0, keepdims=True))
    vwt = jnp.concatenate([vwt_ref[0, 0, c0 + r] for r in range(n_wc)], axis=1)
    o_win = (jnp.dot(vwt, ew.astype(bf), preferred_element_type=jnp.float32)
             / jnp.maximum(jnp.sum(ew, axis=0, keepdims=True), 1e-30))

    gate = jax.nn.sigmoid(glt_ref[0, 0])
    for g in range(GROUP):
        sl = slice(g * tq, (g + 1) * tq)
        ot = (gate[3 * g:3 * g + 1] * o_cmp[:, sl] + gate[3 * g + 1:3 * g + 2] * o_sel[:, sl]
              + gate[3 * g + 2:3 * g + 3] * o_win[:, sl])
        o_ref[0, :, g * HEAD_DIM:(g + 1) * HEAD_DIM] = ot.T.astype(o_ref.dtype)


def _overlap_matrix(n_cmp, n_sb):
    start = np.arange(128) * CMP_STRIDE
    end = start + CMP_BLOCK - 1
    sb = np.arange(128) * SEL_BLOCK
    ov = (start[:, None] < sb[None, :] + SEL_BLOCK) & (end[:, None] >= sb[None, :])
    ov &= (np.arange(128)[:, None] < n_cmp) & (np.arange(128)[None, :] < n_sb)
    return jnp.asarray(ov, jnp.bfloat16)


def _heads_major(x):
    B, T, W = x.shape
    return x.reshape(B, T, N_KV_HEADS, W // N_KV_HEADS).transpose(0, 2, 1, 3)


def _nsa_prompt_pallas(q, kcmp, vcmp, ks, vs_t, kw, vw_t, gl_t):
    B, T, _ = q.shape
    tq = ATT_TQ
    n_cmp = (T - CMP_BLOCK) // CMP_STRIDE + 1
    n_sb = -(-T // SEL_BLOCK)
    assert T % SEL_KC == 0 and T >= WINDOW + tq and kcmp.shape[1] <= 128 and n_sb <= 128
    kcmp, vcmp = (jnp.pad(a, ((0, 0), (0, 128 - a.shape[1]), (0, 0))) for a in (kcmp, vcmp))
    vcmp_t = vcmp.reshape(B, 128, N_KV_HEADS, HEAD_DIM).transpose(0, 2, 3, 1)
    et = jnp.asarray((np.arange(T)[:, None] // SEL_BLOCK) == np.arange(128)[None, :], jnp.bfloat16)
    k_spec = pl.BlockSpec((1, 1, T, HEAD_DIM), lambda b, k, i: (b, k, 0, 0))
    v_spec = pl.BlockSpec((1, 1, T // tq, HEAD_DIM, tq), lambda b, k, i: (b, k, 0, 0, 0))
    return pl.pallas_call(
        functools.partial(_nsa_prompt_kernel, tq=tq, n_cmp=n_cmp, n_sb=n_sb),
        out_shape=jax.ShapeDtypeStruct((B, T, Q_W), jnp.bfloat16),
        grid=(B, N_KV_HEADS, T // tq),
        in_specs=[pl.BlockSpec((1, tq, GROUP * HEAD_DIM), lambda b, k, i: (b, i, k)),
                  pl.BlockSpec((1, 1, 128, HEAD_DIM), lambda b, k, i: (b, k, 0, 0)),
                  pl.BlockSpec((1, 1, HEAD_DIM, 128), lambda b, k, i: (b, k, 0, 0)),
                  k_spec, v_spec, k_spec, v_spec,
                  pl.BlockSpec((1, 1, 3 * GROUP, tq), lambda b, k, i: (b, k, 0, i)),
                  pl.BlockSpec((128, 128), lambda b, k, i: (0, 0)),
                  pl.BlockSpec((T, 128), lambda b, k, i: (0, 0))],
        out_specs=pl.BlockSpec((1, tq, GROUP * HEAD_DIM), lambda b, k, i: (b, i, k)),
        compiler_params=pltpu.CompilerParams(
            dimension_semantics=("parallel", "parallel", "arbitrary"), vmem_limit_bytes=VMEM_LIMIT_BYTES),
        name="nsa_prompt",
    )(q, _heads_major(kcmp), vcmp_t, ks, vs_t, kw, vw_t, gl_t, _overlap_matrix(n_cmp, n_sb).T, et)


_NT = (((1,), (1,)), ((), ()))
SAMPLE_NB = 2


def _decode_attend(s, mask, s_new, mask_new, pv_fn, v_new):
    sm = jnp.where(mask, s, NEG_INF)
    sn = jnp.where(mask_new, s_new, NEG_INF)
    m = jnp.maximum(jnp.max(sm, axis=1, keepdims=True), sn)
    e = jnp.where(mask, jnp.exp(sm - m), 0.0)
    en = jnp.where(mask_new, jnp.exp(sn - m), 0.0)
    l = jnp.sum(e, axis=1, keepdims=True) + en
    acc = pv_fn(e.astype(jnp.bfloat16)) + (en.astype(jnp.bfloat16).astype(jnp.float32)
                                            * v_new.astype(jnp.bfloat16).astype(jnp.float32))
    return acc / jnp.maximum(l, 1e-30)


def _nsa_sample_kernel(pt_ref, *refs, n_pages, n_sb, nb):
    del pt_ref
    cmp_pages = refs[:nb * n_pages]
    sel_pages_all = refs[nb * n_pages:2 * nb * n_pages]
    (win_ref, q_ref, gl_ref, ksn_ref, vsn_ref, kwn_ref, vwn_ref, kwc_ref, vwc_ref,
     wk_ref, wv_ref, pek_ref, pev_ref, w1k_ref, w1v_ref, w2k_ref, w2v_ref, cos_ref, sin_ref, ov_ref, ex_ref,
     o_ref, wout_ref, xk_s, xv_s) = refs[2 * nb * n_pages:]
    past = n_pages * PAGE_SIZE
    n_half = past // CMP_STRIDE
    n_cmp = (past + 1 - CMP_BLOCK) // CMP_STRIDE + 1

    for p in range(nb * n_pages):
        xk_s[p * PAGE_SIZE:(p + 1) * PAGE_SIZE, :] = cmp_pages[p][0, 0, 0].T
        xv_s[p * PAGE_SIZE:(p + 1) * PAGE_SIZE, :] = cmp_pages[p][0, 0, 1].T
    rows = nb * n_half
    valid = (lax.broadcasted_iota(jnp.int32, (rows, KV_W), 0) % n_half) < n_cmp
    kcmp = _compress_rows(lambda j: xk_s[pl.ds(j, rows, stride=CMP_STRIDE), :],
                          wk_ref, pek_ref, w1k_ref, w2k_ref, rows)
    kcmp = jnp.where(valid, _rope_lanes(kcmp, cos_ref[...], sin_ref[...]), 0.0).astype(jnp.bfloat16)
    vcmp = _compress_rows(lambda j: xv_s[pl.ds(j, rows, stride=CMP_STRIDE), :],
                          wv_ref, pev_ref, w1v_ref, w2v_ref, rows)
    vcmp = jnp.where(valid, vcmp, 0.0).astype(jnp.bfloat16)
    for s in range(nb):
        _nsa_sample_one(s, kcmp[s * n_half:(s + 1) * n_half], vcmp[s * n_half:(s + 1) * n_half],
                        sel_pages_all[s * n_pages:(s + 1) * n_pages], win_ref, q_ref, gl_ref, ksn_ref, vsn_ref,
                        kwn_ref, vwn_ref, kwc_ref, vwc_ref, ov_ref, ex_ref, o_ref, wout_ref, n_pages, n_sb)


def _nsa_sample_one(s, kcmp, vcmp, sel_pages, win_ref, q_ref, gl_ref, ksn_ref, vsn_ref, kwn_ref, vwn_ref,
                    kwc_ref, vwc_ref, ov_ref, ex_ref, o_ref, wout_ref, n_pages, n_sb):
    past = n_pages * PAGE_SIZE
    qpos = past
    n_half = past // CMP_STRIDE
    n_cmp = (past + 1 - CMP_BLOCK) // CMP_STRIDE + 1
    wb = win_ref.shape[-1]
    bf = jnp.bfloat16

    row8 = lax.broadcasted_iota(jnp.int32, (8, KV_W), 0)
    lane8 = lax.broadcasted_iota(jnp.int32, (8, KV_W), 1)
    top1 = lax.broadcasted_iota(jnp.int32, (8, 1), 0) < GROUP
    q8 = q_ref[s].astype(jnp.float32)
    q2 = jnp.where((row8 < GROUP) == (lane8 < HEAD_DIM), jnp.concatenate([q8, q8], axis=1), 0.0).astype(bf)
    q2f = q2.astype(jnp.float32)

    def halves(x):
        return jnp.where(top1, x[:, :HEAD_DIM], x[:, HEAD_DIM:])

    def new_score(k_new):
        return jnp.sum(q2f * k_new.astype(bf).astype(jnp.float32), axis=1, keepdims=True)

    sc = lax.dot_general(q2, kcmp, _NT, preferred_element_type=jnp.float32)
    nl = lax.broadcasted_iota(jnp.int32, (8, n_half), 1)
    pc = _softmax_rows(sc, ((nl * CMP_STRIDE + (CMP_BLOCK - 1)) <= qpos) & (nl < n_cmp))
    o_cmp = halves(jnp.dot(pc.astype(bf), vcmp, preferred_element_type=jnp.float32))
    pk0 = jnp.sum(jnp.where(top1, pc, 0.0), axis=0, keepdims=True)
    pk1 = jnp.sum(jnp.where(top1, 0.0, pc), axis=0, keepdims=True)
    rown = lax.broadcasted_iota(jnp.int32, (8, n_half), 0)
    p2 = jnp.where(rown == 0, pk0, jnp.where(rown == 1, pk1, 0.0))
    p_hi = p2.astype(bf)
    p_lo = (p2 - p_hi.astype(jnp.float32)).astype(bf)
    imp = (jnp.dot(p_hi, ov_ref[...], preferred_element_type=jnp.float32)
           + jnp.dot(p_lo, ov_ref[...], preferred_element_type=jnp.float32))
    sel2 = _select_blocks(imp, jnp.full((8, 1), qpos, jnp.int32), n_sb)

    picked2 = jnp.dot(sel2.astype(bf), ex_ref[...], preferred_element_type=jnp.float32)
    mask_s = jnp.where(top1, picked2[0:1], picked2[1:2]) > 0.5
    seln = jnp.sum(jnp.where(lane8 == qpos // SEL_BLOCK, sel2, 0.0), axis=1, keepdims=True)
    mask_new = jnp.where(top1, seln[0:1], seln[1:2]) > 0.5
    s_s = jnp.concatenate([jnp.dot(q2, sel_pages[p][0, 0, 0].astype(bf), preferred_element_type=jnp.float32)
                           for p in range(n_pages)], axis=1)

    def pv_sel(e):
        acc = jnp.zeros((8, KV_W), jnp.float32)
        for p in range(n_pages):
            acc = acc + lax.dot_general(e[:, p * PAGE_SIZE:(p + 1) * PAGE_SIZE], sel_pages[p][0, 0, 1].astype(bf),
                                        _NT, preferred_element_type=jnp.float32)
        return acc

    o_sel = halves(_decode_attend(s_s, mask_s, new_score(ksn_ref[s]), mask_new, pv_sel, vsn_ref[s]))

    s_w = jnp.dot(q2, win_ref[0, s, 0].astype(bf), preferred_element_type=jnp.float32)
    kpos = past - wb + lax.broadcasted_iota(jnp.int32, (8, wb), 1)
    mask_w = (qpos - kpos >= 0) & (qpos - kpos < WINDOW) & (kpos >= 0)
    pv_win = lambda e: lax.dot_general(e, win_ref[0, s, 1].astype(bf), _NT, preferred_element_type=jnp.float32)
    o_win = halves(_decode_attend(s_w, mask_w, new_score(kwn_ref[s]), jnp.full((8, 1), True), pv_win, vwn_ref[s]))

    gate = jax.nn.sigmoid(gl_ref[s])
    o_ref[s] = (gate[:, 0:1] * o_cmp + gate[:, 1:2] * o_sel + gate[:, 2:3] * o_win).astype(o_ref.dtype)

    lane_w = lax.broadcasted_iota(jnp.int32, (KV_W, wb), 1)
    wout_ref[s, 0] = jnp.where(lane_w == wb - 1, kwc_ref[s], pltpu.roll(win_ref[0, s, 0], wb - 1, 1))
    wout_ref[s, 1] = jnp.where(lane_w == wb - 1, vwc_ref[s], pltpu.roll(win_ref[0, s, 1], wb - 1, 1))


def _cache_rows_on_lanes(c):
    nd = c.ndim
    c = jnp.moveaxis(c, nd - 4, nd - 1)
    return c.reshape(c.shape[:-3] + (c.shape[-3] * c.shape[-2], c.shape[-1]))


def _nsa_sample_pallas(layer, q, gl, ks, vs, kw, vw, cmpw, cmp_t, sel_t, win_t, page_table):
    B = q.shape[0]
    n_pages = page_table.shape[1]
    past = n_pages * PAGE_SIZE
    wb = win_t.shape[-1]
    n_half = past // CMP_STRIDE
    n_cmp = (past + 1 - CMP_BLOCK) // CMP_STRIDE + 1
    n_sb = -(-(past + 1) // SEL_BLOCK)
    wk1, pek, wk2, wv1, pev, wv2 = cmpw
    wk, pekf, w1k, w2k = _compress_weights(wk1, pek, wk2)
    wv, pevf, w1v, w2v = _compress_weights(wv1, pev, wv2)
    nb = _pick_tile(B, (SAMPLE_NB,))
    nb = nb if nb == SAMPLE_NB else 1
    cos, sin = _rope_tables(jnp.tile(jnp.arange(n_half) * CMP_STRIDE + CMP_BLOCK - 1, nb), N_KV_HEADS)
    ov = _overlap_matrix(n_cmp, n_sb)[:n_half]
    ex = jnp.asarray((np.arange(past)[None, :] // SEL_BLOCK) == np.arange(128)[:, None], jnp.bfloat16)
    consts = (wk, wv, pekf, pevf, w1k, w1v, w2k, w2v, cos, sin, ov, ex)
    row3 = lambda x: x.reshape(B, 1, KV_W)
    col3 = lambda x: x.reshape(B, KV_W, 1)
    per_b = (q.reshape(B, N_HEADS, HEAD_DIM), gl.reshape(B, N_HEADS, 3), row3(ks), row3(vs), row3(kw), row3(vw),
             col3(kw), col3(vw))
    page_spec = lambda s, p: pl.BlockSpec((1, 1, 2, KV_W, PAGE_SIZE),
                                          lambda b, pt: (layer, pt[b * nb + s, p], 0, 0, 0))
    b_spec = lambda a: pl.BlockSpec((nb,) + a.shape[1:], lambda b, pt: (b,) + (0,) * (a.ndim - 1))
    full = lambda a: pl.BlockSpec(a.shape, lambda b, pt: (0,) * a.ndim)
    in_specs = ([page_spec(s, p) for s in range(nb) for p in range(n_pages)] * 2
                + [pl.BlockSpec((1, nb, 2, KV_W, wb), lambda b, pt: (layer, b, 0, 0, 0))]
                + [b_spec(a) for a in per_b] + [full(a) for a in consts])
    return pl.pallas_call(
        functools.partial(_nsa_sample_kernel, n_pages=n_pages, n_sb=n_sb, nb=nb),
        out_shape=(jax.ShapeDtypeStruct((B, N_HEADS, HEAD_DIM), jnp.bfloat16),
                   jax.ShapeDtypeStruct((B, 2, KV_W, wb), jnp.float32)),
        grid_spec=pltpu.PrefetchScalarGridSpec(
            num_scalar_prefetch=1, grid=(B // nb,), in_specs=in_specs,
            out_specs=(pl.BlockSpec((nb, N_HEADS, HEAD_DIM), lambda b, pt: (b, 0, 0)),
                       pl.BlockSpec((nb, 2, KV_W, wb), lambda b, pt: (b, 0, 0, 0))),
            scratch_shapes=[pltpu.VMEM((nb * past, KV_W), jnp.float32)] * 2),
        compiler_params=pltpu.CompilerParams(dimension_semantics=("arbitrary",), vmem_limit_bytes=VMEM_LIMIT_BYTES),
        name="nsa_sample",
    )(page_table, *([cmp_t] * (nb * n_pages)), *([sel_t] * (nb * n_pages)), win_t, *per_b, *consts)


SSM_N = SSM_GROUPS * SSM_STATE
SSM_LANE_BLK = 512
SSM_TL = 64


def _ssm_kernel(u_ref, h0r_ref, h0i_ref, ar_ref, ai_ref, bm_ref, cr_ref, ci_ref, d_ref,
                y_ref, hr_ref, hi_ref, xr_s, xi_s, *, tl, nb):
    c = pl.program_id(0)

    @pl.when(c == 0)
    def _():
        hr_ref[...] = h0r_ref[...]
        hi_ref[...] = h0i_ref[...]

    u = u_ref[...]
    ub = u.astype(jnp.bfloat16)
    n_grp = SSM_WIDTH // 128
    for j in range(n_grp):
        bu = jnp.dot(ub[:, 128 * j:128 * (j + 1)], bm_ref[j], preferred_element_type=jnp.float32)
        xr_s[:, 512 * j:512 * (j + 1)] = bu[:, :512]
        xi_s[:, 512 * j:512 * (j + 1)] = bu[:, 512:]

    for lb in range(SSM_N // SSM_LANE_BLK):
        sl = slice(lb * SSM_LANE_BLK, (lb + 1) * SSM_LANE_BLK)
        ar = jnp.broadcast_to(ar_ref[:, sl], (8, SSM_LANE_BLK))
        ai = jnp.broadcast_to(ai_ref[:, sl], (8, SSM_LANE_BLK))
        for r in range(nb // 8):
            def step(t, carry):
                hr, hi = carry
                row = pl.multiple_of(t * nb + r * 8, 8)
                xr = xr_s[pl.ds(row, 8), sl]
                xi = xi_s[pl.ds(row, 8), sl]
                nr = ar * hr - ai * hi + xr
                ni = ar * hi + ai * hr + xi
                xr_s[pl.ds(row, 8), sl] = nr
                xi_s[pl.ds(row, 8), sl] = ni
                return nr, ni

            hr, hi = lax.fori_loop(0, tl, step, (hr_ref[r * 8:(r + 1) * 8, sl], hi_ref[r * 8:(r + 1) * 8, sl]))
            hr_ref[r * 8:(r + 1) * 8, sl] = hr
            hi_ref[r * 8:(r + 1) * 8, sl] = hi

    for j in range(n_grp):
        yr = jnp.dot(xr_s[:, 512 * j:512 * (j + 1)].astype(jnp.bfloat16), cr_ref[j], preferred_element_type=jnp.float32)
        yi = jnp.dot(xi_s[:, 512 * j:512 * (j + 1)].astype(jnp.bfloat16), ci_ref[j], preferred_element_type=jnp.float32)
        y_ref[:, 128 * j:128 * (j + 1)] = yr - yi + d_ref[:, 128 * j:128 * (j + 1)] * u[:, 128 * j:128 * (j + 1)]


def _ssm_params(a_re, a_im, log_dt, b_re, b_im, c_re, c_im, d_skip):
    dt = jnp.exp(log_dt)[:, None]
    mag = jnp.exp(dt * a_re)
    ab_re = mag * jnp.cos(dt * a_im)
    ab_im = mag * jnp.sin(dt * a_im)
    den = a_re * a_re + a_im * a_im
    zr = ((ab_re - 1.0) * a_re + ab_im * a_im) / den
    zi = (ab_im * a_re - (ab_re - 1.0) * a_im) / den
    bb_re = zr[..., None] * b_re - zi[..., None] * b_im
    bb_im = zr[..., None] * b_im + zi[..., None] * b_re
    n_grp = SSM_WIDTH // 128
    gpl = 128 // SSM_GROUP
    eye = jnp.eye(gpl, dtype=jnp.float32)

    def b_blocks(bb):
        x = bb.reshape(n_grp, gpl, SSM_STATE, SSM_GROUP)
        return jnp.einsum('jgpc,gh->jgchp', x, eye).reshape(n_grp, 128, gpl * SSM_STATE)

    def c_blocks(cc):
        x = cc.reshape(n_grp, gpl, SSM_GROUP, SSM_STATE)
        return jnp.einsum('jgcp,gh->jgphc', x, eye).reshape(n_grp, gpl * SSM_STATE, 128)

    bm = jnp.concatenate([b_blocks(bb_re), b_blocks(bb_im)], axis=-1).astype(jnp.bfloat16)
    return (ab_re.reshape(1, SSM_N), ab_im.reshape(1, SSM_N), bm,
            c_blocks(c_re).astype(jnp.bfloat16), c_blocks(c_im).astype(jnp.bfloat16), d_skip.reshape(1, SSM_WIDTH))


def _ssm_pallas(u_tb, B, T, h0_re, h0_im, params):
    ab_re, ab_im, bm, cr, ci, d = params
    tl = _pick_tile(T, (SSM_TL,))
    full = lambda a: pl.BlockSpec(a.shape, lambda c: (0,) * a.ndim)
    h0r = h0_re.reshape(B, SSM_N)
    h0i = h0_im.reshape(B, SSM_N)
    consts = (h0r, h0i, ab_re, ab_im, bm, cr, ci, d)
    y, hr, hi = pl.pallas_call(
        functools.partial(_ssm_kernel, tl=tl, nb=B),
        out_shape=(jax.ShapeDtypeStruct((T * B, SSM_WIDTH), jnp.float32),
                   jax.ShapeDtypeStruct((B, SSM_N), jnp.float32), jax.ShapeDtypeStruct((B, SSM_N), jnp.float32)),
        grid=(T // tl,),
        in_specs=[pl.BlockSpec((tl * B, SSM_WIDTH), lambda c: (c, 0))] + [full(a) for a in consts],
        out_specs=(pl.BlockSpec((tl * B, SSM_WIDTH), lambda c: (c, 0)),
                   pl.BlockSpec((B, SSM_N), lambda c: (0, 0)), pl.BlockSpec((B, SSM_N), lambda c: (0, 0))),
        scratch_shapes=[pltpu.VMEM((tl * B, SSM_N), jnp.float32), pltpu.VMEM((tl * B, SSM_N), jnp.float32)],
        compiler_params=pltpu.CompilerParams(dimension_semantics=("arbitrary",), vmem_limit_bytes=VMEM_LIMIT_BYTES),
        name="ssm",
    )(u_tb, *consts)
    return y, hr.reshape(B, SSM_GROUPS, SSM_STATE), hi.reshape(B, SSM_GROUPS, SSM_STATE)


def _rmsnorm(x, g):
    xf = x.astype(jnp.float32)
    y = xf * lax.rsqrt(jnp.mean(xf * xf, axis=-1, keepdims=True) + EPS)
    return (y * g.astype(jnp.float32)).astype(x.dtype)


def _rope(x, pos):
    half = HEAD_DIM // 2
    inv = jnp.float32(ROPE_THETA) ** (-jnp.arange(half, dtype=jnp.float32) / half)
    ang = pos.astype(jnp.float32)[:, None] * inv[None, :]
    cos = jnp.cos(ang)[:, None, :]
    sin = jnp.sin(ang)[:, None, :]
    x1, x2 = x[..., :half], x[..., half:]
    return jnp.concatenate([x1 * cos - x2 * sin, x2 * cos + x1 * sin], axis=-1)


def _masked_softmax(s, mask):
    s = jnp.where(mask, s.astype(jnp.float32), NEG_INF)
    m = jnp.max(s, axis=-1, keepdims=True)
    e = jnp.where(mask, jnp.exp(s - m), 0.0)
    return e / jnp.maximum(jnp.sum(e, axis=-1, keepdims=True), 1e-30)


def _attend(q, k, v, mask):
    s = jnp.einsum('btkgd,bskd->bkgts', q, k) * SCALE
    p = _masked_softmax(s, mask)
    o = jnp.einsum('bkgts,bskd->btkgd', p.astype(v.dtype), v)
    return o, p


def _window_mask(q_pos, k_pos):
    d = q_pos[:, None] - k_pos[None, :]
    return (d >= 0) & (d < WINDOW) & (k_pos[None, :] >= 0)


def _compress(x, w1, pe, w2):
    B, L, K, D = x.shape
    xt = x.transpose(0, 2, 1, 3).reshape(B * K, L, D)
    hdn = lax.conv_general_dilated(xt, w1.astype(xt.dtype), (CMP_STRIDE,), 'VALID',
                                   dimension_numbers=('NWC', 'WIO', 'NWC'))
    hdn = hdn + jnp.einsum('ld,ldf->f', pe, w1)
    out = jax.nn.gelu(hdn) @ w2
    n = out.shape[1]
    return out.reshape(B, K, n, D).transpose(0, 2, 1, 3)


def _cmp_branch(q, kc_full, vc_full, q_pos, wk1, pek, wk2, wv1, pev, wv2):
    kcmp = _compress(kc_full, wk1, pek, wk2)
    vcmp = _compress(vc_full, wv1, pev, wv2)
    n_cmp = kcmp.shape[1]
    start = jnp.arange(n_cmp) * CMP_STRIDE
    end = start + CMP_BLOCK - 1
    kcmp = _rope(kcmp, end)
    o, p = _attend(q, kcmp, vcmp, end[None, :] <= q_pos[:, None])
    L = kc_full.shape[1]
    n_sb = -(-L // SEL_BLOCK)
    sb_start = jnp.arange(n_sb) * SEL_BLOCK
    overlap = ((start[:, None] < sb_start[None, :] + SEL_BLOCK) & (end[:, None] >= sb_start[None, :])).astype(jnp.float32)
    imp = jnp.einsum('bkgtn,nj->btkj', p, overlap)
    blk = jnp.arange(n_sb)
    forced = (blk[None, :] == 0) | (blk[None, :] == (q_pos // SEL_BLOCK)[:, None])
    causal = sb_start[None, :] <= q_pos[:, None]
    imp = jnp.where(forced[None, :, None, :], FORCE_SCORE, imp)
    imp = jnp.where(causal[None, :, None, :], imp, NEG_INF)
    top_v, top_i = lax.top_k(imp, min(N_SEL, n_sb))
    return o, top_i, top_v > 0.5 * NEG_INF


def _sel_attend(q, idx, valid, q_pos, fetch):
    pos = idx[..., None] * SEL_BLOCK + jnp.arange(SEL_BLOCK)
    k, v = fetch(pos)
    B, Tc, K, N, S = pos.shape
    mask = (valid[..., None] & (pos <= q_pos[None, :, None, None, None])).reshape(B, Tc, K, N * S)
    k = k.reshape(B, Tc, K, N * S, HEAD_DIM)
    v = v.reshape(B, Tc, K, N * S, HEAD_DIM)
    s = jnp.einsum('btkgd,btksd->btkgs', q, k) * SCALE
    p = _masked_softmax(s, mask[:, :, :, None, :])
    return jnp.einsum('btkgs,btksd->btkgd', p.astype(v.dtype), v)


def _to_chunks(x, c):
    B, T = x.shape[:2]
    return x.reshape((B, T // c, c) + x.shape[2:]).swapaxes(0, 1)


def _from_chunks(x):
    nc, B, c = x.shape[:3]
    return x.swapaxes(0, 1).reshape((B, nc * c) + x.shape[3:])


def _batch_head_index(B):
    bi = jnp.arange(B)[:, None, None, None, None]
    hd = jnp.arange(N_KV_HEADS)[None, None, :, None, None]
    return bi, hd


def _nsa_prompt(q, kc, vc, ks, vs, kw, vw, cmpw):
    B, T = q.shape[:2]
    pos = jnp.arange(T)
    o_cmp, idx, valid = _cmp_branch(q, kc, vc, pos, *cmpw)
    bi, hd = _batch_head_index(B)

    def fetch(p):
        pc = jnp.clip(p, 0, T - 1)
        return ks[bi, pc, hd], vs[bi, pc, hd]

    qb = Q_BLOCK if T % Q_BLOCK == 0 else T
    o_sel = _from_chunks(lax.map(lambda a: _sel_attend(a[0], a[1], a[2], a[3], fetch),
                                 (_to_chunks(q, qb), _to_chunks(idx, qb), _to_chunks(valid, qb), pos.reshape(-1, qb))))
    kp = jnp.pad(kw, ((0, 0), (WINDOW, 0), (0, 0), (0, 0)))
    vp = jnp.pad(vw, ((0, 0), (WINDOW, 0), (0, 0), (0, 0)))

    def win_block(a):
        c, qc = a
        start = c * qb
        kb = lax.dynamic_slice_in_dim(kp, start, WINDOW + qb, axis=1)
        vb = lax.dynamic_slice_in_dim(vp, start, WINDOW + qb, axis=1)
        qpos = start + jnp.arange(qb)
        kpos = start - WINDOW + jnp.arange(WINDOW + qb)
        return _attend(qc, kb, vb, _window_mask(qpos, kpos))[0]

    o_win = _from_chunks(lax.map(win_block, (jnp.arange(T // qb), _to_chunks(q, qb))))
    n_keep = min(WINDOW, T)
    rows = (jnp.stack([kc, vc], axis=2), jnp.stack([ks, vs], axis=2), jnp.stack([kw, vw], axis=2)[:, T - n_keep:])
    return o_cmp, o_sel, o_win, rows


def _nsa_sample(q, kc, vc, ks, vs, kw, vw, cmpw, cmp_pool, sel_pool, win_buf, page_table):
    B, T = q.shape[:2]
    past = page_table.shape[1] * PAGE_SIZE
    pos = past + jnp.arange(T)
    past_cmp = cmp_pool[page_table].reshape((B, past) + cmp_pool.shape[2:])
    kc_full = jnp.concatenate([past_cmp[:, :, 0], kc], axis=1)
    vc_full = jnp.concatenate([past_cmp[:, :, 1], vc], axis=1)
    o_cmp, idx, valid = _cmp_branch(q, kc_full, vc_full, pos, *cmpw)
    sel_rows = sel_pool.reshape((-1,) + sel_pool.shape[2:])
    bi, hd = _batch_head_index(B)

    def fetch(p):
        pc = jnp.clip(p, 0, past - 1)
        phys = page_table[bi, pc // PAGE_SIZE] * PAGE_SIZE + pc % PAGE_SIZE
        pn = jnp.clip(p - past, 0, T - 1)
        in_past = (p < past)[..., None]
        k = jnp.where(in_past, sel_rows[phys, 0, hd], ks[bi, pn, hd])
        v = jnp.where(in_past, sel_rows[phys, 1, hd], vs[bi, pn, hd])
        return k, v

    o_sel = _sel_attend(q, idx, valid, pos, fetch)
    wb = win_buf.shape[1]
    win_all = jnp.concatenate([win_buf, jnp.stack([kw, vw], axis=2)], axis=1)
    kpos = past - wb + jnp.arange(wb + T)
    o_win, _ = _attend(q, win_all[:, :, 0], win_all[:, :, 1], _window_mask(pos, kpos))
    rows = (jnp.stack([kc, vc], axis=2), jnp.stack([ks, vs], axis=2), win_all[:, T:])
    return o_cmp, o_sel, o_win, rows


def _cmul_combine(e1, e2):
    ar1, ai1, br1, bi1 = e1
    ar2, ai2, br2, bi2 = e2
    return (ar2 * ar1 - ai2 * ai1, ar2 * ai1 + ai2 * ar1,
            ar2 * br1 - ai2 * bi1 + br2, ar2 * bi1 + ai2 * br1 + bi2)


def _ssm_scan(u, h0_re, h0_im, a_re, a_im, log_dt, b_re, b_im, c_re, c_im, d_skip):
    B, T, _ = u.shape
    uf = u.reshape(B, T, SSM_GROUPS, SSM_GROUP)
    ar = a_re
    ai = a_im
    dt = jnp.exp(log_dt)[:, None]
    mag = jnp.exp(dt * ar)
    ab_re = mag * jnp.cos(dt * ai)
    ab_im = mag * jnp.sin(dt * ai)
    den = ar * ar + ai * ai
    zr = ((ab_re - 1.0) * ar + ab_im * ai) / den
    zi = (ab_im * ar - (ab_re - 1.0) * ai) / den
    bb_re = zr[..., None] * b_re - zi[..., None] * b_im
    bb_im = zr[..., None] * b_im + zi[..., None] * b_re
    bu_re = jnp.einsum('gpc,btgc->tbgp', bb_re, uf)
    bu_im = jnp.einsum('gpc,btgc->tbgp', bb_im, uf)
    chunk = SSM_CHUNK if T % SSM_CHUNK == 0 else T
    nc = T // chunk
    shp = (chunk, B, SSM_GROUPS, SSM_STATE)
    bu_re = bu_re.reshape((nc,) + shp)
    bu_im = bu_im.reshape((nc,) + shp)
    a_re_c = jnp.broadcast_to(ab_re, shp)
    a_im_c = jnp.broadcast_to(ab_im, shp)

    def step(carry, xs):
        hr0, hi0 = carry
        xr, xi = xs
        pr, pim, sr, si = lax.associative_scan(_cmul_combine, (a_re_c, a_im_c, xr, xi), axis=0)
        hr = sr + pr * hr0 - pim * hi0
        hi = si + pr * hi0 + pim * hr0
        y = jnp.einsum('gcp,tbgp->tbgc', c_re, hr) - jnp.einsum('gcp,tbgp->tbgc', c_im, hi)
        return (hr[-1], hi[-1]), y

    (hr, hi), y = lax.scan(step, (h0_re, h0_im), (bu_re, bu_im))
    y = y.reshape(T, B, SSM_GROUPS, SSM_GROUP).transpose(1, 0, 2, 3) + d_skip * uf
    return y.reshape(B, T, SSM_WIDTH), hr, hi


def _block(h, p_l, pos, lw, sample, h0_re, h0_im, conv_prefix):
    (g_attn, w_in, g_q, g_kc, g_ks, g_kw, wk1, pek, wk2, wv1, pev, wv2,
     a_re, a_im, log_dt, b_re, b_im, c_re, c_im, d_skip,
     w_a, w_glu1, w_glu2, w_o, g_ffn, w_up, conv_w, conv_b, w_down,
     g_ple, w_ple_gate, w_ple) = lw
    B, T = p_l.shape[:2]
    N = B * T
    qs, rows_cmp, rows_sel, rows_win, gl, u, gab, *attn = _in_proj_pallas(h, w_in, g_attn, g_q, g_kc, g_ks, g_kw,
                                                                           pos, T)
    gl = gl[:, :3 * N_HEADS]
    cmpw = (wk1, pek, wk2, wv1, pev, wv2)
    as_rows = lambda a: a.reshape(B, T, 2, N_KV_HEADS, HEAD_DIM)
    if sample is None:
        kcmp, vcmp = _compress_pallas(rows_cmp.reshape(B, T, 2 * KV_W), cmpw)
        o = _nsa_prompt_pallas(qs.reshape(B, T, Q_W), kcmp, vcmp, *attn).reshape(N, Q_W)
        n_keep = min(WINDOW, T)
        rows = (as_rows(rows_cmp), as_rows(rows_sel), as_rows(rows_win)[:, T - n_keep:])
    else:
        assert T == 1
        layer, cmp_t, sel_t, win_t, page_table = sample
        o, wnew = _nsa_sample_pallas(layer, qs, gl, rows_sel[:, :KV_W], rows_sel[:, KV_W:],
                                     rows_win[:, :KV_W], rows_win[:, KV_W:], cmpw, cmp_t, sel_t, win_t, page_table)
        o = o.reshape(N, Q_W)
        wb = wnew.shape[-1]
        wnew = wnew.reshape(B, 2, N_KV_HEADS, HEAD_DIM, wb).transpose(0, 4, 1, 2, 3)
        rows = (as_rows(rows_cmp), as_rows(rows_sel), wnew)
    y, hr, hi = _ssm_pallas(u.reshape(T * B, SSM_WIDTH), B, T, h0_re, h0_im,
                            _ssm_params(a_re, a_im, log_dt, b_re, b_im, c_re, c_im, d_skip))
    h, conv_rows = _mix_ffn_pallas(h, o, y.reshape(u.shape), gab, p_l.reshape(N, -1), conv_prefix,
                                   (w_a, w_glu1, w_glu2, w_o, g_ffn, w_up, conv_w, conv_b, w_down,
                                    g_ple, w_ple_gate, w_ple), T)
    return h, rows, hr, hi, conv_rows


def kernel(x_prompt, x_sample, cache_cmp, cache_sel, cache_win, state_ssm_re, state_ssm_im, state_conv, page_table, p_prompt, p_sample, g_attn, w_in, g_q, g_kc, g_ks, g_kw, cmp_wk1, cmp_pek, cmp_wk2, cmp_wv1, cmp_pev, cmp_wv2, ssm_a_re, ssm_a_im, ssm_log_dt, ssm_b_re, ssm_b_im, ssm_c_re, ssm_c_im, ssm_d, w_a, w_glu1, w_glu2, w_o, g_ffn, w_up, conv_w, conv_b, w_down, g_ple, w_ple_gate, w_ple):
    Bp, Tp = x_prompt.shape[:2]
    Ts = x_sample.shape[1]
    depth = w_in.shape[0]
    past = page_table.shape[1] * PAGE_SIZE
    pos_p = jnp.arange(Tp)
    pos_s = past + jnp.arange(Ts)
    zeros_h = jnp.zeros((Bp, SSM_GROUPS, SSM_STATE), x_prompt.dtype)
    layer_w = (g_attn, _pad_w_in(w_in), g_q, g_kc, g_ks, g_kw, cmp_wk1, cmp_pek, cmp_wk2, cmp_wv1, cmp_pev, cmp_wv2,
               ssm_a_re, ssm_a_im, ssm_log_dt, ssm_b_re, ssm_b_im, ssm_c_re, ssm_c_im, ssm_d,
               w_a, w_glu1, w_glu2, w_o, g_ffn, w_up, conv_w, conv_b, w_down, g_ple, w_ple_gate, w_ple)
    cmp_t, sel_t, win_t = (_cache_rows_on_lanes(c) for c in (cache_cmp, cache_sel, cache_win))
    st = [[] for _ in range(12)]
    hp, hs = x_prompt.reshape(Bp * Tp, D_MODEL), x_sample.reshape(-1, D_MODEL)
    for i in range(depth):
        lw = [w[i] for w in layer_w]
        hp, rows, hr, hi, cv = _block(hp, p_prompt[i], pos_p, lw, None, zeros_h, zeros_h, None)
        for j, a in enumerate(list(rows) + [hr, hi, cv]):
            st[j].append(a)
        hs, rows, hr, hi, cv = _block(hs, p_sample[i], pos_s, lw, (i, cmp_t, sel_t, win_t, page_table),
                                      state_ssm_re[i], state_ssm_im[i], state_conv[i])
        for j, a in enumerate(list(rows) + [hr, hi, cv]):
            st[6 + j].append(a)
    return (hp.reshape(x_prompt.shape), hs.reshape(x_sample.shape)) + tuple(jnp.stack(s) for s in st)
```

```python
import functools
import math

import numpy as np
import jax
import jax.numpy as jnp
from jax import lax
from jax.experimental import pallas as pl
from jax.experimental.pallas import tpu as pltpu

D_MODEL = 1024
N_HEADS = 8
N_KV_HEADS = 2
HEAD_DIM = 64
GROUP = N_HEADS // N_KV_HEADS
Q_W = N_HEADS * HEAD_DIM
KV_W = N_KV_HEADS * HEAD_DIM
CMP_BLOCK = 32
CMP_STRIDE = 16
SEL_BLOCK = 64
N_SEL = 8
WINDOW = 512
PAGE_SIZE = 128
ROPE_THETA = 10000.0
SSM_WIDTH = D_MODEL // 2
SSM_GROUP = 16
SSM_GROUPS = SSM_WIDTH // SSM_GROUP
SSM_STATE = 64
D_FF = 11 * D_MODEL // 4
CONV_W = 3
EPS = 1e-6
NEG_INF = -1e30
FORCE_SCORE = 1e9
SCALE = HEAD_DIM ** -0.5

VMEM_LIMIT_BYTES = 56 * 1024 * 1024


def _pick_tile(n, cands):
    for c in cands:
        if n % c == 0:
            return c
    return n


def _rms(x, g):
    return x * lax.rsqrt(jnp.mean(x * x, axis=-1, keepdims=True) + EPS) * g


def _bdot(a, b):
    return jnp.dot(a.astype(jnp.bfloat16), b, preferred_element_type=jnp.float32)


GL_PAD = 128
_IN_WIDTHS = (Q_W, 6 * KV_W, GL_PAD, SSM_WIDTH, 2 * D_MODEL)
_IN_OFFS = tuple(int(v) for v in np.cumsum((0,) + _IN_WIDTHS))
ROW_TILE = 256
IN_TILE = 512


def _in_proj_kernel(h_ref, g_ref, w_ref, gq_ref, gk_ref, ones_ref, cos_ref, sin_ref,
                    q_ref, cmp_ref, sel_ref, win_ref, gl_ref, u_ref, gab_ref, *attn_refs):
    xn = _rms(h_ref[...], g_ref[...]).astype(jnp.bfloat16)
    seg = lambda s: jnp.dot(xn, w_ref[:, _IN_OFFS[s]:_IN_OFFS[s + 1]], preferred_element_type=jnp.float32)
    cos, sin = cos_ref[...], sin_ref[...]
    ones = ones_ref[...]

    def head_norm(x, gain):
        x2 = x * x
        hi = x2.astype(jnp.bfloat16)
        lo = (x2 - hi.astype(jnp.float32)).astype(jnp.bfloat16)
        ss = (jnp.dot(hi, ones, preferred_element_type=jnp.float32)
              + jnp.dot(lo, ones, preferred_element_type=jnp.float32))
        return x * lax.rsqrt(ss * (1.0 / HEAD_DIM) + EPS) * gain

    zq = seg(0)
    for c in range(Q_W // KV_W):
        qn = _rope_lanes(head_norm(zq[:, c * KV_W:(c + 1) * KV_W], gq_ref[...]), cos, sin)
        q_ref[:, c * KV_W:(c + 1) * KV_W] = (qn * SCALE).astype(q_ref.dtype)
    zkv = seg(1)
    part = lambda c: zkv[:, c * KV_W:(c + 1) * KV_W]
    cmp_ref[:, :KV_W] = head_norm(part(0), gk_ref[0:1])
    cmp_ref[:, KV_W:] = part(1)
    sel_ref[:, :KV_W] = _rope_lanes(head_norm(part(2), gk_ref[1:2]), cos, sin)
    sel_ref[:, KV_W:] = part(3)
    win_ref[:, :KV_W] = _rope_lanes(head_norm(part(4), gk_ref[2:3]), cos, sin)
    win_ref[:, KV_W:] = part(5)
    gl = seg(2)
    gl_ref[...] = gl
    u_ref[...] = seg(3)
    gab_ref[...] = seg(4)
    if attn_refs:
        ksb_ref, vst_ref, kwb_ref, vwt_ref, glt_ref = attn_refs
        tq = vst_ref.shape[-1]
        for k_ref, v_ref, src in ((ksb_ref, vst_ref, sel_ref), (kwb_ref, vwt_ref, win_ref)):
            kb = src[:, :KV_W].astype(jnp.bfloat16)
            for hd in range(N_KV_HEADS):
                k_ref[0, hd] = kb[:, hd * HEAD_DIM:(hd + 1) * HEAD_DIM]
            for r in range(src.shape[0] // tq):
                vt = src[r * tq:(r + 1) * tq, KV_W:].T.astype(jnp.bfloat16)
                for hd in range(N_KV_HEADS):
                    v_ref[0, hd, r] = vt[hd * HEAD_DIM:(hd + 1) * HEAD_DIM]
        glt = gl.T
        for hd in range(N_KV_HEADS):
            glt_ref[0, hd] = glt[hd * 3 * GROUP:(hd + 1) * 3 * GROUP]


def _pad_w_in(w_in):
    a = Q_W + 6 * KV_W + 3 * N_HEADS
    pad = jnp.zeros(w_in.shape[:-1] + (GL_PAD - 3 * N_HEADS,), w_in.dtype)
    return jnp.concatenate([w_in[..., :a], pad, w_in[..., a:]], axis=-1).astype(jnp.bfloat16)


def _in_proj_pallas(h2d, w_in_p, g_attn, g_q, g_kc, g_ks, g_kw, pos, seq_len):
    N = h2d.shape[0]
    T = seq_len
    B = N // T
    tm = _pick_tile(N, (IN_TILE, ROW_TILE, 128))
    nt = max(T // tm, 1)
    assert T == 1 or T % tm == 0
    cos, sin = _rope_tables(pos, N_KV_HEADS)
    if T == 1:
        cos, sin = (jnp.broadcast_to(t, (tm, KV_W)) for t in (cos, sin))
    tile2 = lambda g: jnp.tile(g.reshape(1, HEAD_DIM), (1, N_KV_HEADS))
    gk = jnp.concatenate([tile2(g_kc), tile2(g_ks), tile2(g_kw), jnp.zeros((5, KV_W), jnp.float32)], axis=0)
    ones = jnp.asarray(np.kron(np.eye(N_KV_HEADS), np.ones((HEAD_DIM, HEAD_DIM))), jnp.bfloat16)
    row = lambda w: pl.BlockSpec((tm, w), lambda i: (i, 0))
    const = lambda a: pl.BlockSpec(a.shape, lambda i: (0,) * a.ndim, pipeline_mode=pl.Buffered(1))
    tab = pl.BlockSpec((tm, KV_W), lambda i: (i % nt, 0))
    if T == 1:
        u_shape, u_spec = (N, SSM_WIDTH), row(SSM_WIDTH)
    else:
        u_shape, u_spec = (T, B * SSM_WIDTH), pl.BlockSpec((tm, SSM_WIDTH), lambda i: (i % nt, i // nt))
    consts = (g_attn.reshape(1, D_MODEL), w_in_p, tile2(g_q), gk, ones)
    f32, bf = jnp.float32, jnp.bfloat16
    out_shape = ((jax.ShapeDtypeStruct((N, Q_W), bf),) + (jax.ShapeDtypeStruct((N, 2 * KV_W), f32),) * 3
                 + (jax.ShapeDtypeStruct((N, GL_PAD), f32), jax.ShapeDtypeStruct(u_shape, f32),
                    jax.ShapeDtypeStruct((N, 2 * D_MODEL), f32)))
    out_specs = (row(Q_W), row(2 * KV_W), row(2 * KV_W), row(2 * KV_W), row(GL_PAD), u_spec, row(2 * D_MODEL))
    if T > 1:
        tq = ATT_TQ
        assert tm % tq == 0
        k_shape = jax.ShapeDtypeStruct((B, N_KV_HEADS, T, HEAD_DIM), bf)
        k_spec = pl.BlockSpec((1, N_KV_HEADS, tm, HEAD_DIM), lambda i: (i // nt, 0, i % nt, 0))
        v_shape = jax.ShapeDtypeStruct((B, N_KV_HEADS, T // tq, HEAD_DIM, tq), bf)
        v_spec = pl.BlockSpec((1, N_KV_HEADS, tm // tq, HEAD_DIM, tq), lambda i: (i // nt, 0, i % nt, 0, 0))
        g_shape = jax.ShapeDtypeStruct((B, N_KV_HEADS, 3 * GROUP, T), f32)
        g_spec = pl.BlockSpec((1, N_KV_HEADS, 3 * GROUP, tm), lambda i: (i // nt, 0, 0, i % nt))
        out_shape += (k_shape, v_shape, k_shape, v_shape, g_shape)
        out_specs += (k_spec, v_spec, k_spec, v_spec, g_spec)
    return pl.pallas_call(
        _in_proj_kernel,
        out_shape=out_shape,
        grid=(N // tm,),
        in_specs=[row(D_MODEL)] + [const(a) for a in consts] + [tab, tab],
        out_specs=out_specs,
        compiler_params=pltpu.CompilerParams(dimension_semantics=("parallel",), vmem_limit_bytes=VMEM_LIMIT_BYTES),
        name="in_proj",
    )(h2d, *consts, cos, sin)


FF_CHUNK = D_FF // 2
assert FF_CHUNK % 128 == 0


def _mix_ffn_kernel(h_ref, o_ref, y_ref, gab_ref, p_ref, pre0_ref, pre1_ref,
                    wa_ref, wg1_ref, wg2_ref, wo_ref, gffn_ref, wup_ref, cw_ref, cb_ref, wdn_ref,
                    gple_ref, wpg_ref, wpl_ref, hout_ref, cs0_ref, cs1_ref, carry_s, *, seq_tiles):
    tm = h_ref.shape[0]
    a_out = jnp.dot(o_ref[...], wa_ref[...], preferred_element_type=jnp.float32)
    yg = jax.nn.gelu(y_ref[...]).astype(jnp.bfloat16)
    b_out = (jnp.dot(yg, wg1_ref[...], preferred_element_type=jnp.float32)
             * jax.nn.sigmoid(jnp.dot(yg, wg2_ref[...], preferred_element_type=jnp.float32)))
    mixed = (jax.nn.sigmoid(gab_ref[:, :D_MODEL]) * a_out + jax.nn.sigmoid(gab_ref[:, D_MODEL:]) * b_out)
    h1 = h_ref[...] + _bdot(mixed, wo_ref[...])

    xn = _rms(h1, gffn_ref[...]).astype(jnp.bfloat16)
    if seq_tiles:
        @pl.when(pl.program_id(0) % seq_tiles == 0)
        def _():
            carry_s[...] = jnp.zeros_like(carry_s)
        row = lax.broadcasted_iota(jnp.int32, (tm, 1), 0)
    ffn = jnp.zeros((tm, D_MODEL), jnp.float32)
    for c in range(D_FF // FF_CHUNK):
        sl = slice(c * FF_CHUNK, (c + 1) * FF_CHUNK)
        gp = jnp.dot(xn, wup_ref[:, sl], preferred_element_type=jnp.float32)
        val = jnp.dot(xn, wup_ref[:, D_FF + c * FF_CHUNK:D_FF + (c + 1) * FF_CHUNK], preferred_element_type=jnp.float32)
        if seq_tiles:
            old1, old2 = carry_s[7:8, sl], carry_s[6:7, sl]
            prev1 = jnp.where(row == 0, old1, pltpu.roll(gp, 1, 0))
            prev2 = jnp.where(row == 0, old2, jnp.where(row == 1, old1, pltpu.roll(gp, 2, 0)))
            carry_s[:, sl] = gp[tm - 8:, :]
            cs0_ref[0, :, sl] = gp[tm - 2:tm - 1, :]
            cs1_ref[0, :, sl] = gp[tm - 1:tm, :]
        else:
            prev2, prev1 = pre0_ref[:, sl], pre1_ref[:, sl]
            cs0_ref[:, sl] = prev1
            cs1_ref[:, sl] = gp
        conv = cb_ref[:, sl] + cw_ref[0:1, sl] * prev2 + cw_ref[1:2, sl] * prev1 + cw_ref[2:3, sl] * gp
        ffn = ffn + _bdot(jax.nn.gelu(conv) * val, wdn_ref[sl, :])
    h2 = h1 + ffn

    gate = jax.nn.sigmoid(_bdot(_rms(h2, gple_ref[...]), wpg_ref[...]))
    hout_ref[...] = h2 + gate * _bdot(p_ref[...], wpl_ref[...])


def _mix_ffn_pallas(h2d, o2d, y, gab, p2d, prefix, w, seq_len):
    (w_a, w_glu1, w_glu2, w_o, g_ffn, w_up, conv_w, conv_b, w_down, g_ple, w_ple_gate, w_ple) = w
    N = h2d.shape[0]
    T = seq_len
    B = N // T
    tm = _pick_tile(N, (ROW_TILE, 128))
    nt = max(T // tm, 1)
    seq = prefix is None
    assert (seq and T % tm == 0 and tm >= 8) or (not seq and T == 1)
    row = lambda wd: pl.BlockSpec((tm, wd), lambda i: (i, 0))
    const = lambda a: pl.BlockSpec(a.shape, lambda i: (0,) * a.ndim, pipeline_mode=pl.Buffered(1))
    f32 = jnp.float32
    if seq:
        y_spec = pl.BlockSpec((tm, SSM_WIDTH), lambda i: (i % nt, i // nt))
        pre = (jnp.zeros((8, D_FF), f32),) * 2
        pre_spec = const(pre[0])
        cs_shape = jax.ShapeDtypeStruct((B, 1, D_FF), f32)
        cs_spec = pl.BlockSpec((1, 1, D_FF), lambda i: (i // nt, 0, 0))
    else:
        y_spec = row(SSM_WIDTH)
        pre = (prefix[:, 0], prefix[:, 1])
        pre_spec = row(D_FF)
        cs_shape = jax.ShapeDtypeStruct((N, D_FF), f32)
        cs_spec = row(D_FF)
    bf = lambda a: a.astype(jnp.bfloat16)
    vec = lambda a: a.reshape(1, -1)
    cw8 = jnp.concatenate([conv_w, jnp.zeros((8 - CONV_W, D_FF), f32)], axis=0)
    consts = (bf(w_a), bf(w_glu1), bf(w_glu2), bf(w_o), vec(g_ffn), bf(w_up), cw8, vec(conv_b), bf(w_down),
              vec(g_ple), bf(w_ple_gate), bf(w_ple))
    hout, cs0, cs1 = pl.pallas_call(
        functools.partial(_mix_ffn_kernel, seq_tiles=nt if seq else 0),
        out_shape=(jax.ShapeDtypeStruct((N, D_MODEL), f32), cs_shape, cs_shape),
        grid=(N // tm,),
        in_specs=[row(D_MODEL), row(Q_W), y_spec, row(2 * D_MODEL), row(p2d.shape[1]), pre_spec, pre_spec]
        + [const(a) for a in consts],
        out_specs=(row(D_MODEL), cs_spec, cs_spec),
        scratch_shapes=[pltpu.VMEM((8, D_FF), f32)],
        compiler_params=pltpu.CompilerParams(dimension_semantics=("arbitrary",), vmem_limit_bytes=VMEM_LIMIT_BYTES),
        name="mix_ffn",
    )(h2d, o2d, y, gab, p2d, *pre, *consts)
    return hout, jnp.stack([cs0.reshape(B, D_FF), cs1.reshape(B, D_FF)], axis=1)


HALF_ROWS = CMP_BLOCK // CMP_STRIDE
assert HALF_ROWS == 2


def _rope_lanes(x, cos, sin_signed):
    w = x.shape[-1]
    half = HEAD_DIM // 2
    lane = lax.broadcasted_iota(jnp.int32, x.shape, x.ndim - 1)
    first = (lane % HEAD_DIM) < half
    partner = jnp.where(first, pltpu.roll(x, w - half, x.ndim - 1), pltpu.roll(x, half, x.ndim - 1))
    return x * cos + partner * sin_signed


def _compress_rows(xcat, w_ref, pe_ref, w1_ref, w2_ref, n_half):
    acc = jnp.dot(xcat, w_ref[...], preferred_element_type=jnp.float32)
    pa = acc[:, :256]
    pb = pltpu.roll(acc[:, 256:], n_half - 1, 0)
    bias = jnp.dot(pe_ref[...].astype(jnp.bfloat16), w1_ref[...], preferred_element_type=jnp.float32)[0:1]
    bias2 = jnp.concatenate([bias, bias], axis=1)
    hdn = jax.nn.gelu(pa + pb + bias2)
    return jnp.dot(hdn.astype(jnp.bfloat16), w2_ref[...], preferred_element_type=jnp.float32)


def _compress_kernel(xk_ref, xv_ref, wk_ref, wv_ref, pek_ref, pev_ref, w1k_ref, w1v_ref, w2k_ref, w2v_ref,
                     cos_ref, sin_ref, ko_ref, vo_ref, *, n_half, n_cmp):
    row = lax.broadcasted_iota(jnp.int32, (n_half, KV_W), 0)
    gather = lambda x_ref: jnp.concatenate(
        [x_ref[0, pl.ds(j, n_half, stride=CMP_STRIDE), :].astype(jnp.bfloat16) for j in range(CMP_STRIDE)], axis=1)
    k = _compress_rows(gather(xk_ref), wk_ref, pek_ref, w1k_ref, w2k_ref, n_half)
    k = _rope_lanes(k, cos_ref[...], sin_ref[...])
    v = _compress_rows(gather(xv_ref), wv_ref, pev_ref, w1v_ref, w2v_ref, n_half)
    ko_ref[0] = jnp.where(row < n_cmp, k, 0.0).astype(ko_ref.dtype)
    vo_ref[0] = jnp.where(row < n_cmp, v, 0.0).astype(vo_ref.dtype)


def _blockdiag2(w):
    z = jnp.zeros_like(w)
    return jnp.concatenate([jnp.concatenate([w, z], axis=-1), jnp.concatenate([z, w], axis=-1)], axis=-2)


def _compress_weights(w1, pe, w2):
    bd = _blockdiag2(w1)
    wcat = jnp.concatenate([bd[:CMP_STRIDE], bd[CMP_STRIDE:]], axis=-1)
    wcat = wcat.reshape(CMP_STRIDE * KV_W, -1).astype(jnp.bfloat16)
    pe_flat = jnp.broadcast_to(pe.reshape(1, -1), (8, pe.size))
    w1_flat = w1.reshape(-1, w1.shape[-1]).astype(jnp.bfloat16)
    w2bd = _blockdiag2(w2).astype(jnp.bfloat16)
    return wcat, pe_flat, w1_flat, w2bd


def _rope_tables(pos, reps):
    half = HEAD_DIM // 2
    inv = jnp.float32(ROPE_THETA) ** (-jnp.arange(half, dtype=jnp.float32) / half)
    ang = pos.astype(jnp.float32)[:, None] * inv[None, :]
    cos = jnp.cos(ang)
    sin = jnp.sin(ang)
    return (jnp.tile(jnp.concatenate([cos, cos], axis=-1), (1, reps)),
            jnp.tile(jnp.concatenate([-sin, sin], axis=-1), (1, reps)))


def _compress_pallas(rows, cmpw):
    wk1, pek, wk2, wv1, pev, wv2 = cmpw
    B, L, _ = rows.shape
    n_half = L // CMP_STRIDE
    n_cmp = n_half - 1
    wk, pekf, w1k, w2k = _compress_weights(wk1, pek, wk2)
    wv, pevf, w1v, w2v = _compress_weights(wv1, pev, wv2)
    end = jnp.arange(n_half) * CMP_STRIDE + CMP_BLOCK - 1
    cos, sin = _rope_tables(end, N_KV_HEADS)
    full = lambda a: pl.BlockSpec(a.shape, lambda b: (0,) * a.ndim)
    consts = (wk, wv, pekf, pevf, w1k, w1v, w2k, w2v, cos, sin)
    return pl.pallas_call(
        functools.partial(_compress_kernel, n_half=n_half, n_cmp=n_cmp),
        out_shape=(jax.ShapeDtypeStruct((B, n_half, KV_W), jnp.bfloat16),) * 2,
        grid=(B,),
        in_specs=[pl.BlockSpec((1, L, KV_W), lambda b: (b, 0, 0)), pl.BlockSpec((1, L, KV_W), lambda b: (b, 0, 1))]
        + [full(a) for a in consts],
        out_specs=(pl.BlockSpec((1, n_half, KV_W), lambda b: (b, 0, 0)),) * 2,
        compiler_params=pltpu.CompilerParams(dimension_semantics=("parallel",), vmem_limit_bytes=VMEM_LIMIT_BYTES),
        name="compress",
    )(rows, rows, *consts)


ATT_TQ = 256
SEL_KC = 512
BIG_NEG = -3.0e38


def _softmax_rows(s, mask):
    s = jnp.where(mask, s, NEG_INF)
    m = jnp.max(s, axis=-1, keepdims=True)
    e = jnp.where(mask, jnp.exp(s - m), 0.0)
    return e / jnp.maximum(jnp.sum(e, axis=-1, keepdims=True), 1e-30)


def _select_blocks(imp, tpos, n_sb):
    rows = imp.shape[0]
    jl = lax.broadcasted_iota(jnp.int32, (rows, 128), 1)
    forced = (jl == 0) | (jl == (tpos >> 6))
    imp = jnp.where(forced, FORCE_SCORE, imp)
    imp = jnp.where(jl * SEL_BLOCK <= tpos, imp, NEG_INF)
    imp = jnp.where(jl < n_sb, imp, BIG_NEG)
    beaten_by = jnp.zeros((rows, 128), jnp.float32)
    for i in range(n_sb):
        col = imp[:, i:i + 1]
        beaten_by = beaten_by + jnp.where((col > imp) | ((col == imp) & (i < jl)), 1.0, 0.0)
    return jnp.where((beaten_by < N_SEL) & (imp > 0.5 * NEG_INF), 1.0, 0.0)


def _select_blocks_t(imp, tpos, n_sb):
    nj, tq = imp.shape
    jr = lax.broadcasted_iota(jnp.int32, (nj, tq), 0)
    jf = jr.astype(jnp.float32)
    forced = (jr == 0) | (jr == (tpos >> 6))
    imp = jnp.where(forced, FORCE_SCORE, imp)
    imp = jnp.where(jr * SEL_BLOCK <= tpos, imp, NEG_INF)
    imp = jnp.where(jr < n_sb, imp, BIG_NEG)
    sel = jnp.zeros((nj, tq), jnp.float32)
    for _ in range(N_SEL):
        m = jnp.max(imp, axis=0, keepdims=True)
        first = jnp.min(jnp.where(imp == m, jf, 1e9), axis=0, keepdims=True)
        hit = jf == first
        sel = jnp.where(hit & (m > 0.5 * NEG_INF), 1.0, sel)
        imp = jnp.where(hit, BIG_NEG, imp)
    return sel


def _nsa_prompt_kernel(q_ref, kcmp_ref, vcmpt_ref, ks_ref, vst_ref, kw_ref, vwt_ref, glt_ref, ovt_ref, et_ref, o_ref,
                       *, tq, n_cmp, n_sb):
    i = pl.program_id(2)
    t0 = i * tq
    bf = jnp.bfloat16
    qf = q_ref[0]
    q4 = jnp.concatenate([qf[:, g * HEAD_DIM:(g + 1) * HEAD_DIM] for g in range(GROUP)], axis=0)
    lanes4 = lambda x: jnp.concatenate([x] * GROUP, axis=1)
    tq_pos = t0 + lax.broadcasted_iota(jnp.int32, (1, tq), 1)
    tpos = lanes4(tq_pos)

    nr = lax.broadcasted_iota(jnp.int32, (128, 1), 0)
    maskc = ((nr * CMP_STRIDE + (CMP_BLOCK - 1)) <= tpos) & (nr < n_cmp)
    sc = jnp.where(maskc, lax.dot_general(kcmp_ref[0, 0], q4, _NT, preferred_element_type=jnp.float32), NEG_INF)
    ec = jnp.where(maskc, jnp.exp(sc - jnp.max(sc, axis=0, keepdims=True)), 0.0)
    pc = ec / jnp.maximum(jnp.sum(ec, axis=0, keepdims=True), 1e-30)
    o_cmp = jnp.dot(vcmpt_ref[0, 0], pc.astype(bf), preferred_element_type=jnp.float32)
    psum = pc[:, :tq] + pc[:, tq:2 * tq] + pc[:, 2 * tq:3 * tq] + pc[:, 3 * tq:]
    p_hi = psum.astype(bf)
    p_lo = (psum - p_hi.astype(jnp.float32)).astype(bf)
    imp = (jnp.dot(ovt_ref[...], p_hi, preferred_element_type=jnp.float32)
           + jnp.dot(ovt_ref[...], p_lo, preferred_element_type=jnp.float32))
    nj = -(-n_sb // 8) * 8
    sel = _select_blocks_t(imp[:nj], tq_pos, n_sb)
    sel = jnp.concatenate([sel, jnp.zeros((128 - nj, tq), jnp.float32)], axis=0).astype(bf)

    per_kc = SEL_KC // tq

    def sel_step(c, carry):
        m, l, acc = carry
        k0 = pl.multiple_of(c * SEL_KC, SEL_KC)
        kpos = k0 + lax.broadcasted_iota(jnp.int32, (SEL_KC, 1), 0)
        picked = jnp.dot(et_ref[pl.ds(k0, SEL_KC), :], sel, preferred_element_type=jnp.float32)
        mask = lanes4((picked > 0.5) & (kpos <= tq_pos))
        s = lax.dot_general(ks_ref[0, 0, pl.ds(k0, SEL_KC), :], q4, _NT, preferred_element_type=jnp.float32)
        s = jnp.where(mask, s, NEG_INF)
        m_new = jnp.maximum(m, jnp.max(s, axis=0, keepdims=True))
        alpha = jnp.exp(m - m_new)
        p = jnp.exp(s - m_new)
        l = alpha * l + jnp.sum(p, axis=0, keepdims=True)
        vt = jnp.concatenate([vst_ref[0, 0, c * per_kc + r] for r in range(per_kc)], axis=1)
        acc = alpha * acc + jnp.dot(vt, p.astype(bf), preferred_element_type=jnp.float32)
        return m_new, l, acc

    nq = GROUP * tq
    init = (jnp.full((1, nq), NEG_INF, jnp.float32), jnp.zeros((1, nq), jnp.float32),
            jnp.zeros((HEAD_DIM, nq), jnp.float32))
    n_kc = (t0 + tq + SEL_KC - 1) // SEL_KC
    _, l_s, acc_s = lax.fori_loop(0, n_kc, sel_step, init)
    o_sel = acc_s / jnp.maximum(l_s, 1e-30)

    n_wc = WINDOW // tq + 1
    c0 = jnp.maximum(i - WINDOW // tq, 0)
    w0 = pl.multiple_of(c0 * tq, tq)
    d = tpos - (w0 + lax.broadcasted_iota(jnp.int32, (n_wc * tq, 1), 0))
    sw = lax.dot_general(kw_ref[0, 0, pl.ds(w0, n_wc * tq), :], q4, _NT, preferred_element_type=jnp.float32)
    sw = jnp.where((d >= 0) & (d < WINDOW), sw, NEG_INF)
    ew = jnp.exp(sw - jnp.max(sw, axis=0, keepdims=True))
    vwt = jnp.concatenate([vwt_ref[0, 0, c0 + r] for r in range(n_wc)], axis=1)
    o_win = (jnp.dot(vwt, ew.astype(bf), preferred_element_type=jnp.float32)
             / jnp.maximum(jnp.sum(ew, axis=0, keepdims=True), 1e-30))

    gate = jax.nn.sigmoid(glt_ref[0, 0])
    for g in range(GROUP):
        sl = slice(g * tq, (g + 1) * tq)
        ot = (gate[3 * g:3 * g + 1] * o_cmp[:, sl] + gate[3 * g + 1:3 * g + 2] * o_sel[:, sl]
              + gate[3 * g + 2:3 * g + 3] * o_win[:, sl])
        o_ref[0, :, g * HEAD_DIM:(g + 1) * HEAD_DIM] = ot.T.astype(o_ref.dtype)


def _overlap_matrix(n_cmp, n_sb):
    start = np.arange(128) * CMP_STRIDE
    end = start + CMP_BLOCK - 1
    sb = np.arange(128) * SEL_BLOCK
    ov = (start[:, None] < sb[None, :] + SEL_BLOCK) & (end[:, None] >= sb[None, :])
    ov &= (np.arange(128)[:, None] < n_cmp) & (np.arange(128)[None, :] < n_sb)
    return jnp.asarray(ov, jnp.bfloat16)


def _heads_major(x):
    B, T, W = x.shape
    return x.reshape(B, T, N_KV_HEADS, W // N_KV_HEADS).transpose(0, 2, 1, 3)


def _nsa_prompt_pallas(q, kcmp, vcmp, ks, vs_t, kw, vw_t, gl_t):
    B, T, _ = q.shape
    tq = ATT_TQ
    n_cmp = (T - CMP_BLOCK) // CMP_STRIDE + 1
    n_sb = -(-T // SEL_BLOCK)
    assert T % SEL_KC == 0 and T >= WINDOW + tq and kcmp.shape[1] <= 128 and n_sb <= 128
    kcmp, vcmp = (jnp.pad(a, ((0, 0), (0, 128 - a.shape[1]), (0, 0))) for a in (kcmp, vcmp))
    vcmp_t = vcmp.reshape(B, 128, N_KV_HEADS, HEAD_DIM).transpose(0, 2, 3, 1)
    et = jnp.asarray((np.arange(T)[:, None] // SEL_BLOCK) == np.arange(128)[None, :], jnp.bfloat16)
    k_spec = pl.BlockSpec((1, 1, T, HEAD_DIM), lambda b, k, i: (b, k, 0, 0))
    v_spec = pl.BlockSpec((1, 1, T // tq, HEAD_DIM, tq), lambda b, k, i: (b, k, 0, 0, 0))
    return pl.pallas_call(
        functools.partial(_nsa_prompt_kernel, tq=tq, n_cmp=n_cmp, n_sb=n_sb),
        out_shape=jax.ShapeDtypeStruct((B, T, Q_W), jnp.bfloat16),
        grid=(B, N_KV_HEADS, T // tq),
        in_specs=[pl.BlockSpec((1, tq, GROUP * HEAD_DIM), lambda b, k, i: (b, i, k)),
                  pl.BlockSpec((1, 1, 128, HEAD_DIM), lambda b, k, i: (b, k, 0, 0)),
                  pl.BlockSpec((1, 1, HEAD_DIM, 128), lambda b, k, i: (b, k, 0, 0)),
                  k_spec, v_spec, k_spec, v_spec,
                  pl.BlockSpec((1, 1, 3 * GROUP, tq), lambda b, k, i: (b, k, 0, i)),
                  pl.BlockSpec((128, 128), lambda b, k, i: (0, 0)),
                  pl.BlockSpec((T, 128), lambda b, k, i: (0, 0))],
        out_specs=pl.BlockSpec((1, tq, GROUP * HEAD_DIM), lambda b, k, i: (b, i, k)),
        compiler_params=pltpu.CompilerParams(
            dimension_semantics=("parallel", "parallel", "arbitrary"), vmem_limit_bytes=VMEM_LIMIT_BYTES),
        name="nsa_prompt",
    )(q, _heads_major(kcmp), vcmp_t, ks, vs_t, kw, vw_t, gl_t, _overlap_matrix(n_cmp, n_sb).T, et)


_NT = (((1,), (1,)), ((), ()))
SAMPLE_NB = 2


def _decode_attend(s, mask, s_new, mask_new, pv_fn, v_new):
    sm = jnp.where(mask, s, NEG_INF)
    sn = jnp.where(mask_new, s_new, NEG_INF)
    m = jnp.maximum(jnp.max(sm, axis=1, keepdims=True), sn)
    e = jnp.where(mask, jnp.exp(sm - m), 0.0)
    en = jnp.where(mask_new, jnp.exp(sn - m), 0.0)
    l = jnp.sum(e, axis=1, keepdims=True) + en
    acc = pv_fn(e.astype(jnp.bfloat16)) + (en.astype(jnp.bfloat16).astype(jnp.float32)
                                            * v_new.astype(jnp.bfloat16).astype(jnp.float32))
    return acc / jnp.maximum(l, 1e-30)


def _nsa_sample_kernel(pt_ref, *refs, n_pages, n_sb, nb):
    del pt_ref
    cmp_pages = refs[:nb * n_pages]
    sel_pages_all = refs[nb * n_pages:2 * nb * n_pages]
    (win_ref, q_ref, gl_ref, ksn_ref, vsn_ref, kwn_ref, vwn_ref, kwc_ref, vwc_ref,
     wk_ref, wv_ref, pek_ref, pev_ref, w1k_ref, w1v_ref, w2k_ref, w2v_ref, cos_ref, sin_ref, ov_ref, ex_ref,
     perm_ref, o_ref, wout_ref) = refs[2 * nb * n_pages:]
    past = n_pages * PAGE_SIZE
    n_half = past // CMP_STRIDE
    n_cmp = (past + 1 - CMP_BLOCK) // CMP_STRIDE + 1

    groups = PAGE_SIZE // CMP_STRIDE

    k_tiles, v_tiles = [], []
    for p in range(nb * n_pages):
        page = cmp_pages[p][0, 0].reshape(2 * KV_W, PAGE_SIZE).astype(jnp.bfloat16)
        z = lax.dot_general(perm_ref[...], page, _NT, preferred_element_type=jnp.float32)
        for tiles, z_kv in ((k_tiles, z[:, :KV_W]), (v_tiles, z[:, KV_W:])):
            tiles.append(jnp.concatenate([z_kv[j * groups:(j + 1) * groups] for j in range(CMP_STRIDE)], axis=1))
    k_cat = jnp.concatenate(k_tiles, axis=0).astype(jnp.bfloat16)
    v_cat = jnp.concatenate(v_tiles, axis=0).astype(jnp.bfloat16)

    rows = nb * n_half
    valid = (lax.broadcasted_iota(jnp.int32, (rows, KV_W), 0) % n_half) < n_cmp
    kcmp = _compress_rows(k_cat, wk_ref, pek_ref, w1k_ref, w2k_ref, rows)
    kcmp = jnp.where(valid, _rope_lanes(kcmp, cos_ref[...], sin_ref[...]), 0.0).astype(jnp.bfloat16)
    vcmp = _compress_rows(v_cat, wv_ref, pev_ref, w1v_ref, w2v_ref, rows)
    vcmp = jnp.where(valid, vcmp, 0.0).astype(jnp.bfloat16)
    for s in range(nb):
        _nsa_sample_one(s, kcmp[s * n_half:(s + 1) * n_half], vcmp[s * n_half:(s + 1) * n_half],
                        sel_pages_all[s * n_pages:(s + 1) * n_pages], win_ref, q_ref, gl_ref, ksn_ref, vsn_ref,
                        kwn_ref, vwn_ref, kwc_ref, vwc_ref, ov_ref, ex_ref, o_ref, wout_ref, n_pages, n_sb)


def _nsa_sample_one(s, kcmp, vcmp, sel_pages, win_ref, q_ref, gl_ref, ksn_ref, vsn_ref, kwn_ref, vwn_ref,
                    kwc_ref, vwc_ref, ov_ref, ex_ref, o_ref, wout_ref, n_pages, n_sb):
    past = n_pages * PAGE_SIZE
    qpos = past
    n_half = past // CMP_STRIDE
    n_cmp = (past + 1 - CMP_BLOCK) // CMP_STRIDE + 1
    wb = win_ref.shape[-1]
    bf = jnp.bfloat16

    row8 = lax.broadcasted_iota(jnp.int32, (8, KV_W), 0)
    lane8 = lax.broadcasted_iota(jnp.int32, (8, KV_W), 1)
    top1 = lax.broadcasted_iota(jnp.int32, (8, 1), 0) < GROUP
    q8 = q_ref[s].astype(jnp.float32)
    q2 = jnp.where((row8 < GROUP) == (lane8 < HEAD_DIM), jnp.concatenate([q8, q8], axis=1), 0.0).astype(bf)
    q2f = q2.astype(jnp.float32)

    def halves(x):
        return jnp.where(top1, x[:, :HEAD_DIM], x[:, HEAD_DIM:])

    def new_score(k_new):
        return jnp.sum(q2f * k_new.astype(bf).astype(jnp.float32), axis=1, keepdims=True)

    sc = lax.dot_general(q2, kcmp, _NT, preferred_element_type=jnp.float32)
    nl = lax.broadcasted_iota(jnp.int32, (8, n_half), 1)
    pc = _softmax_rows(sc, ((nl * CMP_STRIDE + (CMP_BLOCK - 1)) <= qpos) & (nl < n_cmp))
    o_cmp = halves(jnp.dot(pc.astype(bf), vcmp, preferred_element_type=jnp.float32))
    pk0 = jnp.sum(jnp.where(top1, pc, 0.0), axis=0, keepdims=True)
    pk1 = jnp.sum(jnp.where(top1, 0.0, pc), axis=0, keepdims=True)
    rown = lax.broadcasted_iota(jnp.int32, (8, n_half), 0)
    p2 = jnp.where(rown == 0, pk0, jnp.where(rown == 1, pk1, 0.0))
    p_hi = p2.astype(bf)
    p_lo = (p2 - p_hi.astype(jnp.float32)).astype(bf)
    imp = (jnp.dot(p_hi, ov_ref[...], preferred_element_type=jnp.float32)
           + jnp.dot(p_lo, ov_ref[...], preferred_element_type=jnp.float32))
    sel2 = _select_blocks(imp, jnp.full((8, 1), qpos, jnp.int32), n_sb)

    picked2 = jnp.dot(sel2.astype(bf), ex_ref[...], preferred_element_type=jnp.float32)
    mask_s = jnp.where(top1, picked2[0:1], picked2[1:2]) > 0.5
    seln = jnp.sum(jnp.where(lane8 == qpos // SEL_BLOCK, sel2, 0.0), axis=1, keepdims=True)
    mask_new = jnp.where(top1, seln[0:1], seln[1:2]) > 0.5
    kt = jnp.concatenate([sel_pages[p][0, 0, 0].astype(bf) for p in range(n_pages)], axis=1)
    vt = jnp.concatenate([sel_pages[p][0, 0, 1].astype(bf) for p in range(n_pages)], axis=1)
    s_s = jnp.dot(q2, kt, preferred_element_type=jnp.float32)
    pv_sel = lambda e: lax.dot_general(e, vt, _NT, preferred_element_type=jnp.float32)

    o_sel = halves(_decode_attend(s_s, mask_s, new_score(ksn_ref[s]), mask_new, pv_sel, vsn_ref[s]))

    s_w = jnp.dot(q2, win_ref[0, s, 0].astype(bf), preferred_element_type=jnp.float32)
    kpos = past - wb + lax.broadcasted_iota(jnp.int32, (8, wb), 1)
    mask_w = (qpos - kpos >= 0) & (qpos - kpos < WINDOW) & (kpos >= 0)
    pv_win = lambda e: lax.dot_general(e, win_ref[0, s, 1].astype(bf), _NT, preferred_element_type=jnp.float32)
    o_win = halves(_decode_attend(s_w, mask_w, new_score(kwn_ref[s]), jnp.full((8, 1), True), pv_win, vwn_ref[s]))

    gate = jax.nn.sigmoid(gl_ref[s])
    o_ref[s] = (gate[:, 0:1] * o_cmp + gate[:, 1:2] * o_sel + gate[:, 2:3] * o_win).astype(o_ref.dtype)

    lane_w = lax.broadcasted_iota(jnp.int32, (KV_W, wb), 1)
    wout_ref[s, 0] = jnp.where(lane_w == wb - 1, kwc_ref[s], pltpu.roll(win_ref[0, s, 0], wb - 1, 1))
    wout_ref[s, 1] = jnp.where(lane_w == wb - 1, vwc_ref[s], pltpu.roll(win_ref[0, s, 1], wb - 1, 1))


def _cache_rows_on_lanes(c):
    nd = c.ndim
    c = jnp.moveaxis(c, nd - 4, nd - 1)
    return c.reshape(c.shape[:-3] + (c.shape[-3] * c.shape[-2], c.shape[-1]))


def _nsa_sample_pallas(layer, q, gl, ks, vs, kw, vw, cmpw, cmp_t, sel_t, win_t, page_table):
    B = q.shape[0]
    n_pages = page_table.shape[1]
    past = n_pages * PAGE_SIZE
    wb = win_t.shape[-1]
    n_half = past // CMP_STRIDE
    n_cmp = (past + 1 - CMP_BLOCK) // CMP_STRIDE + 1
    n_sb = -(-(past + 1) // SEL_BLOCK)
    wk1, pek, wk2, wv1, pev, wv2 = cmpw
    wk, pekf, w1k, w2k = _compress_weights(wk1, pek, wk2)
    wv, pevf, w1v, w2v = _compress_weights(wv1, pev, wv2)
    nb = _pick_tile(B, (SAMPLE_NB,))
    nb = nb if nb == SAMPLE_NB else 1
    cos, sin = _rope_tables(jnp.tile(jnp.arange(n_half) * CMP_STRIDE + CMP_BLOCK - 1, nb), N_KV_HEADS)
    ov = _overlap_matrix(n_cmp, n_sb)[:n_half]
    ex = jnp.asarray((np.arange(past)[None, :] // SEL_BLOCK) == np.arange(128)[:, None], jnp.bfloat16)
    groups = PAGE_SIZE // CMP_STRIDE
    src_row = (np.arange(PAGE_SIZE) % groups) * CMP_STRIDE + np.arange(PAGE_SIZE) // groups
    perm = jnp.asarray(src_row[:, None] == np.arange(PAGE_SIZE)[None, :], jnp.bfloat16)
    consts = (wk, wv, pekf, pevf, w1k, w1v, w2k, w2v, cos, sin, ov, ex, perm)
    row3 = lambda x: x.reshape(B, 1, KV_W)
    col3 = lambda x: x.reshape(B, KV_W, 1)
    per_b = (q.reshape(B, N_HEADS, HEAD_DIM), gl.reshape(B, N_HEADS, 3), row3(ks), row3(vs), row3(kw), row3(vw),
             col3(kw), col3(vw))
    page_spec = lambda s, p: pl.BlockSpec((1, 1, 2, KV_W, PAGE_SIZE),
                                          lambda b, pt: (layer, pt[b * nb + s, p], 0, 0, 0))
    b_spec = lambda a: pl.BlockSpec((nb,) + a.shape[1:], lambda b, pt: (b,) + (0,) * (a.ndim - 1))
    full = lambda a: pl.BlockSpec(a.shape, lambda b, pt: (0,) * a.ndim)
    in_specs = ([page_spec(s, p) for s in range(nb) for p in range(n_pages)] * 2
                + [pl.BlockSpec((1, nb, 2, KV_W, wb), lambda b, pt: (layer, b, 0, 0, 0))]
                + [b_spec(a) for a in per_b] + [full(a) for a in consts])
    return pl.pallas_call(
        functools.partial(_nsa_sample_kernel, n_pages=n_pages, n_sb=n_sb, nb=nb),
        out_shape=(jax.ShapeDtypeStruct((B, N_HEADS, HEAD_DIM), jnp.bfloat16),
                   jax.ShapeDtypeStruct((B, 2, KV_W, wb), jnp.float32)),
        grid_spec=pltpu.PrefetchScalarGridSpec(
            num_scalar_prefetch=1, grid=(B // nb,), in_specs=in_specs,
            out_specs=(pl.BlockSpec((nb, N_HEADS, HEAD_DIM), lambda b, pt: (b, 0, 0)),
                       pl.BlockSpec((nb, 2, KV_W, wb), lambda b, pt: (b, 0, 0, 0)))),
        compiler_params=pltpu.CompilerParams(dimension_semantics=("arbitrary",), vmem_limit_bytes=VMEM_LIMIT_BYTES),
        name="nsa_sample",
    )(page_table, *([cmp_t] * (nb * n_pages)), *([sel_t] * (nb * n_pages)), win_t, *per_b, *consts)


SSM_N = SSM_GROUPS * SSM_STATE
SSM_LANE_BLK = 512
SSM_TL = 64


def _ssm_kernel(u_ref, h0r_ref, h0i_ref, ar_ref, ai_ref, bm_ref, cr_ref, ci_ref, d_ref,
                y_ref, hr_ref, hi_ref, xr_s, xi_s, *, tl, nb):
    c = pl.program_id(0)

    @pl.when(c == 0)
    def _():
        hr_ref[...] = h0r_ref[...]
        hi_ref[...] = h0i_ref[...]

    u = u_ref[...]
    ub = u.astype(jnp.bfloat16)
    n_grp = SSM_WIDTH // 128
    for j in range(n_grp):
        bu = jnp.dot(ub[:, 128 * j:128 * (j + 1)], bm_ref[j], preferred_element_type=jnp.float32)
        xr_s[:, 512 * j:512 * (j + 1)] = bu[:, :512]
        xi_s[:, 512 * j:512 * (j + 1)] = bu[:, 512:]

    for lb in range(SSM_N // SSM_LANE_BLK):
        sl = slice(lb * SSM_LANE_BLK, (lb + 1) * SSM_LANE_BLK)
        ar = jnp.broadcast_to(ar_ref[:, sl], (8, SSM_LANE_BLK))
        ai = jnp.broadcast_to(ai_ref[:, sl], (8, SSM_LANE_BLK))
        for r in range(nb // 8):
            def step(t, carry):
                hr, hi = carry
                row = pl.multiple_of(t * nb + r * 8, 8)
                xr = xr_s[pl.ds(row, 8), sl]
                xi = xi_s[pl.ds(row, 8), sl]
                nr = ar * hr - ai * hi + xr
                ni = ar * hi + ai * hr + xi
                xr_s[pl.ds(row, 8), sl] = nr
                xi_s[pl.ds(row, 8), sl] = ni
                return nr, ni

            hr, hi = lax.fori_loop(0, tl, step, (hr_ref[r * 8:(r + 1) * 8, sl], hi_ref[r * 8:(r + 1) * 8, sl]))
            hr_ref[r * 8:(r + 1) * 8, sl] = hr
            hi_ref[r * 8:(r + 1) * 8, sl] = hi

    for j in range(n_grp):
        yr = jnp.dot(xr_s[:, 512 * j:512 * (j + 1)].astype(jnp.bfloat16), cr_ref[j], preferred_element_type=jnp.float32)
        yi = jnp.dot(xi_s[:, 512 * j:512 * (j + 1)].astype(jnp.bfloat16), ci_ref[j], preferred_element_type=jnp.float32)
        y_ref[:, 128 * j:128 * (j + 1)] = yr - yi + d_ref[:, 128 * j:128 * (j + 1)] * u[:, 128 * j:128 * (j + 1)]


def _ssm_params(a_re, a_im, log_dt, b_re, b_im, c_re, c_im, d_skip):
    dt = jnp.exp(log_dt)[:, None]
    mag = jnp.exp(dt * a_re)
    ab_re = mag * jnp.cos(dt * a_im)
    ab_im = mag * jnp.sin(dt * a_im)
    den = a_re * a_re + a_im * a_im
    zr = ((ab_re - 1.0) * a_re + ab_im * a_im) / den
    zi = (ab_im * a_re - (ab_re - 1.0) * a_im) / den
    bb_re = zr[..., None] * b_re - zi[..., None] * b_im
    bb_im = zr[..., None] * b_im + zi[..., None] * b_re
    n_grp = SSM_WIDTH // 128
    gpl = 128 // SSM_GROUP
    eye = jnp.eye(gpl, dtype=jnp.float32)

    def b_blocks(bb):
        x = bb.reshape(n_grp, gpl, SSM_STATE, SSM_GROUP)
        return jnp.einsum('jgpc,gh->jgchp', x, eye).reshape(n_grp, 128, gpl * SSM_STATE)

    def c_blocks(cc):
        x = cc.reshape(n_grp, gpl, SSM_GROUP, SSM_STATE)
        return jnp.einsum('jgcp,gh->jgphc', x, eye).reshape(n_grp, gpl * SSM_STATE, 128)

    bm = jnp.concatenate([b_blocks(bb_re), b_blocks(bb_im)], axis=-1).astype(jnp.bfloat16)
    return (ab_re.reshape(1, SSM_N), ab_im.reshape(1, SSM_N), bm,
            c_blocks(c_re).astype(jnp.bfloat16), c_blocks(c_im).astype(jnp.bfloat16), d_skip.reshape(1, SSM_WIDTH))


def _ssm_pallas(u_tb, B, T, h0_re, h0_im, params):
    ab_re, ab_im, bm, cr, ci, d = params
    tl = _pick_tile(T, (SSM_TL,))
    full = lambda a: pl.BlockSpec(a.shape, lambda c: (0,) * a.ndim)
    h0r = h0_re.reshape(B, SSM_N)
    h0i = h0_im.reshape(B, SSM_N)
    consts = (h0r, h0i, ab_re, ab_im, bm, cr, ci, d)
    y, hr, hi = pl.pallas_call(
        functools.partial(_ssm_kernel, tl=tl, nb=B),
        out_shape=(jax.ShapeDtypeStruct((T * B, SSM_WIDTH), jnp.float32),
                   jax.ShapeDtypeStruct((B, SSM_N), jnp.float32), jax.ShapeDtypeStruct((B, SSM_N), jnp.float32)),
        grid=(T // tl,),
        in_specs=[pl.BlockSpec((tl * B, SSM_WIDTH), lambda c: (c, 0))] + [full(a) for a in consts],
        out_specs=(pl.BlockSpec((tl * B, SSM_WIDTH), lambda c: (c, 0)),
                   pl.BlockSpec((B, SSM_N), lambda c: (0, 0)), pl.BlockSpec((B, SSM_N), lambda c: (0, 0))),
        scratch_shapes=[pltpu.VMEM((tl * B, SSM_N), jnp.float32), pltpu.VMEM((tl * B, SSM_N), jnp.float32)],
        compiler_params=pltpu.CompilerParams(dimension_semantics=("arbitrary",), vmem_limit_bytes=VMEM_LIMIT_BYTES),
        name="ssm",
    )(u_tb, *consts)
    return y, hr.reshape(B, SSM_GROUPS, SSM_STATE), hi.reshape(B, SSM_GROUPS, SSM_STATE)


def _block(h, p_l, pos, lw, sample, h0_re, h0_im, conv_prefix):
    (g_attn, w_in, g_q, g_kc, g_ks, g_kw, wk1, pek, wk2, wv1, pev, wv2,
     a_re, a_im, log_dt, b_re, b_im, c_re, c_im, d_skip,
     w_a, w_glu1, w_glu2, w_o, g_ffn, w_up, conv_w, conv_b, w_down,
     g_ple, w_ple_gate, w_ple) = lw
    B, T = p_l.shape[:2]
    N = B * T
    qs, rows_cmp, rows_sel, rows_win, gl, u, gab, *attn = _in_proj_pallas(h, w_in, g_attn, g_q, g_kc, g_ks, g_kw,
                                                                           pos, T)
    gl = gl[:, :3 * N_HEADS]
    cmpw = (wk1, pek, wk2, wv1, pev, wv2)
    as_rows = lambda a: a.reshape(B, T, 2, N_KV_HEADS, HEAD_DIM)
    if sample is None:
        kcmp, vcmp = _compress_pallas(rows_cmp.reshape(B, T, 2 * KV_W), cmpw)
        o = _nsa_prompt_pallas(qs.reshape(B, T, Q_W), kcmp, vcmp, *attn).reshape(N, Q_W)
        n_keep = min(WINDOW, T)
        rows = (as_rows(rows_cmp), as_rows(rows_sel), as_rows(rows_win)[:, T - n_keep:])
    else:
        assert T == 1
        layer, cmp_t, sel_t, win_t, page_table = sample
        o, wnew = _nsa_sample_pallas(layer, qs, gl, rows_sel[:, :KV_W], rows_sel[:, KV_W:],
                                     rows_win[:, :KV_W], rows_win[:, KV_W:], cmpw, cmp_t, sel_t, win_t, page_table)
        o = o.reshape(N, Q_W)
        wb = wnew.shape[-1]
        wnew = wnew.reshape(B, 2, N_KV_HEADS, HEAD_DIM, wb).transpose(0, 4, 1, 2, 3)
        rows = (as_rows(rows_cmp), as_rows(rows_sel), wnew)
    y, hr, hi = _ssm_pallas(u.reshape(T * B, SSM_WIDTH), B, T, h0_re, h0_im,
                            _ssm_params(a_re, a_im, log_dt, b_re, b_im, c_re, c_im, d_skip))
    h, conv_rows = _mix_ffn_pallas(h, o, y.reshape(u.shape), gab, p_l.reshape(N, -1), conv_prefix,
                                   (w_a, w_glu1, w_glu2, w_o, g_ffn, w_up, conv_w, conv_b, w_down,
                                    g_ple, w_ple_gate, w_ple), T)
    return h, rows, hr, hi, conv_rows


def kernel(x_prompt, x_sample, cache_cmp, cache_sel, cache_win, state_ssm_re, state_ssm_im, state_conv, page_table, p_prompt, p_sample, g_attn, w_in, g_q, g_kc, g_ks, g_kw, cmp_wk1, cmp_pek, cmp_wk2, cmp_wv1, cmp_pev, cmp_wv2, ssm_a_re, ssm_a_im, ssm_log_dt, ssm_b_re, ssm_b_im, ssm_c_re, ssm_c_im, ssm_d, w_a, w_glu1, w_glu2, w_o, g_ffn, w_up, conv_w, conv_b, w_down, g_ple, w_ple_gate, w_ple):
    Bp, Tp = x_prompt.shape[:2]
    Ts = x_sample.shape[1]
    depth = w_in.shape[0]
    past = page_table.shape[1] * PAGE_SIZE
    pos_p = jnp.arange(Tp)
    pos_s = past + jnp.arange(Ts)
    zeros_h = jnp.zeros((Bp, SSM_GROUPS, SSM_STATE), x_prompt.dtype)
    layer_w = (g_attn, _pad_w_in(w_in), g_q, g_kc, g_ks, g_kw, cmp_wk1, cmp_pek, cmp_wk2, cmp_wv1, cmp_pev, cmp_wv2,
               ssm_a_re, ssm_a_im, ssm_log_dt, ssm_b_re, ssm_b_im, ssm_c_re, ssm_c_im, ssm_d,
               w_a, w_glu1, w_glu2, w_o, g_ffn, w_up, conv_w, conv_b, w_down, g_ple, w_ple_gate, w_ple)
    cmp_t, sel_t, win_t = (_cache_rows_on_lanes(c) for c in (cache_cmp, cache_sel, cache_win))
    st = [[] for _ in range(12)]
    hp, hs = x_prompt.reshape(Bp * Tp, D_MODEL), x_sample.reshape(-1, D_MODEL)
    for i in range(depth):
        lw = [w[i] for w in layer_w]
        hp, rows, hr, hi, cv = _block(hp, p_prompt[i], pos_p, lw, None, zeros_h, zeros_h, None)
        for j, a in enumerate(list(rows) + [hr, hi, cv]):
            st[j].append(a)
        hs, rows, hr, hi, cv = _block(hs, p_sample[i], pos_s, lw, (i, cmp_t, sel_t, win_t, page_table),
                                      state_ssm_re[i], state_ssm_im[i], state_conv[i])
        for j, a in enumerate(list(rows) + [hr, hi, cv]):
            st[6 + j].append(a)
    return (hp.reshape(x_prompt.shape), hs.reshape(x_sample.shape)) + tuple(jnp.stack(s) for s in st)
```

```python
import functools
import math

import numpy as np
import jax
import jax.numpy as jnp
from jax import lax
from jax.experimental import pallas as pl
from jax.experimental.pallas import tpu as pltpu

D_MODEL = 1024
N_HEADS = 8
N_KV_HEADS = 2
HEAD_DIM = 64
GROUP = N_HEADS // N_KV_HEADS
Q_W = N_HEADS * HEAD_DIM
KV_W = N_KV_HEADS * HEAD_DIM
CMP_BLOCK = 32
CMP_STRIDE = 16
SEL_BLOCK = 64
N_SEL = 8
WINDOW = 512
PAGE_SIZE = 128
ROPE_THETA = 10000.0
SSM_WIDTH = D_MODEL // 2
SSM_GROUP = 16
SSM_GROUPS = SSM_WIDTH // SSM_GROUP
SSM_STATE = 64
D_FF = 11 * D_MODEL // 4
CONV_W = 3
EPS = 1e-6
NEG_INF = -1e30
FORCE_SCORE = 1e9
SCALE = HEAD_DIM ** -0.5
QK_SCALE = SCALE * math.log2(math.e)

VMEM_LIMIT_BYTES = 56 * 1024 * 1024


def _pick_tile(n, cands):
    for c in cands:
        if n % c == 0:
            return c
    return n


def _rms(x, g):
    return x * lax.rsqrt(jnp.mean(x * x, axis=-1, keepdims=True) + EPS) * g


def _bdot(a, b):
    return jnp.dot(a.astype(jnp.bfloat16), b, preferred_element_type=jnp.float32)


def _layer_spec(stack, layer):
    nd = stack.ndim
    return pl.BlockSpec((None,) + stack.shape[1:], lambda *_: (layer,) + (0,) * (nd - 1),
                        pipeline_mode=pl.Buffered(1))


GL_PAD = 128
_IN_WIDTHS = (Q_W, 6 * KV_W, GL_PAD, SSM_WIDTH, 2 * D_MODEL)
_IN_OFFS = tuple(int(v) for v in np.cumsum((0,) + _IN_WIDTHS))
ROW_TILE = 256
IN_TILE = 512


def _in_proj_kernel(h_ref, g_ref, w_ref, gq_ref, gk_ref, ones_ref, cos_ref, sin_ref,
                    q_ref, cmp_ref, sel_ref, win_ref, gl_ref, u_ref, gab_ref, *attn_refs):
    xn = _rms(h_ref[...], g_ref[...]).astype(jnp.bfloat16)
    seg = lambda s: jnp.dot(xn, w_ref[:, _IN_OFFS[s]:_IN_OFFS[s + 1]], preferred_element_type=jnp.float32)
    cos, sin = cos_ref[...], sin_ref[...]
    ones = ones_ref[...]

    def head_norm(x, gain):
        x2 = x * x
        hi = x2.astype(jnp.bfloat16)
        lo = (x2 - hi.astype(jnp.float32)).astype(jnp.bfloat16)
        ss = (jnp.dot(hi, ones, preferred_element_type=jnp.float32)
              + jnp.dot(lo, ones, preferred_element_type=jnp.float32))
        return x * lax.rsqrt(ss * (1.0 / HEAD_DIM) + EPS) * gain

    zq = seg(0)
    for c in range(Q_W // KV_W):
        qn = _rope_lanes(head_norm(zq[:, c * KV_W:(c + 1) * KV_W], gq_ref[...]), cos, sin)
        q_ref[:, c * KV_W:(c + 1) * KV_W] = (qn * QK_SCALE).astype(q_ref.dtype)
    zkv = seg(1)
    part = lambda c: zkv[:, c * KV_W:(c + 1) * KV_W]
    cmp_ref[:, :KV_W] = head_norm(part(0), gk_ref[0:1])
    cmp_ref[:, KV_W:] = part(1)
    sel_ref[:, :KV_W] = _rope_lanes(head_norm(part(2), gk_ref[1:2]), cos, sin)
    sel_ref[:, KV_W:] = part(3)
    win_ref[:, :KV_W] = _rope_lanes(head_norm(part(4), gk_ref[2:3]), cos, sin)
    win_ref[:, KV_W:] = part(5)
    gl = seg(2)
    gl_ref[...] = gl
    u_ref[...] = seg(3)
    gab_ref[...] = seg(4)
    if attn_refs:
        ksb_ref, vst_ref, kwb_ref, vwt_ref, glt_ref = attn_refs
        tq = vst_ref.shape[-1]
        for k_ref, v_ref, src in ((ksb_ref, vst_ref, sel_ref), (kwb_ref, vwt_ref, win_ref)):
            kb = src[:, :KV_W].astype(jnp.bfloat16)
            for hd in range(N_KV_HEADS):
                k_ref[0, hd] = kb[:, hd * HEAD_DIM:(hd + 1) * HEAD_DIM]
            for r in range(src.shape[0] // tq):
                vt = src[r * tq:(r + 1) * tq, KV_W:].T.astype(jnp.bfloat16)
                for hd in range(N_KV_HEADS):
                    v_ref[0, hd, r] = vt[hd * HEAD_DIM:(hd + 1) * HEAD_DIM]
        glt = gl.T
        for hd in range(N_KV_HEADS):
            glt_ref[0, hd] = glt[hd * 3 * GROUP:(hd + 1) * 3 * GROUP]


def _pad_w_in(w_in):
    a = Q_W + 6 * KV_W + 3 * N_HEADS
    pad = jnp.zeros(w_in.shape[:-1] + (GL_PAD - 3 * N_HEADS,), w_in.dtype)
    return jnp.concatenate([w_in[..., :a], pad, w_in[..., a:]], axis=-1).astype(jnp.bfloat16)


def _in_proj_pallas(h2d, w_in_p, layer, g_attn, g_q, g_kc, g_ks, g_kw, pos, seq_len):
    N = h2d.shape[0]
    T = seq_len
    B = N // T
    tm = _pick_tile(N, (IN_TILE, ROW_TILE, 128))
    nt = max(T // tm, 1)
    assert T == 1 or T % tm == 0
    cos, sin = _rope_tables(pos, N_KV_HEADS)
    if T == 1:
        cos, sin = (jnp.broadcast_to(t, (tm, KV_W)) for t in (cos, sin))
    tile2 = lambda g: jnp.tile(g.reshape(1, HEAD_DIM), (1, N_KV_HEADS))
    gk = jnp.concatenate([tile2(g_kc), tile2(g_ks), tile2(g_kw), jnp.zeros((5, KV_W), jnp.float32)], axis=0)
    ones = jnp.asarray(np.kron(np.eye(N_KV_HEADS), np.ones((HEAD_DIM, HEAD_DIM))), jnp.bfloat16)
    row = lambda w: pl.BlockSpec((tm, w), lambda i: (i, 0))
    const = lambda a: pl.BlockSpec(a.shape, lambda i: (0,) * a.ndim, pipeline_mode=pl.Buffered(1))
    tab = pl.BlockSpec((tm, KV_W), lambda i: (i % nt, 0))
    if T == 1:
        u_shape, u_spec = (N, SSM_WIDTH), row(SSM_WIDTH)
    else:
        u_shape, u_spec = (T, B * SSM_WIDTH), pl.BlockSpec((tm, SSM_WIDTH), lambda i: (i % nt, i // nt))
    consts = (g_attn.reshape(1, D_MODEL), w_in_p, tile2(g_q), gk, ones)
    f32, bf = jnp.float32, jnp.bfloat16
    out_shape = ((jax.ShapeDtypeStruct((N, Q_W), bf),) + (jax.ShapeDtypeStruct((N, 2 * KV_W), f32),) * 3
                 + (jax.ShapeDtypeStruct((N, GL_PAD), f32), jax.ShapeDtypeStruct(u_shape, f32),
                    jax.ShapeDtypeStruct((N, 2 * D_MODEL), f32)))
    out_specs = (row(Q_W), row(2 * KV_W), row(2 * KV_W), row(2 * KV_W), row(GL_PAD), u_spec, row(2 * D_MODEL))
    if T > 1:
        tq = ATT_TQ
        assert tm % tq == 0
        k_shape = jax.ShapeDtypeStruct((B, N_KV_HEADS, T, HEAD_DIM), bf)
        k_spec = pl.BlockSpec((1, N_KV_HEADS, tm, HEAD_DIM), lambda i: (i // nt, 0, i % nt, 0))
        v_shape = jax.ShapeDtypeStruct((B, N_KV_HEADS, T // tq, HEAD_DIM, tq), bf)
        v_spec = pl.BlockSpec((1, N_KV_HEADS, tm // tq, HEAD_DIM, tq), lambda i: (i // nt, 0, i % nt, 0, 0))
        g_shape = jax.ShapeDtypeStruct((B, N_KV_HEADS, 3 * GROUP, T), f32)
        g_spec = pl.BlockSpec((1, N_KV_HEADS, 3 * GROUP, tm), lambda i: (i // nt, 0, 0, i % nt))
        out_shape += (k_shape, v_shape, k_shape, v_shape, g_shape)
        out_specs += (k_spec, v_spec, k_spec, v_spec, g_spec)
    return pl.pallas_call(
        _in_proj_kernel,
        out_shape=out_shape,
        grid=(N // tm,),
        in_specs=[row(D_MODEL)] + [_layer_spec(a, layer) if a.ndim == 3 else const(a) for a in consts] + [tab, tab],
        out_specs=out_specs,
        compiler_params=pltpu.CompilerParams(dimension_semantics=("parallel",), vmem_limit_bytes=VMEM_LIMIT_BYTES),
        name="in_proj",
    )(h2d, *consts, cos, sin)


FF_CHUNK = D_FF // 2
assert FF_CHUNK % 128 == 0


def _mix_ffn_kernel(h_ref, o_ref, y_ref, gab_ref, p_ref, pre0_ref, pre1_ref,
                    wa_ref, wg1_ref, wg2_ref, wo_ref, gffn_ref, wup_ref, cw_ref, cb_ref, wdn_ref,
                    gple_ref, wpg_ref, wpl_ref, hout_ref, cs0_ref, cs1_ref, carry_s, *, seq_tiles):
    tm = h_ref.shape[0]
    a_out = jnp.dot(o_ref[...], wa_ref[...], preferred_element_type=jnp.float32)
    yg = jax.nn.gelu(y_ref[...]).astype(jnp.bfloat16)
    b_out = (jnp.dot(yg, wg1_ref[...], preferred_element_type=jnp.float32)
             * jax.nn.sigmoid(jnp.dot(yg, wg2_ref[...], preferred_element_type=jnp.float32)))
    mixed = (jax.nn.sigmoid(gab_ref[:, :D_MODEL]) * a_out + jax.nn.sigmoid(gab_ref[:, D_MODEL:]) * b_out)
    h1 = h_ref[...] + _bdot(mixed, wo_ref[...])

    xn = _rms(h1, gffn_ref[...]).astype(jnp.bfloat16)
    if seq_tiles:
        @pl.when(pl.program_id(0) % seq_tiles == 0)
        def _():
            carry_s[...] = jnp.zeros_like(carry_s)
        row = lax.broadcasted_iota(jnp.int32, (tm, 1), 0)
    ffn = jnp.zeros((tm, D_MODEL), jnp.float32)
    for c in range(D_FF // FF_CHUNK):
        sl = slice(c * FF_CHUNK, (c + 1) * FF_CHUNK)
        gp = jnp.dot(xn, wup_ref[:, sl], preferred_element_type=jnp.float32)
        val = jnp.dot(xn, wup_ref[:, D_FF + c * FF_CHUNK:D_FF + (c + 1) * FF_CHUNK], preferred_element_type=jnp.float32)
        if seq_tiles:
            old1, old2 = carry_s[7:8, sl], carry_s[6:7, sl]
            prev1 = jnp.where(row == 0, old1, pltpu.roll(gp, 1, 0))
            prev2 = jnp.where(row == 0, old2, jnp.where(row == 1, old1, pltpu.roll(gp, 2, 0)))
            carry_s[:, sl] = gp[tm - 8:, :]
            cs0_ref[0, :, sl] = gp[tm - 2:tm - 1, :]
            cs1_ref[0, :, sl] = gp[tm - 1:tm, :]
        else:
            prev2, prev1 = pre0_ref[:, sl], pre1_ref[:, sl]
            cs0_ref[:, sl] = prev1
            cs1_ref[:, sl] = gp
        conv = cb_ref[:, sl] + cw_ref[0:1, sl] * prev2 + cw_ref[1:2, sl] * prev1 + cw_ref[2:3, sl] * gp
        ffn = ffn + _bdot(jax.nn.gelu(conv) * val, wdn_ref[sl, :])
    h2 = h1 + ffn

    gate = jax.nn.sigmoid(_bdot(_rms(h2, gple_ref[...]), wpg_ref[...]))
    hout_ref[...] = h2 + gate * _bdot(p_ref[...], wpl_ref[...])


def _mix_ffn_pallas(h2d, o2d, y, gab, p2d, prefix, w, layer, seq_len):
    (w_a, w_glu1, w_glu2, w_o, g_ffn, w_up, conv_w, conv_b, w_down, g_ple, w_ple_gate, w_ple) = w
    N = h2d.shape[0]
    T = seq_len
    B = N // T
    tm = _pick_tile(N, (ROW_TILE, 128))
    nt = max(T // tm, 1)
    seq = prefix is None
    assert (seq and T % tm == 0 and tm >= 8) or (not seq and T == 1)
    row = lambda wd: pl.BlockSpec((tm, wd), lambda i: (i, 0))
    const = lambda a: pl.BlockSpec(a.shape, lambda i: (0,) * a.ndim, pipeline_mode=pl.Buffered(1))
    f32 = jnp.float32
    if seq:
        y_spec = pl.BlockSpec((tm, SSM_WIDTH), lambda i: (i % nt, i // nt))
        pre = (jnp.zeros((8, D_FF), f32),) * 2
        pre_spec = const(pre[0])
        cs_shape = jax.ShapeDtypeStruct((B, 1, D_FF), f32)
        cs_spec = pl.BlockSpec((1, 1, D_FF), lambda i: (i // nt, 0, 0))
    else:
        y_spec = row(SSM_WIDTH)
        pre = (prefix[:, 0], prefix[:, 1])
        pre_spec = row(D_FF)
        cs_shape = jax.ShapeDtypeStruct((N, D_FF), f32)
        cs_spec = row(D_FF)
    vec = lambda a: a.reshape(1, -1)
    cw8 = jnp.concatenate([conv_w, jnp.zeros((8 - CONV_W, D_FF), f32)], axis=0)
    consts = (w_a, w_glu1, w_glu2, w_o, vec(g_ffn), w_up, cw8, vec(conv_b), w_down, vec(g_ple), w_ple_gate, w_ple)
    spec = lambda a: _layer_spec(a, layer) if a.ndim == 3 else const(a)
    hout, cs0, cs1 = pl.pallas_call(
        functools.partial(_mix_ffn_kernel, seq_tiles=nt if seq else 0),
        out_shape=(jax.ShapeDtypeStruct((N, D_MODEL), f32), cs_shape, cs_shape),
        grid=(N // tm,),
        in_specs=[row(D_MODEL), row(Q_W), y_spec, row(2 * D_MODEL), row(p2d.shape[1]), pre_spec, pre_spec]
        + [spec(a) for a in consts],
        out_specs=(row(D_MODEL), cs_spec, cs_spec),
        scratch_shapes=[pltpu.VMEM((8, D_FF), f32)],
        compiler_params=pltpu.CompilerParams(dimension_semantics=("arbitrary",), vmem_limit_bytes=VMEM_LIMIT_BYTES),
        name="mix_ffn",
    )(h2d, o2d, y, gab, p2d, *pre, *consts)
    return hout, jnp.stack([cs0.reshape(B, D_FF), cs1.reshape(B, D_FF)], axis=1)


HALF_ROWS = CMP_BLOCK // CMP_STRIDE
assert HALF_ROWS == 2


def _rope_lanes(x, cos, sin_signed):
    w = x.shape[-1]
    half = HEAD_DIM // 2
    lane = lax.broadcasted_iota(jnp.int32, x.shape, x.ndim - 1)
    first = (lane % HEAD_DIM) < half
    partner = jnp.where(first, pltpu.roll(x, w - half, x.ndim - 1), pltpu.roll(x, half, x.ndim - 1))
    return x * cos + partner * sin_signed


def _compress_rows(xcat, w_ref, pe_ref, w1_ref, w2_ref, n_half):
    acc = jnp.dot(xcat, w_ref[...], preferred_element_type=jnp.float32)
    pa = acc[:, :256]
    pb = pltpu.roll(acc[:, 256:], n_half - 1, 0)
    bias = jnp.dot(pe_ref[...].astype(jnp.bfloat16), w1_ref[...], preferred_element_type=jnp.float32)[0:1]
    bias2 = jnp.concatenate([bias, bias], axis=1)
    hdn = jax.nn.gelu(pa + pb + bias2)
    return jnp.dot(hdn.astype(jnp.bfloat16), w2_ref[...], preferred_element_type=jnp.float32)


def _compress_kernel(xk_ref, xv_ref, wk_ref, wv_ref, pek_ref, pev_ref, w1k_ref, w1v_ref, w2k_ref, w2v_ref,
                     cos_ref, sin_ref, ko_ref, vo_ref, *, n_half, n_cmp):
    row = lax.broadcasted_iota(jnp.int32, (n_half, KV_W), 0)
    gather = lambda x_ref: jnp.concatenate(
        [x_ref[0, pl.ds(j, n_half, stride=CMP_STRIDE), :].astype(jnp.bfloat16) for j in range(CMP_STRIDE)], axis=1)
    k = _compress_rows(gather(xk_ref), wk_ref, pek_ref, w1k_ref, w2k_ref, n_half)
    k = _rope_lanes(k, cos_ref[...], sin_ref[...])
    v = _compress_rows(gather(xv_ref), wv_ref, pev_ref, w1v_ref, w2v_ref, n_half)
    ko_ref[0] = jnp.where(row < n_cmp, k, 0.0).astype(ko_ref.dtype)
    vo_ref[0] = jnp.where(row < n_cmp, v, 0.0).astype(vo_ref.dtype)


def _blockdiag2(w):
    z = jnp.zeros_like(w)
    return jnp.concatenate([jnp.concatenate([w, z], axis=-1), jnp.concatenate([z, w], axis=-1)], axis=-2)


def _compress_weights(w1, pe, w2):
    bd = _blockdiag2(w1)
    wcat = jnp.concatenate([bd[:CMP_STRIDE], bd[CMP_STRIDE:]], axis=-1)
    wcat = wcat.reshape(CMP_STRIDE * KV_W, -1).astype(jnp.bfloat16)
    pe_flat = jnp.broadcast_to(pe.reshape(1, -1), (8, pe.size))
    w1_flat = w1.reshape(-1, w1.shape[-1]).astype(jnp.bfloat16)
    w2bd = _blockdiag2(w2).astype(jnp.bfloat16)
    return wcat, pe_flat, w1_flat, w2bd


def _rope_tables(pos, reps):
    half = HEAD_DIM // 2
    inv = jnp.float32(ROPE_THETA) ** (-jnp.arange(half, dtype=jnp.float32) / half)
    ang = pos.astype(jnp.float32)[:, None] * inv[None, :]
    cos = jnp.cos(ang)
    sin = jnp.sin(ang)
    return (jnp.tile(jnp.concatenate([cos, cos], axis=-1), (1, reps)),
            jnp.tile(jnp.concatenate([-sin, sin], axis=-1), (1, reps)))


def _compress_pallas(rows, cmpw):
    wk1, pek, wk2, wv1, pev, wv2 = cmpw
    B, L, _ = rows.shape
    n_half = L // CMP_STRIDE
    n_cmp = n_half - 1
    wk, pekf, w1k, w2k = _compress_weights(wk1, pek, wk2)
    wv, pevf, w1v, w2v = _compress_weights(wv1, pev, wv2)
    end = jnp.arange(n_half) * CMP_STRIDE + CMP_BLOCK - 1
    cos, sin = _rope_tables(end, N_KV_HEADS)
    full = lambda a: pl.BlockSpec(a.shape, lambda b: (0,) * a.ndim)
    consts = (wk, wv, pekf, pevf, w1k, w1v, w2k, w2v, cos, sin)
    return pl.pallas_call(
        functools.partial(_compress_kernel, n_half=n_half, n_cmp=n_cmp),
        out_shape=(jax.ShapeDtypeStruct((B, n_half, KV_W), jnp.bfloat16),) * 2,
        grid=(B,),
        in_specs=[pl.BlockSpec((1, L, KV_W), lambda b: (b, 0, 0)), pl.BlockSpec((1, L, KV_W), lambda b: (b, 0, 1))]
        + [full(a) for a in consts],
        out_specs=(pl.BlockSpec((1, n_half, KV_W), lambda b: (b, 0, 0)),) * 2,
        compiler_params=pltpu.CompilerParams(dimension_semantics=("parallel",), vmem_limit_bytes=VMEM_LIMIT_BYTES),
        name="compress",
    )(rows, rows, *consts)


ATT_TQ = 256
SEL_KC = 512
BIG_NEG = -3.0e38


def _softmax_rows(s, mask):
    s = jnp.where(mask, s, NEG_INF)
    m = jnp.max(s, axis=-1, keepdims=True)
    e = jnp.where(mask, jnp.exp2(s - m), 0.0)
    return e / jnp.maximum(jnp.sum(e, axis=-1, keepdims=True), 1e-30)


def _select_blocks(imp, tpos, n_sb):
    rows = imp.shape[0]
    jl = lax.broadcasted_iota(jnp.int32, (rows, 128), 1)
    forced = (jl == 0) | (jl == (tpos >> 6))
    imp = jnp.where(forced, FORCE_SCORE, imp)
    imp = jnp.where(jl * SEL_BLOCK <= tpos, imp, NEG_INF)
    imp = jnp.where(jl < n_sb, imp, BIG_NEG)
    beaten_by = jnp.zeros((rows, 128), jnp.float32)
    for i in range(n_sb):
        col = imp[:, i:i + 1]
        beaten_by = beaten_by + jnp.where((col > imp) | ((col == imp) & (i < jl)), 1.0, 0.0)
    return jnp.where((beaten_by < N_SEL) & (imp > 0.5 * NEG_INF), 1.0, 0.0)


def _select_blocks_t(imp, tpos, n_sb):
    nj, tq = imp.shape
    jr = lax.broadcasted_iota(jnp.int32, (nj, tq), 0)
    jf = jr.astype(jnp.float32)
    forced = (jr == 0) | (jr == (tpos >> 6))
    imp = jnp.where(forced, FORCE_SCORE, imp)
    imp = jnp.where(jr * SEL_BLOCK <= tpos, imp, NEG_INF)
    imp = jnp.where(jr < n_sb, imp, BIG_NEG)
    sel = jnp.zeros((nj, tq), jnp.float32)
    for _ in range(N_SEL):
        m = jnp.max(imp, axis=0, keepdims=True)
        first = jnp.min(jnp.where(imp == m, jf, 1e9), axis=0, keepdims=True)
        hit = jf == first
        sel = jnp.where(hit & (m > 0.5 * NEG_INF), 1.0, sel)
        imp = jnp.where(hit, BIG_NEG, imp)
    return sel


def _nsa_prompt_kernel(q_ref, kcmp_ref, vcmpt_ref, ks_ref, vst_ref, kw_ref, vwt_ref, glt_ref, ovt_ref, et_ref, o_ref,
                       *, tq, n_cmp, n_sb):
    i = pl.program_id(2)
    t0 = i * tq
    bf = jnp.bfloat16
    qf = q_ref[0]
    q4 = jnp.concatenate([qf[:, g * HEAD_DIM:(g + 1) * HEAD_DIM] for g in range(GROUP)], axis=0)
    lanes4 = lambda x: jnp.concatenate([x] * GROUP, axis=1)
    tq_pos = t0 + lax.broadcasted_iota(jnp.int32, (1, tq), 1)
    tpos = lanes4(tq_pos)

    nr = lax.broadcasted_iota(jnp.int32, (128, 1), 0)
    maskc = ((nr * CMP_STRIDE + (CMP_BLOCK - 1)) <= tpos) & (nr < n_cmp)
    sc = jnp.where(maskc, lax.dot_general(kcmp_ref[0, 0], q4, _NT, preferred_element_type=jnp.float32), NEG_INF)
    ec = jnp.where(maskc, jnp.exp2(sc - jnp.max(sc, axis=0, keepdims=True)), 0.0)
    pc = ec / jnp.maximum(jnp.sum(ec, axis=0, keepdims=True), 1e-30)
    o_cmp = jnp.dot(vcmpt_ref[0, 0], pc.astype(bf), preferred_element_type=jnp.float32)
    psum = pc[:, :tq] + pc[:, tq:2 * tq] + pc[:, 2 * tq:3 * tq] + pc[:, 3 * tq:]
    p_hi = psum.astype(bf)
    p_lo = (psum - p_hi.astype(jnp.float32)).astype(bf)
    imp = (jnp.dot(ovt_ref[...], p_hi, preferred_element_type=jnp.float32)
           + jnp.dot(ovt_ref[...], p_lo, preferred_element_type=jnp.float32))
    nj = -(-n_sb // 8) * 8
    sel = _select_blocks_t(imp[:nj], tq_pos, n_sb)
    sel = jnp.concatenate([sel, jnp.zeros((128 - nj, tq), jnp.float32)], axis=0).astype(bf)

    per_kc = SEL_KC // tq

    def sel_step(c, carry):
        m, l, acc = carry
        k0 = pl.multiple_of(c * SEL_KC, SEL_KC)
        kpos = k0 + lax.broadcasted_iota(jnp.int32, (SEL_KC, 1), 0)
        picked = jnp.dot(et_ref[pl.ds(k0, SEL_KC), :], sel, preferred_element_type=jnp.float32)
        mask = lanes4((picked > 0.5) & (kpos <= tq_pos))
        s = lax.dot_general(ks_ref[0, 0, pl.ds(k0, SEL_KC), :], q4, _NT, preferred_element_type=jnp.float32)
        s = jnp.where(mask, s, NEG_INF)
        m_new = jnp.maximum(m, jnp.max(s, axis=0, keepdims=True))
        alpha = jnp.exp2(m - m_new)
        p = jnp.exp2(s - m_new)
        l = alpha * l + jnp.sum(p, axis=0, keepdims=True)
        vt = jnp.concatenate([vst_ref[0, 0, c * per_kc + r] for r in range(per_kc)], axis=1)
        acc = alpha * acc + jnp.dot(vt, p.astype(bf), preferred_element_type=jnp.float32)
        return m_new, l, acc

    nq = GROUP * tq
    init = (jnp.full((1, nq), NEG_INF, jnp.float32), jnp.zeros((1, nq), jnp.float32),
            jnp.zeros((HEAD_DIM, nq), jnp.float32))
    n_kc = (t0 + tq + SEL_KC - 1) // SEL_KC
    _, l_s, acc_s = lax.fori_loop(0, n_kc, sel_step, init)
    o_sel = acc_s / jnp.maximum(l_s, 1e-30)

    n_wc = WINDOW // tq + 1
    c0 = jnp.maximum(i - WINDOW // tq, 0)
    w0 = pl.multiple_of(c0 * tq, tq)
    d = tpos - (w0 + lax.broadcasted_iota(jnp.int32, (n_wc * tq, 1), 0))
    sw = lax.dot_general(kw_ref[0, 0, pl.ds(w0, n_wc * tq), :], q4, _NT, preferred_element_type=jnp.float32)
    sw = jnp.where((d >= 0) & (d < WINDOW), sw, NEG_INF)
    ew = jnp.exp2(sw - jnp.max(sw, axis=0, keepdims=True))
    vwt = jnp.concatenate([vwt_ref[0, 0, c0 + r] for r in range(n_wc)], axis=1)
    o_win = (jnp.dot(vwt, ew.astype(bf), preferred_element_type=jnp.float32)
             / jnp.maximum(jnp.sum(ew, axis=0, keepdims=True), 1e-30))

    gate = jax.nn.sigmoid(glt_ref[0, 0])
    for g in range(GROUP):
        sl = slice(g * tq, (g + 1) * tq)
        ot = (gate[3 * g:3 * g + 1] * o_cmp[:, sl] + gate[3 * g + 1:3 * g + 2] * o_sel[:, sl]
              + gate[3 * g + 2:3 * g + 3] * o_win[:, sl])
        o_ref[0, :, g * HEAD_DIM:(g + 1) * HEAD_DIM] = ot.T.astype(o_ref.dtype)


def _overlap_matrix(n_cmp, n_sb):
    start = np.arange(128) * CMP_STRIDE
    end = start + CMP_BLOCK - 1
    sb = np.arange(128) * SEL_BLOCK
    ov = (start[:, None] < sb[None, :] + SEL_BLOCK) & (end[:, None] >= sb[None, :])
    ov &= (np.arange(128)[:, None] < n_cmp) & (np.arange(128)[None, :] < n_sb)
    return jnp.asarray(ov, jnp.bfloat16)


def _heads_major(x):
    B, T, W = x.shape
    return x.reshape(B, T, N_KV_HEADS, W // N_KV_HEADS).transpose(0, 2, 1, 3)


def _nsa_prompt_pallas(q, kcmp, vcmp, ks, vs_t, kw, vw_t, gl_t):
    B, T, _ = q.shape
    tq = ATT_TQ
    n_cmp = (T - CMP_BLOCK) // CMP_STRIDE + 1
    n_sb = -(-T // SEL_BLOCK)
    assert T % SEL_KC == 0 and T >= WINDOW + tq and kcmp.shape[1] <= 128 and n_sb <= 128
    kcmp, vcmp = (jnp.pad(a, ((0, 0), (0, 128 - a.shape[1]), (0, 0))) for a in (kcmp, vcmp))
    vcmp_t = vcmp.reshape(B, 128, N_KV_HEADS, HEAD_DIM).transpose(0, 2, 3, 1)
    et = jnp.asarray((np.arange(T)[:, None] // SEL_BLOCK) == np.arange(128)[None, :], jnp.bfloat16)
    k_spec = pl.BlockSpec((1, 1, T, HEAD_DIM), lambda b, k, i: (b, k, 0, 0))
    v_spec = pl.BlockSpec((1, 1, T // tq, HEAD_DIM, tq), lambda b, k, i: (b, k, 0, 0, 0))
    return pl.pallas_call(
        functools.partial(_nsa_prompt_kernel, tq=tq, n_cmp=n_cmp, n_sb=n_sb),
        out_shape=jax.ShapeDtypeStruct((B, T, Q_W), jnp.bfloat16),
        grid=(B, N_KV_HEADS, T // tq),
        in_specs=[pl.BlockSpec((1, tq, GROUP * HEAD_DIM), lambda b, k, i: (b, i, k)),
                  pl.BlockSpec((1, 1, 128, HEAD_DIM), lambda b, k, i: (b, k, 0, 0)),
                  pl.BlockSpec((1, 1, HEAD_DIM, 128), lambda b, k, i: (b, k, 0, 0)),
                  k_spec, v_spec, k_spec, v_spec,
                  pl.BlockSpec((1, 1, 3 * GROUP, tq), lambda b, k, i: (b, k, 0, i)),
                  pl.BlockSpec((128, 128), lambda b, k, i: (0, 0)),
                  pl.BlockSpec((T, 128), lambda b, k, i: (0, 0))],
        out_specs=pl.BlockSpec((1, tq, GROUP * HEAD_DIM), lambda b, k, i: (b, i, k)),
        compiler_params=pltpu.CompilerParams(
            dimension_semantics=("parallel", "parallel", "arbitrary"), vmem_limit_bytes=VMEM_LIMIT_BYTES),
        name="nsa_prompt",
    )(q, _heads_major(kcmp), vcmp_t, ks, vs_t, kw, vw_t, gl_t, _overlap_matrix(n_cmp, n_sb).T, et)


_NT = (((1,), (1,)), ((), ()))
SAMPLE_NB = 2


def _decode_attend(s, mask, s_new, mask_new, pv_fn, v_new):
    sm = jnp.where(mask, s, NEG_INF)
    sn = jnp.where(mask_new, s_new, NEG_INF)
    m = jnp.maximum(jnp.max(sm, axis=1, keepdims=True), sn)
    e = jnp.where(mask, jnp.exp2(sm - m), 0.0)
    en = jnp.where(mask_new, jnp.exp2(sn - m), 0.0)
    l = jnp.sum(e, axis=1, keepdims=True) + en
    acc = pv_fn(e.astype(jnp.bfloat16)) + (en.astype(jnp.bfloat16).astype(jnp.float32)
                                            * v_new.astype(jnp.bfloat16).astype(jnp.float32))
    return acc / jnp.maximum(l, 1e-30)


def _nsa_sample_kernel(pt_ref, *refs, n_pages, n_sb, nb, n_const):
    del pt_ref
    cmp_pages = refs[:nb * n_pages]
    sel_pages_all = refs[nb * n_pages:2 * nb * n_pages]
    rest = refs[2 * nb * n_pages:]
    win_ref, q_ref, gl_ref, ksn_ref, vsn_ref, kwn_ref, vwn_ref, kwc_ref, vwc_ref = rest[:9]
    (wk_ref, wv_ref, pek_ref, pev_ref, w1k_ref, w1v_ref, w2k_ref, w2v_ref, cos_ref, sin_ref, ov_ref, ex_ref,
     perm_ref) = rest[9:9 + n_const]
    o_ref, wout_ref = rest[-2:]
    past = n_pages * PAGE_SIZE
    n_half = past // CMP_STRIDE
    n_cmp = (past + 1 - CMP_BLOCK) // CMP_STRIDE + 1

    groups = PAGE_SIZE // CMP_STRIDE

    k_tiles, v_tiles = [], []
    for p in range(nb * n_pages):
        page = cmp_pages[p][0, 0].reshape(2 * KV_W, PAGE_SIZE).astype(jnp.bfloat16)
        z = lax.dot_general(perm_ref[...], page, _NT, preferred_element_type=jnp.float32)
        for tiles, z_kv in ((k_tiles, z[:, :KV_W]), (v_tiles, z[:, KV_W:])):
            tiles.append(jnp.concatenate([z_kv[j * groups:(j + 1) * groups] for j in range(CMP_STRIDE)], axis=1))
    k_cat = jnp.concatenate(k_tiles, axis=0).astype(jnp.bfloat16)
    v_cat = jnp.concatenate(v_tiles, axis=0).astype(jnp.bfloat16)

    rows = nb * n_half
    valid = (lax.broadcasted_iota(jnp.int32, (rows, KV_W), 0) % n_half) < n_cmp
    kcmp = _compress_rows(k_cat, wk_ref, pek_ref, w1k_ref, w2k_ref, rows)
    kcmp = jnp.where(valid, _rope_lanes(kcmp, cos_ref[...], sin_ref[...]), 0.0).astype(jnp.bfloat16)
    vcmp = _compress_rows(v_cat, wv_ref, pev_ref, w1v_ref, w2v_ref, rows)
    vcmp = jnp.where(valid, vcmp, 0.0).astype(jnp.bfloat16)
    for s in range(nb):
        _nsa_sample_one(s, kcmp[s * n_half:(s + 1) * n_half], vcmp[s * n_half:(s + 1) * n_half],
                        sel_pages_all[s * n_pages:(s + 1) * n_pages], win_ref, q_ref, gl_ref, ksn_ref, vsn_ref,
                        kwn_ref, vwn_ref, kwc_ref, vwc_ref, ov_ref, ex_ref, o_ref, wout_ref, n_pages, n_sb)


def _nsa_sample_one(s, kcmp, vcmp, sel_pages, win_ref, q_ref, gl_ref, ksn_ref, vsn_ref, kwn_ref, vwn_ref,
                    kwc_ref, vwc_ref, ov_ref, ex_ref, o_ref, wout_ref, n_pages, n_sb):
    past = n_pages * PAGE_SIZE
    qpos = past
    n_half = past // CMP_STRIDE
    n_cmp = (past + 1 - CMP_BLOCK) // CMP_STRIDE + 1
    wb = win_ref.shape[-1]
    bf = jnp.bfloat16

    row8 = lax.broadcasted_iota(jnp.int32, (8, KV_W), 0)
    lane8 = lax.broadcasted_iota(jnp.int32, (8, KV_W), 1)
    top1 = lax.broadcasted_iota(jnp.int32, (8, 1), 0) < GROUP
    q8 = q_ref[s].astype(jnp.float32)
    q2 = jnp.where((row8 < GROUP) == (lane8 < HEAD_DIM), jnp.concatenate([q8, q8], axis=1), 0.0).astype(bf)
    q2f = q2.astype(jnp.float32)

    def halves(x):
        return jnp.where(top1, x[:, :HEAD_DIM], x[:, HEAD_DIM:])

    def new_score(k_new):
        return jnp.sum(q2f * k_new.astype(bf).astype(jnp.float32), axis=1, keepdims=True)

    sc = lax.dot_general(q2, kcmp, _NT, preferred_element_type=jnp.float32)
    nl = lax.broadcasted_iota(jnp.int32, (8, n_half), 1)
    pc = _softmax_rows(sc, ((nl * CMP_STRIDE + (CMP_BLOCK - 1)) <= qpos) & (nl < n_cmp))
    o_cmp = halves(jnp.dot(pc.astype(bf), vcmp, preferred_element_type=jnp.float32))
    pk0 = jnp.sum(jnp.where(top1, pc, 0.0), axis=0, keepdims=True)
    pk1 = jnp.sum(jnp.where(top1, 0.0, pc), axis=0, keepdims=True)
    rown = lax.broadcasted_iota(jnp.int32, (8, n_half), 0)
    p2 = jnp.where(rown == 0, pk0, jnp.where(rown == 1, pk1, 0.0))
    p_hi = p2.astype(bf)
    p_lo = (p2 - p_hi.astype(jnp.float32)).astype(bf)
    imp = (jnp.dot(p_hi, ov_ref[...], preferred_element_type=jnp.float32)
           + jnp.dot(p_lo, ov_ref[...], preferred_element_type=jnp.float32))
    sel2 = _select_blocks(imp, jnp.full((8, 1), qpos, jnp.int32), n_sb)

    picked2 = jnp.dot(sel2.astype(bf), ex_ref[...], preferred_element_type=jnp.float32)
    mask_s = jnp.where(top1, picked2[0:1], picked2[1:2]) > 0.5
    seln = jnp.sum(jnp.where(lane8 == qpos // SEL_BLOCK, sel2, 0.0), axis=1, keepdims=True)
    mask_new = jnp.where(top1, seln[0:1], seln[1:2]) > 0.5
    kt = jnp.concatenate([sel_pages[p][0, 0, 0].astype(bf) for p in range(n_pages)], axis=1)
    vt = jnp.concatenate([sel_pages[p][0, 0, 1].astype(bf) for p in range(n_pages)], axis=1)
    s_s = jnp.dot(q2, kt, preferred_element_type=jnp.float32)
    pv_sel = lambda e: lax.dot_general(e, vt, _NT, preferred_element_type=jnp.float32)

    o_sel = halves(_decode_attend(s_s, mask_s, new_score(ksn_ref[s]), mask_new, pv_sel, vsn_ref[s]))

    s_w = jnp.dot(q2, win_ref[0, s, 0].astype(bf), preferred_element_type=jnp.float32)
    kpos = past - wb + lax.broadcasted_iota(jnp.int32, (8, wb), 1)
    mask_w = (qpos - kpos >= 0) & (qpos - kpos < WINDOW) & (kpos >= 0)
    pv_win = lambda e: lax.dot_general(e, win_ref[0, s, 1].astype(bf), _NT, preferred_element_type=jnp.float32)
    o_win = halves(_decode_attend(s_w, mask_w, new_score(kwn_ref[s]), jnp.full((8, 1), True), pv_win, vwn_ref[s]))

    gate = jax.nn.sigmoid(gl_ref[s])
    o_ref[s] = (gate[:, 0:1] * o_cmp + gate[:, 1:2] * o_sel + gate[:, 2:3] * o_win).astype(o_ref.dtype)

    lane_w = lax.broadcasted_iota(jnp.int32, (KV_W, wb), 1)
    wout_ref[s, 0] = jnp.where(lane_w == wb - 1, kwc_ref[s], pltpu.roll(win_ref[0, s, 0], wb - 1, 1))
    wout_ref[s, 1] = jnp.where(lane_w == wb - 1, vwc_ref[s], pltpu.roll(win_ref[0, s, 1], wb - 1, 1))


def _cache_rows_on_lanes(c):
    nd = c.ndim
    c = jnp.moveaxis(c, nd - 4, nd - 1)
    return c.reshape(c.shape[:-3] + (c.shape[-3] * c.shape[-2], c.shape[-1]))


def _nsa_sample_pallas(layer, q, gl, ks, vs, kw, vw, cmpw, cmp_t, sel_t, win_t, page_table, win_out):
    B = q.shape[0]
    n_pages = page_table.shape[1]
    past = n_pages * PAGE_SIZE
    wb = win_t.shape[-1]
    n_half = past // CMP_STRIDE
    n_cmp = (past + 1 - CMP_BLOCK) // CMP_STRIDE + 1
    n_sb = -(-(past + 1) // SEL_BLOCK)
    wk1, pek, wk2, wv1, pev, wv2 = cmpw
    wk, pekf, w1k, w2k = _compress_weights(wk1, pek, wk2)
    wv, pevf, w1v, w2v = _compress_weights(wv1, pev, wv2)
    nb = _pick_tile(B, (SAMPLE_NB,))
    nb = nb if nb == SAMPLE_NB else 1
    cos, sin = _rope_tables(jnp.tile(jnp.arange(n_half) * CMP_STRIDE + CMP_BLOCK - 1, nb), N_KV_HEADS)
    ov = _overlap_matrix(n_cmp, n_sb)[:n_half]
    ex = jnp.asarray((np.arange(past)[None, :] // SEL_BLOCK) == np.arange(128)[:, None], jnp.bfloat16)
    groups = PAGE_SIZE // CMP_STRIDE
    src_row = (np.arange(PAGE_SIZE) % groups) * CMP_STRIDE + np.arange(PAGE_SIZE) // groups
    perm = jnp.asarray(src_row[:, None] == np.arange(PAGE_SIZE)[None, :], jnp.bfloat16)
    consts = (wk, wv, pekf, pevf, w1k, w1v, w2k, w2v, cos, sin, ov, ex, perm)
    row3 = lambda x: x.reshape(B, 1, KV_W)
    col3 = lambda x: x.reshape(B, KV_W, 1)
    per_b = (q.reshape(B, N_HEADS, HEAD_DIM), gl.reshape(B, N_HEADS, 3), row3(ks), row3(vs), row3(kw), row3(vw),
             col3(kw), col3(vw))
    page_spec = lambda s, p: pl.BlockSpec((1, 1, 2, KV_W, PAGE_SIZE),
                                          lambda b, pt: (layer, pt[b * nb + s, p], 0, 0, 0))
    b_spec = lambda a: pl.BlockSpec((nb,) + a.shape[1:], lambda b, pt: (b,) + (0,) * (a.ndim - 1))
    full = lambda a: pl.BlockSpec(a.shape, lambda b, pt: (0,) * a.ndim)
    in_specs = ([page_spec(s, p) for s in range(nb) for p in range(n_pages)] * 2
                + [pl.BlockSpec((1, nb, 2, KV_W, wb), lambda b, pt: (layer, b, 0, 0, 0))]
                + [b_spec(a) for a in per_b] + [full(a) for a in consts])
    args = (page_table, *([cmp_t] * (nb * n_pages)), *([sel_t] * (nb * n_pages)), win_t, *per_b, *consts)
    aliases = {}
    if win_out is not None:
        in_specs.append(pl.BlockSpec(memory_space=pl.ANY))
        aliases = {len(args): 1}
        args += (win_out,)
    return pl.pallas_call(
        functools.partial(_nsa_sample_kernel, n_pages=n_pages, n_sb=n_sb, nb=nb, n_const=len(consts)),
        out_shape=(jax.ShapeDtypeStruct((B, N_HEADS, HEAD_DIM), jnp.bfloat16),
                   jax.ShapeDtypeStruct(win_t.shape, jnp.float32)),
        grid_spec=pltpu.PrefetchScalarGridSpec(
            num_scalar_prefetch=1, grid=(B // nb,), in_specs=in_specs,
            out_specs=(pl.BlockSpec((nb, N_HEADS, HEAD_DIM), lambda b, pt: (b, 0, 0)),
                       pl.BlockSpec((None, nb, 2, KV_W, wb), lambda b, pt: (layer, b, 0, 0, 0)))),
        input_output_aliases=aliases,
        compiler_params=pltpu.CompilerParams(dimension_semantics=("arbitrary",), vmem_limit_bytes=VMEM_LIMIT_BYTES),
        name="nsa_sample",
    )(*args)


SSM_N = SSM_GROUPS * SSM_STATE
SSM_LANE_BLK = 512
SSM_TL = 64


def _ssm_kernel(u_ref, h0r_ref, h0i_ref, ar_ref, ai_ref, bm_ref, cr_ref, ci_ref, d_ref,
                y_ref, hr_ref, hi_ref, xr_s, xi_s, *, tl, nb):
    c = pl.program_id(0)

    @pl.when(c == 0)
    def _():
        hr_ref[...] = h0r_ref[...]
        hi_ref[...] = h0i_ref[...]

    u = u_ref[...]
    ub = u.astype(jnp.bfloat16)
    n_grp = SSM_WIDTH // 128
    for j in range(n_grp):
        bu = jnp.dot(ub[:, 128 * j:128 * (j + 1)], bm_ref[j], preferred_element_type=jnp.float32)
        xr_s[:, 512 * j:512 * (j + 1)] = bu[:, :512]
        xi_s[:, 512 * j:512 * (j + 1)] = bu[:, 512:]

    for lb in range(SSM_N // SSM_LANE_BLK):
        sl = slice(lb * SSM_LANE_BLK, (lb + 1) * SSM_LANE_BLK)
        ar = jnp.broadcast_to(ar_ref[:, sl], (8, SSM_LANE_BLK))
        ai = jnp.broadcast_to(ai_ref[:, sl], (8, SSM_LANE_BLK))
        for r in range(nb // 8):
            def step(t, carry):
                hr, hi = carry
                row = pl.multiple_of(t * nb + r * 8, 8)
                xr = xr_s[pl.ds(row, 8), sl]
                xi = xi_s[pl.ds(row, 8), sl]
                nr = ar * hr - ai * hi + xr
                ni = ar * hi + ai * hr + xi
                xr_s[pl.ds(row, 8), sl] = nr
                xi_s[pl.ds(row, 8), sl] = ni
                return nr, ni

            hr, hi = lax.fori_loop(0, tl, step, (hr_ref[r * 8:(r + 1) * 8, sl], hi_ref[r * 8:(r + 1) * 8, sl]))
            hr_ref[r * 8:(r + 1) * 8, sl] = hr
            hi_ref[r * 8:(r + 1) * 8, sl] = hi

    for j in range(n_grp):
        yr = jnp.dot(xr_s[:, 512 * j:512 * (j + 1)].astype(jnp.bfloat16), cr_ref[j], preferred_element_type=jnp.float32)
        yi = jnp.dot(xi_s[:, 512 * j:512 * (j + 1)].astype(jnp.bfloat16), ci_ref[j], preferred_element_type=jnp.float32)
        y_ref[:, 128 * j:128 * (j + 1)] = yr - yi + d_ref[:, 128 * j:128 * (j + 1)] * u[:, 128 * j:128 * (j + 1)]


def _ssm_params(a_re, a_im, log_dt, b_re, b_im, c_re, c_im, d_skip):
    dt = jnp.exp(log_dt)[:, None]
    mag = jnp.exp(dt * a_re)
    ab_re = mag * jnp.cos(dt * a_im)
    ab_im = mag * jnp.sin(dt * a_im)
    den = a_re * a_re + a_im * a_im
    zr = ((ab_re - 1.0) * a_re + ab_im * a_im) / den
    zi = (ab_im * a_re - (ab_re - 1.0) * a_im) / den
    bb_re = zr[..., None] * b_re - zi[..., None] * b_im
    bb_im = zr[..., None] * b_im + zi[..., None] * b_re
    n_grp = SSM_WIDTH // 128
    gpl = 128 // SSM_GROUP
    eye = jnp.eye(gpl, dtype=jnp.float32)

    def b_blocks(bb):
        x = bb.reshape(n_grp, gpl, SSM_STATE, SSM_GROUP)
        return jnp.einsum('jgpc,gh->jgchp', x, eye).reshape(n_grp, 128, gpl * SSM_STATE)

    def c_blocks(cc):
        x = cc.reshape(n_grp, gpl, SSM_GROUP, SSM_STATE)
        return jnp.einsum('jgcp,gh->jgphc', x, eye).reshape(n_grp, gpl * SSM_STATE, 128)

    bm = jnp.concatenate([b_blocks(bb_re), b_blocks(bb_im)], axis=-1).astype(jnp.bfloat16)
    return (ab_re.reshape(1, SSM_N), ab_im.reshape(1, SSM_N), bm,
            c_blocks(c_re).astype(jnp.bfloat16), c_blocks(c_im).astype(jnp.bfloat16), d_skip.reshape(1, SSM_WIDTH))


def _ssm_pallas(u_tb, B, T, h0_re, h0_im, params):
    ab_re, ab_im, bm, cr, ci, d = params
    tl = _pick_tile(T, (SSM_TL,))
    full = lambda a: pl.BlockSpec(a.shape, lambda c: (0,) * a.ndim)
    h0r = h0_re.reshape(B, SSM_N)
    h0i = h0_im.reshape(B, SSM_N)
    consts = (h0r, h0i, ab_re, ab_im, bm, cr, ci, d)
    y, hr, hi = pl.pallas_call(
        functools.partial(_ssm_kernel, tl=tl, nb=B),
        out_shape=(jax.ShapeDtypeStruct((T * B, SSM_WIDTH), jnp.float32),
                   jax.ShapeDtypeStruct((B, SSM_N), jnp.float32), jax.ShapeDtypeStruct((B, SSM_N), jnp.float32)),
        grid=(T // tl,),
        in_specs=[pl.BlockSpec((tl * B, SSM_WIDTH), lambda c: (c, 0))] + [full(a) for a in consts],
        out_specs=(pl.BlockSpec((tl * B, SSM_WIDTH), lambda c: (c, 0)),
                   pl.BlockSpec((B, SSM_N), lambda c: (0, 0)), pl.BlockSpec((B, SSM_N), lambda c: (0, 0))),
        scratch_shapes=[pltpu.VMEM((tl * B, SSM_N), jnp.float32), pltpu.VMEM((tl * B, SSM_N), jnp.float32)],
        compiler_params=pltpu.CompilerParams(dimension_semantics=("arbitrary",), vmem_limit_bytes=VMEM_LIMIT_BYTES),
        name="ssm",
    )(u_tb, *consts)
    return y, hr.reshape(B, SSM_GROUPS, SSM_STATE), hi.reshape(B, SSM_GROUPS, SSM_STATE)


def _block(h, p_l, pos, lw, layer, sample, h0_re, h0_im, conv_prefix):
    (g_attn, w_in, g_q, g_kc, g_ks, g_kw, wk1, pek, wk2, wv1, pev, wv2,
     a_re, a_im, log_dt, b_re, b_im, c_re, c_im, d_skip,
     w_a, w_glu1, w_glu2, w_o, g_ffn, w_up, conv_w, conv_b, w_down,
     g_ple, w_ple_gate, w_ple) = lw
    B, T = p_l.shape[:2]
    N = B * T
    qs, rows_cmp, rows_sel, rows_win, gl, u, gab, *attn = _in_proj_pallas(h, w_in, layer, g_attn, g_q, g_kc, g_ks,
                                                                           g_kw, pos, T)
    gl = gl[:, :3 * N_HEADS]
    cmpw = (wk1, pek, wk2, wv1, pev, wv2)
    as_rows = lambda a: a.reshape(B, T, 2, N_KV_HEADS, HEAD_DIM)
    if sample is None:
        kcmp, vcmp = _compress_pallas(rows_cmp.reshape(B, T, 2 * KV_W), cmpw)
        o = _nsa_prompt_pallas(qs.reshape(B, T, Q_W), kcmp, vcmp, *attn).reshape(N, Q_W)
        n_keep = min(WINDOW, T)
        rows = (as_rows(rows_cmp), as_rows(rows_sel), as_rows(rows_win)[:, T - n_keep:])
    else:
        assert T == 1
        cmp_t, sel_t, win_t, page_table, win_out = sample
        o, win_out = _nsa_sample_pallas(layer, qs, gl, rows_sel[:, :KV_W], rows_sel[:, KV_W:], rows_win[:, :KV_W],
                                        rows_win[:, KV_W:], cmpw, cmp_t, sel_t, win_t, page_table, win_out)
        o = o.reshape(N, Q_W)
        rows = (as_rows(rows_cmp), as_rows(rows_sel), win_out)
    y, hr, hi = _ssm_pallas(u.reshape(T * B, SSM_WIDTH), B, T, h0_re, h0_im,
                            _ssm_params(a_re, a_im, log_dt, b_re, b_im, c_re, c_im, d_skip))
    h, conv_rows = _mix_ffn_pallas(h, o, y.reshape(u.shape), gab, p_l.reshape(N, -1), conv_prefix,
                                   (w_a, w_glu1, w_glu2, w_o, g_ffn, w_up, conv_w, conv_b, w_down,
                                    g_ple, w_ple_gate, w_ple), layer, T)
    return h, rows, hr, hi, conv_rows


def kernel(x_prompt, x_sample, cache_cmp, cache_sel, cache_win, state_ssm_re, state_ssm_im, state_conv, page_table, p_prompt, p_sample, g_attn, w_in, g_q, g_kc, g_ks, g_kw, cmp_wk1, cmp_pek, cmp_wk2, cmp_wv1, cmp_pev, cmp_wv2, ssm_a_re, ssm_a_im, ssm_log_dt, ssm_b_re, ssm_b_im, ssm_c_re, ssm_c_im, ssm_d, w_a, w_glu1, w_glu2, w_o, g_ffn, w_up, conv_w, conv_b, w_down, g_ple, w_ple_gate, w_ple):
    Bp, Tp = x_prompt.shape[:2]
    Ts = x_sample.shape[1]
    depth = w_in.shape[0]
    past = page_table.shape[1] * PAGE_SIZE
    pos_p = jnp.arange(Tp)
    pos_s = past + jnp.arange(Ts)
    zeros_h = jnp.zeros((Bp, SSM_GROUPS, SSM_STATE), x_prompt.dtype)
    whole = lambda w: (w.astype(jnp.bfloat16),)
    layer_w = (g_attn, (_pad_w_in(w_in),), g_q, g_kc, g_ks, g_kw, cmp_wk1, cmp_pek, cmp_wk2, cmp_wv1, cmp_pev, cmp_wv2,
               ssm_a_re, ssm_a_im, ssm_log_dt, ssm_b_re, ssm_b_im, ssm_c_re, ssm_c_im, ssm_d,
               whole(w_a), whole(w_glu1), whole(w_glu2), whole(w_o), g_ffn, whole(w_up), conv_w, conv_b, whole(w_down),
               g_ple, whole(w_ple_gate), whole(w_ple))
    cmp_t, sel_t, win_t = (_cache_rows_on_lanes(c) for c in (cache_cmp, cache_sel, cache_win))
    st = [[] for _ in range(12)]
    hp, hs = x_prompt.reshape(Bp * Tp, D_MODEL), x_sample.reshape(-1, D_MODEL)
    win_out = None
    for i in range(depth):
        lw = [w[0] if isinstance(w, tuple) else w[i] for w in layer_w]
        hp, rows, hr, hi, cv = _block(hp, p_prompt[i], pos_p, lw, i, None, zeros_h, zeros_h, None)
        for j, a in enumerate(list(rows) + [hr, hi, cv]):
            st[j].append(a)
        hs, rows, hr, hi, cv = _block(hs, p_sample[i], pos_s, lw, i, (cmp_t, sel_t, win_t, page_table, win_out),
                                      state_ssm_re[i], state_ssm_im[i], state_conv[i])
        win_out = rows[2]
        for j, a in enumerate(list(rows) + [hr, hi, cv]):
            st[6 + j].append(a)
    outs = [None if j == 8 else jnp.stack(s) for j, s in enumerate(st)]
    wb = win_out.shape[-1]
    outs[8] = win_out.reshape(win_out.shape[:3] + (N_KV_HEADS, HEAD_DIM, wb)).transpose(0, 1, 5, 2, 3, 4)
    return (hp.reshape(x_prompt.shape), hs.reshape(x_sample.shape)) + tuple(outs)
```

```python
import functools
import math

import numpy as np
import jax
import jax.numpy as jnp
from jax import lax
from jax.experimental import pallas as pl
from jax.experimental.pallas import tpu as pltpu

D_MODEL = 1024
N_HEADS = 8
N_KV_HEADS = 2
HEAD_DIM = 64
GROUP = N_HEADS // N_KV_HEADS
Q_W = N_HEADS * HEAD_DIM
KV_W = N_KV_HEADS * HEAD_DIM
CMP_BLOCK = 32
CMP_STRIDE = 16
SEL_BLOCK = 64
N_SEL = 8
WINDOW = 512
PAGE_SIZE = 128
ROPE_THETA = 10000.0
SSM_WIDTH = D_MODEL // 2
SSM_GROUP = 16
SSM_GROUPS = SSM_WIDTH // SSM_GROUP
SSM_STATE = 64
D_FF = 11 * D_MODEL // 4
CONV_W = 3
EPS = 1e-6
NEG_INF = -1e30
FORCE_SCORE = 1e9
SCALE = HEAD_DIM ** -0.5
QK_SCALE = SCALE * math.log2(math.e)

VMEM_LIMIT_BYTES = 56 * 1024 * 1024


def _pick_tile(n, cands):
    for c in cands:
        if n % c == 0:
            return c
    return n


def _rms(x, g):
    return x * lax.rsqrt(jnp.mean(x * x, axis=-1, keepdims=True) + EPS) * g


def _bdot(a, b):
    return jnp.dot(a.astype(jnp.bfloat16), b, preferred_element_type=jnp.float32)


def _layer_spec(stack, layer):
    nd = stack.ndim
    return pl.BlockSpec((None,) + stack.shape[1:], lambda *_: (layer,) + (0,) * (nd - 1),
                        pipeline_mode=pl.Buffered(1))


GL_PAD = 128
_IN_WIDTHS = (Q_W, 6 * KV_W, GL_PAD, SSM_WIDTH, 2 * D_MODEL)
_IN_OFFS = tuple(int(v) for v in np.cumsum((0,) + _IN_WIDTHS))
ROW_TILE = 256
IN_TILE = 512


def _in_proj_kernel(h_ref, g_ref, w_ref, gq_ref, gk_ref, ones_ref, cos_ref, sin_ref,
                    q_ref, cmp_ref, sel_ref, win_ref, gl_ref, u_ref, gab_ref, *attn_refs):
    xn = _rms(h_ref[...], g_ref[...]).astype(jnp.bfloat16)
    seg = lambda s: jnp.dot(xn, w_ref[:, _IN_OFFS[s]:_IN_OFFS[s + 1]], preferred_element_type=jnp.float32)
    cos, sin = cos_ref[...], sin_ref[...]
    ones = ones_ref[...]

    def head_norm(x, gain):
        x2 = x * x
        hi = x2.astype(jnp.bfloat16)
        lo = (x2 - hi.astype(jnp.float32)).astype(jnp.bfloat16)
        ss = (jnp.dot(hi, ones, preferred_element_type=jnp.float32)
              + jnp.dot(lo, ones, preferred_element_type=jnp.float32))
        return x * lax.rsqrt(ss * (1.0 / HEAD_DIM) + EPS) * gain

    zq = seg(0)
    for c in range(Q_W // KV_W):
        qn = _rope_lanes(head_norm(zq[:, c * KV_W:(c + 1) * KV_W], gq_ref[...]), cos, sin)
        q_ref[:, c * KV_W:(c + 1) * KV_W] = (qn * QK_SCALE).astype(q_ref.dtype)
    zkv = seg(1)
    part = lambda c: zkv[:, c * KV_W:(c + 1) * KV_W]
    cmp_ref[:, :KV_W] = head_norm(part(0), gk_ref[0:1])
    cmp_ref[:, KV_W:] = part(1)
    sel_ref[:, :KV_W] = _rope_lanes(head_norm(part(2), gk_ref[1:2]), cos, sin)
    sel_ref[:, KV_W:] = part(3)
    win_ref[:, :KV_W] = _rope_lanes(head_norm(part(4), gk_ref[2:3]), cos, sin)
    win_ref[:, KV_W:] = part(5)
    gl = seg(2)
    gl_ref[...] = gl
    u_ref[...] = seg(3)
    gab_ref[...] = seg(4)
    if attn_refs:
        ksb_ref, vst_ref, kwb_ref, vwt_ref, glt_ref = attn_refs
        tq = vst_ref.shape[-1]
        for k_ref, v_ref, src in ((ksb_ref, vst_ref, sel_ref), (kwb_ref, vwt_ref, win_ref)):
            kb = src[:, :KV_W].astype(jnp.bfloat16)
            for hd in range(N_KV_HEADS):
                k_ref[0, hd] = kb[:, hd * HEAD_DIM:(hd + 1) * HEAD_DIM]
            for r in range(src.shape[0] // tq):
                vt = src[r * tq:(r + 1) * tq, KV_W:].T.astype(jnp.bfloat16)
                for hd in range(N_KV_HEADS):
                    v_ref[0, hd, r] = vt[hd * HEAD_DIM:(hd + 1) * HEAD_DIM]
        glt = gl.T
        for hd in range(N_KV_HEADS):
            glt_ref[0, hd] = glt[hd * 3 * GROUP:(hd + 1) * 3 * GROUP]


def _pad_w_in(w_in):
    a = Q_W + 6 * KV_W + 3 * N_HEADS
    pad = jnp.zeros(w_in.shape[:-1] + (GL_PAD - 3 * N_HEADS,), w_in.dtype)
    return jnp.concatenate([w_in[..., :a], pad, w_in[..., a:]], axis=-1).astype(jnp.bfloat16)


def _in_proj_pallas(h2d, w_in_p, layer, g_attn, g_q, g_kc, g_ks, g_kw, pos, seq_len):
    N = h2d.shape[0]
    T = seq_len
    B = N // T
    tm = _pick_tile(N, (IN_TILE, ROW_TILE, 128))
    nt = max(T // tm, 1)
    assert T == 1 or T % tm == 0
    cos, sin = _rope_tables(pos, N_KV_HEADS)
    if T == 1:
        cos, sin = (jnp.broadcast_to(t, (tm, KV_W)) for t in (cos, sin))
    tile2 = lambda g: jnp.tile(g.reshape(1, HEAD_DIM), (1, N_KV_HEADS))
    gk = jnp.concatenate([tile2(g_kc), tile2(g_ks), tile2(g_kw), jnp.zeros((5, KV_W), jnp.float32)], axis=0)
    ones = jnp.asarray(np.kron(np.eye(N_KV_HEADS), np.ones((HEAD_DIM, HEAD_DIM))), jnp.bfloat16)
    row = lambda w: pl.BlockSpec((tm, w), lambda i: (i, 0))
    const = lambda a: pl.BlockSpec(a.shape, lambda i: (0,) * a.ndim, pipeline_mode=pl.Buffered(1))
    tab = pl.BlockSpec((tm, KV_W), lambda i: (i % nt, 0))
    if T == 1:
        u_shape, u_spec = (N, SSM_WIDTH), row(SSM_WIDTH)
    else:
        u_shape, u_spec = (T, B * SSM_WIDTH), pl.BlockSpec((tm, SSM_WIDTH), lambda i: (i % nt, i // nt))
    consts = (g_attn.reshape(1, D_MODEL), w_in_p, tile2(g_q), gk, ones)
    f32, bf = jnp.float32, jnp.bfloat16
    out_shape = ((jax.ShapeDtypeStruct((N, Q_W), bf),) + (jax.ShapeDtypeStruct((N, 2 * KV_W), f32),) * 3
                 + (jax.ShapeDtypeStruct((N, GL_PAD), f32), jax.ShapeDtypeStruct(u_shape, f32),
                    jax.ShapeDtypeStruct((N, 2 * D_MODEL), f32)))
    out_specs = (row(Q_W), row(2 * KV_W), row(2 * KV_W), row(2 * KV_W), row(GL_PAD), u_spec, row(2 * D_MODEL))
    if T > 1:
        tq = ATT_TQ
        assert tm % tq == 0
        k_shape = jax.ShapeDtypeStruct((B, N_KV_HEADS, T, HEAD_DIM), bf)
        k_spec = pl.BlockSpec((1, N_KV_HEADS, tm, HEAD_DIM), lambda i: (i // nt, 0, i % nt, 0))
        v_shape = jax.ShapeDtypeStruct((B, N_KV_HEADS, T // tq, HEAD_DIM, tq), bf)
        v_spec = pl.BlockSpec((1, N_KV_HEADS, tm // tq, HEAD_DIM, tq), lambda i: (i // nt, 0, i % nt, 0, 0))
        g_shape = jax.ShapeDtypeStruct((B, N_KV_HEADS, 3 * GROUP, T), f32)
        g_spec = pl.BlockSpec((1, N_KV_HEADS, 3 * GROUP, tm), lambda i: (i // nt, 0, 0, i % nt))
        out_shape += (k_shape, v_shape, k_shape, v_shape, g_shape)
        out_specs += (k_spec, v_spec, k_spec, v_spec, g_spec)
    return pl.pallas_call(
        _in_proj_kernel,
        out_shape=out_shape,
        grid=(N // tm,),
        in_specs=[row(D_MODEL)] + [_layer_spec(a, layer) if a.ndim == 3 else const(a) for a in consts] + [tab, tab],
        out_specs=out_specs,
        compiler_params=pltpu.CompilerParams(dimension_semantics=("parallel",), vmem_limit_bytes=VMEM_LIMIT_BYTES),
        name="in_proj",
    )(h2d, *consts, cos, sin)


FF_CHUNK = D_FF // 2
assert FF_CHUNK % 128 == 0


def _mix_ffn_kernel(h_ref, o_ref, y_ref, gab_ref, p_ref, pre0_ref, pre1_ref,
                    wa_ref, wg1_ref, wg2_ref, wo_ref, gffn_ref, wup_ref, cw_ref, cb_ref, wdn_ref,
                    gple_ref, wpg_ref, wpl_ref, hout_ref, cs0_ref, cs1_ref, carry_s, *, seq_tiles):
    tm = h_ref.shape[0]
    a_out = jnp.dot(o_ref[...], wa_ref[...], preferred_element_type=jnp.float32)
    yg = jax.nn.gelu(y_ref[...]).astype(jnp.bfloat16)
    b_out = (jnp.dot(yg, wg1_ref[...], preferred_element_type=jnp.float32)
             * jax.nn.sigmoid(jnp.dot(yg, wg2_ref[...], preferred_element_type=jnp.float32)))
    mixed = (jax.nn.sigmoid(gab_ref[:, :D_MODEL]) * a_out + jax.nn.sigmoid(gab_ref[:, D_MODEL:]) * b_out)
    h1 = h_ref[...] + _bdot(mixed, wo_ref[...])

    xn = _rms(h1, gffn_ref[...]).astype(jnp.bfloat16)
    if seq_tiles:
        @pl.when(pl.program_id(0) % seq_tiles == 0)
        def _():
            carry_s[...] = jnp.zeros_like(carry_s)
        row = lax.broadcasted_iota(jnp.int32, (tm, 1), 0)
    ffn = jnp.zeros((tm, D_MODEL), jnp.float32)
    for c in range(D_FF // FF_CHUNK):
        sl = slice(c * FF_CHUNK, (c + 1) * FF_CHUNK)
        gp = jnp.dot(xn, wup_ref[:, sl], preferred_element_type=jnp.float32)
        val = jnp.dot(xn, wup_ref[:, D_FF + c * FF_CHUNK:D_FF + (c + 1) * FF_CHUNK], preferred_element_type=jnp.float32)
        if seq_tiles:
            old1, old2 = carry_s[7:8, sl], carry_s[6:7, sl]
            prev1 = jnp.where(row == 0, old1, pltpu.roll(gp, 1, 0))
            prev2 = jnp.where(row == 0, old2, jnp.where(row == 1, old1, pltpu.roll(gp, 2, 0)))
            carry_s[:, sl] = gp[tm - 8:, :]
            cs0_ref[0, :, sl] = gp[tm - 2:tm - 1, :]
            cs1_ref[0, :, sl] = gp[tm - 1:tm, :]
        else:
            prev2, prev1 = pre0_ref[:, sl], pre1_ref[:, sl]
            cs0_ref[:, sl] = prev1
            cs1_ref[:, sl] = gp
        conv = cb_ref[:, sl] + cw_ref[0:1, sl] * prev2 + cw_ref[1:2, sl] * prev1 + cw_ref[2:3, sl] * gp
        ffn = ffn + _bdot(jax.nn.gelu(conv) * val, wdn_ref[sl, :])
    h2 = h1 + ffn

    gate = jax.nn.sigmoid(_bdot(_rms(h2, gple_ref[...]), wpg_ref[...]))
    hout_ref[...] = h2 + gate * _bdot(p_ref[...], wpl_ref[...])


def _mix_ffn_pallas(h2d, o2d, y, gab, p2d, prefix, w, layer, seq_len):
    (w_a, w_glu1, w_glu2, w_o, g_ffn, w_up, conv_w, conv_b, w_down, g_ple, w_ple_gate, w_ple) = w
    N = h2d.shape[0]
    T = seq_len
    B = N // T
    tm = _pick_tile(N, (ROW_TILE, 128))
    nt = max(T // tm, 1)
    seq = prefix is None
    assert (seq and T % tm == 0 and tm >= 8) or (not seq and T == 1)
    row = lambda wd: pl.BlockSpec((tm, wd), lambda i: (i, 0))
    const = lambda a: pl.BlockSpec(a.shape, lambda i: (0,) * a.ndim, pipeline_mode=pl.Buffered(1))
    f32 = jnp.float32
    if seq:
        y_spec = pl.BlockSpec((tm, SSM_WIDTH), lambda i: (i % nt, i // nt))
        pre = (jnp.zeros((8, D_FF), f32),) * 2
        pre_spec = const(pre[0])
        cs_shape = jax.ShapeDtypeStruct((B, 1, D_FF), f32)
        cs_spec = pl.BlockSpec((1, 1, D_FF), lambda i: (i // nt, 0, 0))
    else:
        y_spec = row(SSM_WIDTH)
        pre = (prefix[:, 0], prefix[:, 1])
        pre_spec = row(D_FF)
        cs_shape = jax.ShapeDtypeStruct((N, D_FF), f32)
        cs_spec = row(D_FF)
    vec = lambda a: a.reshape(1, -1)
    cw8 = jnp.concatenate([conv_w, jnp.zeros((8 - CONV_W, D_FF), f32)], axis=0)
    consts = (w_a, w_glu1, w_glu2, w_o, vec(g_ffn), w_up, cw8, vec(conv_b), w_down, vec(g_ple), w_ple_gate, w_ple)
    spec = lambda a: _layer_spec(a, layer) if a.ndim == 3 else const(a)
    hout, cs0, cs1 = pl.pallas_call(
        functools.partial(_mix_ffn_kernel, seq_tiles=nt if seq else 0),
        out_shape=(jax.ShapeDtypeStruct((N, D_MODEL), f32), cs_shape, cs_shape),
        grid=(N // tm,),
        in_specs=[row(D_MODEL), row(Q_W), y_spec, row(2 * D_MODEL),
                  pl.BlockSpec((None, tm, p2d.shape[-1]), lambda i: (layer, i, 0)), pre_spec, pre_spec]
        + [spec(a) for a in consts],
        out_specs=(row(D_MODEL), cs_spec, cs_spec),
        scratch_shapes=[pltpu.VMEM((8, D_FF), f32)],
        compiler_params=pltpu.CompilerParams(dimension_semantics=("arbitrary",), vmem_limit_bytes=VMEM_LIMIT_BYTES),
        name="mix_ffn",
    )(h2d, o2d, y, gab, p2d, *pre, *consts)
    return hout, jnp.stack([cs0.reshape(B, D_FF), cs1.reshape(B, D_FF)], axis=1)


HALF_ROWS = CMP_BLOCK // CMP_STRIDE
assert HALF_ROWS == 2


def _rope_lanes(x, cos, sin_signed):
    w = x.shape[-1]
    half = HEAD_DIM // 2
    lane = lax.broadcasted_iota(jnp.int32, x.shape, x.ndim - 1)
    first = (lane % HEAD_DIM) < half
    partner = jnp.where(first, pltpu.roll(x, w - half, x.ndim - 1), pltpu.roll(x, half, x.ndim - 1))
    return x * cos + partner * sin_signed


def _compress_rows(xcat, w_ref, pe_ref, w1_ref, w2_ref, n_half):
    acc = jnp.dot(xcat, w_ref[...], preferred_element_type=jnp.float32)
    pa = acc[:, :256]
    pb = pltpu.roll(acc[:, 256:], n_half - 1, 0)
    bias = jnp.dot(pe_ref[...].astype(jnp.bfloat16), w1_ref[...], preferred_element_type=jnp.float32)[0:1]
    bias2 = jnp.concatenate([bias, bias], axis=1)
    hdn = jax.nn.gelu(pa + pb + bias2)
    return jnp.dot(hdn.astype(jnp.bfloat16), w2_ref[...], preferred_element_type=jnp.float32)


def _compress_kernel(xk_ref, xv_ref, wk_ref, wv_ref, pek_ref, pev_ref, w1k_ref, w1v_ref, w2k_ref, w2v_ref,
                     cos_ref, sin_ref, ko_ref, vo_ref, *, n_half, n_cmp):
    row = lax.broadcasted_iota(jnp.int32, (n_half, KV_W), 0)
    gather = lambda x_ref: jnp.concatenate(
        [x_ref[0, pl.ds(j, n_half, stride=CMP_STRIDE), :].astype(jnp.bfloat16) for j in range(CMP_STRIDE)], axis=1)
    k = _compress_rows(gather(xk_ref), wk_ref, pek_ref, w1k_ref, w2k_ref, n_half)
    k = _rope_lanes(k, cos_ref[...], sin_ref[...])
    v = _compress_rows(gather(xv_ref), wv_ref, pev_ref, w1v_ref, w2v_ref, n_half)
    ko_ref[0] = jnp.where(row < n_cmp, k, 0.0).astype(ko_ref.dtype)
    vo_ref[0] = jnp.where(row < n_cmp, v, 0.0).astype(vo_ref.dtype)


def _blockdiag2(w):
    z = jnp.zeros_like(w)
    return jnp.concatenate([jnp.concatenate([w, z], axis=-1), jnp.concatenate([z, w], axis=-1)], axis=-2)


def _compress_weights(w1, pe, w2):
    bd = _blockdiag2(w1)
    wcat = jnp.concatenate([bd[:CMP_STRIDE], bd[CMP_STRIDE:]], axis=-1)
    wcat = wcat.reshape(CMP_STRIDE * KV_W, -1).astype(jnp.bfloat16)
    pe_flat = jnp.broadcast_to(pe.reshape(1, -1), (8, pe.size))
    w1_flat = w1.reshape(-1, w1.shape[-1]).astype(jnp.bfloat16)
    w2bd = _blockdiag2(w2).astype(jnp.bfloat16)
    return wcat, pe_flat, w1_flat, w2bd


def _rope_tables(pos, reps):
    half = HEAD_DIM // 2
    inv = jnp.float32(ROPE_THETA) ** (-jnp.arange(half, dtype=jnp.float32) / half)
    ang = pos.astype(jnp.float32)[:, None] * inv[None, :]
    cos = jnp.cos(ang)
    sin = jnp.sin(ang)
    return (jnp.tile(jnp.concatenate([cos, cos], axis=-1), (1, reps)),
            jnp.tile(jnp.concatenate([-sin, sin], axis=-1), (1, reps)))


def _compress_pallas(rows, cmpw):
    wk1, pek, wk2, wv1, pev, wv2 = cmpw
    B, L, _ = rows.shape
    n_half = L // CMP_STRIDE
    n_cmp = n_half - 1
    wk, pekf, w1k, w2k = _compress_weights(wk1, pek, wk2)
    wv, pevf, w1v, w2v = _compress_weights(wv1, pev, wv2)
    end = jnp.arange(n_half) * CMP_STRIDE + CMP_BLOCK - 1
    cos, sin = _rope_tables(end, N_KV_HEADS)
    full = lambda a: pl.BlockSpec(a.shape, lambda b: (0,) * a.ndim)
    consts = (wk, wv, pekf, pevf, w1k, w1v, w2k, w2v, cos, sin)
    return pl.pallas_call(
        functools.partial(_compress_kernel, n_half=n_half, n_cmp=n_cmp),
        out_shape=(jax.ShapeDtypeStruct((B, n_half, KV_W), jnp.bfloat16),) * 2,
        grid=(B,),
        in_specs=[pl.BlockSpec((1, L, KV_W), lambda b: (b, 0, 0)), pl.BlockSpec((1, L, KV_W), lambda b: (b, 0, 1))]
        + [full(a) for a in consts],
        out_specs=(pl.BlockSpec((1, n_half, KV_W), lambda b: (b, 0, 0)),) * 2,
        compiler_params=pltpu.CompilerParams(dimension_semantics=("parallel",), vmem_limit_bytes=VMEM_LIMIT_BYTES),
        name="compress",
    )(rows, rows, *consts)


ATT_TQ = 256
SEL_KC = 512
BIG_NEG = -3.0e38


def _softmax_rows(s, mask):
    s = jnp.where(mask, s, NEG_INF)
    m = jnp.max(s, axis=-1, keepdims=True)
    e = jnp.where(mask, jnp.exp2(s - m), 0.0)
    return e / jnp.maximum(jnp.sum(e, axis=-1, keepdims=True), 1e-30)


def _select_blocks(imp, tpos, n_sb):
    rows = imp.shape[0]
    jl = lax.broadcasted_iota(jnp.int32, (rows, 128), 1)
    forced = (jl == 0) | (jl == (tpos >> 6))
    imp = jnp.where(forced, FORCE_SCORE, imp)
    imp = jnp.where(jl * SEL_BLOCK <= tpos, imp, NEG_INF)
    imp = jnp.where(jl < n_sb, imp, BIG_NEG)
    beaten_by = jnp.zeros((rows, 128), jnp.float32)
    for i in range(n_sb):
        col = imp[:, i:i + 1]
        beaten_by = beaten_by + jnp.where((col > imp) | ((col == imp) & (i < jl)), 1.0, 0.0)
    return jnp.where((beaten_by < N_SEL) & (imp > 0.5 * NEG_INF), 1.0, 0.0)


def _select_blocks_t(imp, tpos, n_sb):
    nj, tq = imp.shape
    jr = lax.broadcasted_iota(jnp.int32, (nj, tq), 0)
    jf = jr.astype(jnp.float32)
    forced = (jr == 0) | (jr == (tpos >> 6))
    imp = jnp.where(forced, FORCE_SCORE, imp)
    imp = jnp.where(jr * SEL_BLOCK <= tpos, imp, NEG_INF)
    imp = jnp.where(jr < n_sb, imp, BIG_NEG)
    sel = jnp.zeros((nj, tq), jnp.float32)
    for _ in range(N_SEL):
        m = jnp.max(imp, axis=0, keepdims=True)
        first = jnp.min(jnp.where(imp == m, jf, 1e9), axis=0, keepdims=True)
        hit = jf == first
        sel = jnp.where(hit & (m > 0.5 * NEG_INF), 1.0, sel)
        imp = jnp.where(hit, BIG_NEG, imp)
    return sel


def _nsa_prompt_kernel(q_ref, kcmp_ref, vcmpt_ref, ks_ref, vst_ref, kw_ref, vwt_ref, glt_ref, ovt_ref, et_ref, o_ref,
                       *, tq, n_cmp, n_sb):
    i = pl.program_id(2)
    t0 = i * tq
    bf = jnp.bfloat16
    qf = q_ref[0]
    q4 = jnp.concatenate([qf[:, g * HEAD_DIM:(g + 1) * HEAD_DIM] for g in range(GROUP)], axis=0)
    lanes4 = lambda x: jnp.concatenate([x] * GROUP, axis=1)
    tq_pos = t0 + lax.broadcasted_iota(jnp.int32, (1, tq), 1)
    tpos = lanes4(tq_pos)

    n_wc = WINDOW // tq + 1
    c0 = jnp.maximum(i - WINDOW // tq, 0)
    w0 = pl.multiple_of(c0 * tq, tq)
    d = tpos - (w0 + lax.broadcasted_iota(jnp.int32, (n_wc * tq, 1), 0))
    sw = lax.dot_general(kw_ref[0, 0, pl.ds(w0, n_wc * tq), :], q4, _NT, preferred_element_type=jnp.float32)
    sw = jnp.where((d >= 0) & (d < WINDOW), sw, NEG_INF)
    ew = jnp.exp2(sw - jnp.max(sw, axis=0, keepdims=True))
    vwt = jnp.concatenate([vwt_ref[0, 0, c0 + r] for r in range(n_wc)], axis=1)
    o_win = (jnp.dot(vwt, ew.astype(bf), preferred_element_type=jnp.float32)
             / jnp.maximum(jnp.sum(ew, axis=0, keepdims=True), 1e-30))

    nr = lax.broadcasted_iota(jnp.int32, (128, 1), 0)
    maskc = ((nr * CMP_STRIDE + (CMP_BLOCK - 1)) <= tpos) & (nr < n_cmp)
    sc = jnp.where(maskc, lax.dot_general(kcmp_ref[0, 0], q4, _NT, preferred_element_type=jnp.float32), NEG_INF)
    ec = jnp.where(maskc, jnp.exp2(sc - jnp.max(sc, axis=0, keepdims=True)), 0.0)
    pc = ec / jnp.maximum(jnp.sum(ec, axis=0, keepdims=True), 1e-30)
    o_cmp = jnp.dot(vcmpt_ref[0, 0], pc.astype(bf), preferred_element_type=jnp.float32)
    psum = pc[:, :tq] + pc[:, tq:2 * tq] + pc[:, 2 * tq:3 * tq] + pc[:, 3 * tq:]
    p_hi = psum.astype(bf)
    p_lo = (psum - p_hi.astype(jnp.float32)).astype(bf)
    imp = (jnp.dot(ovt_ref[...], p_hi, preferred_element_type=jnp.float32)
           + jnp.dot(ovt_ref[...], p_lo, preferred_element_type=jnp.float32))
    nj = -(-n_sb // 8) * 8
    sel = _select_blocks_t(imp[:nj], tq_pos, n_sb)
    sel = jnp.concatenate([sel, jnp.zeros((128 - nj, tq), jnp.float32)], axis=0).astype(bf)

    per_kc = SEL_KC // tq

    def sel_step(c, carry):
        m, l, acc = carry
        k0 = pl.multiple_of(c * SEL_KC, SEL_KC)
        kpos = k0 + lax.broadcasted_iota(jnp.int32, (SEL_KC, 1), 0)
        picked = jnp.dot(et_ref[pl.ds(k0, SEL_KC), :], sel, preferred_element_type=jnp.float32)
        mask = lanes4((picked > 0.5) & (kpos <= tq_pos))
        s = lax.dot_general(ks_ref[0, 0, pl.ds(k0, SEL_KC), :], q4, _NT, preferred_element_type=jnp.float32)
        s = jnp.where(mask, s, NEG_INF)
        m_new = jnp.maximum(m, jnp.max(s, axis=0, keepdims=True))
        alpha = jnp.exp2(m - m_new)
        p = jnp.exp2(s - m_new)
        l = alpha * l + jnp.sum(p, axis=0, keepdims=True)
        vt = jnp.concatenate([vst_ref[0, 0, c * per_kc + r] for r in range(per_kc)], axis=1)
        acc = alpha * acc + jnp.dot(vt, p.astype(bf), preferred_element_type=jnp.float32)
        return m_new, l, acc

    nq = GROUP * tq
    init = (jnp.full((1, nq), NEG_INF, jnp.float32), jnp.zeros((1, nq), jnp.float32),
            jnp.zeros((HEAD_DIM, nq), jnp.float32))
    n_kc = (t0 + tq + SEL_KC - 1) // SEL_KC
    _, l_s, acc_s = lax.fori_loop(0, n_kc, sel_step, init)
    o_sel = acc_s / jnp.maximum(l_s, 1e-30)

    gate = jax.nn.sigmoid(glt_ref[0, 0])
    for g in range(GROUP):
        sl = slice(g * tq, (g + 1) * tq)
        ot = (gate[3 * g:3 * g + 1] * o_cmp[:, sl] + gate[3 * g + 1:3 * g + 2] * o_sel[:, sl]
              + gate[3 * g + 2:3 * g + 3] * o_win[:, sl])
        o_ref[0, :, g * HEAD_DIM:(g + 1) * HEAD_DIM] = ot.T.astype(o_ref.dtype)


def _overlap_matrix(n_cmp, n_sb):
    start = np.arange(128) * CMP_STRIDE
    end = start + CMP_BLOCK - 1
    sb = np.arange(128) * SEL_BLOCK
    ov = (start[:, None] < sb[None, :] + SEL_BLOCK) & (end[:, None] >= sb[None, :])
    ov &= (np.arange(128)[:, None] < n_cmp) & (np.arange(128)[None, :] < n_sb)
    return jnp.asarray(ov, jnp.bfloat16)


def _heads_major(x):
    B, T, W = x.shape
    return x.reshape(B, T, N_KV_HEADS, W // N_KV_HEADS).transpose(0, 2, 1, 3)


def _nsa_prompt_pallas(q, kcmp, vcmp, ks, vs_t, kw, vw_t, gl_t):
    B, T, _ = q.shape
    tq = ATT_TQ
    n_cmp = (T - CMP_BLOCK) // CMP_STRIDE + 1
    n_sb = -(-T // SEL_BLOCK)
    assert T % SEL_KC == 0 and T >= WINDOW + tq and kcmp.shape[1] <= 128 and n_sb <= 128
    kcmp, vcmp = (jnp.pad(a, ((0, 0), (0, 128 - a.shape[1]), (0, 0))) for a in (kcmp, vcmp))
    vcmp_t = vcmp.reshape(B, 128, N_KV_HEADS, HEAD_DIM).transpose(0, 2, 3, 1)
    et = jnp.asarray((np.arange(T)[:, None] // SEL_BLOCK) == np.arange(128)[None, :], jnp.bfloat16)
    k_spec = pl.BlockSpec((1, 1, T, HEAD_DIM), lambda b, k, i: (b, k, 0, 0))
    v_spec = pl.BlockSpec((1, 1, T // tq, HEAD_DIM, tq), lambda b, k, i: (b, k, 0, 0, 0))
    return pl.pallas_call(
        functools.partial(_nsa_prompt_kernel, tq=tq, n_cmp=n_cmp, n_sb=n_sb),
        out_shape=jax.ShapeDtypeStruct((B, T, Q_W), jnp.bfloat16),
        grid=(B, N_KV_HEADS, T // tq),
        in_specs=[pl.BlockSpec((1, tq, GROUP * HEAD_DIM), lambda b, k, i: (b, i, k)),
                  pl.BlockSpec((1, 1, 128, HEAD_DIM), lambda b, k, i: (b, k, 0, 0)),
                  pl.BlockSpec((1, 1, HEAD_DIM, 128), lambda b, k, i: (b, k, 0, 0)),
                  k_spec, v_spec, k_spec, v_spec,
                  pl.BlockSpec((1, 1, 3 * GROUP, tq), lambda b, k, i: (b, k, 0, i)),
                  pl.BlockSpec((128, 128), lambda b, k, i: (0, 0)),
                  pl.BlockSpec((T, 128), lambda b, k, i: (0, 0))],
        out_specs=pl.BlockSpec((1, tq, GROUP * HEAD_DIM), lambda b, k, i: (b, i, k)),
        compiler_params=pltpu.CompilerParams(
            dimension_semantics=("parallel", "parallel", "arbitrary"), vmem_limit_bytes=VMEM_LIMIT_BYTES),
        name="nsa_prompt",
    )(q, _heads_major(kcmp), vcmp_t, ks, vs_t, kw, vw_t, gl_t, _overlap_matrix(n_cmp, n_sb).T, et)


_NT = (((1,), (1,)), ((), ()))
SAMPLE_NB = 2


def _decode_attend(s, mask, s_new, mask_new, pv_fn, v_new):
    sm = jnp.where(mask, s, NEG_INF)
    sn = jnp.where(mask_new, s_new, NEG_INF)
    m = jnp.maximum(jnp.max(sm, axis=1, keepdims=True), sn)
    e = jnp.where(mask, jnp.exp2(sm - m), 0.0)
    en = jnp.where(mask_new, jnp.exp2(sn - m), 0.0)
    l = jnp.sum(e, axis=1, keepdims=True) + en
    acc = pv_fn(e.astype(jnp.bfloat16)) + (en.astype(jnp.bfloat16).astype(jnp.float32)
                                            * v_new.astype(jnp.bfloat16).astype(jnp.float32))
    return acc / jnp.maximum(l, 1e-30)


def _nsa_sample_kernel(pt_ref, *refs, n_pages, n_sb, nb, n_const):
    del pt_ref
    cmp_pages = refs[:nb * n_pages]
    sel_pages_all = refs[nb * n_pages:2 * nb * n_pages]
    rest = refs[2 * nb * n_pages:]
    win_ref, q_ref, gl_ref, ksn_ref, vsn_ref, kwn_ref, vwn_ref, kwc_ref, vwc_ref = rest[:9]
    (wk_ref, wv_ref, pek_ref, pev_ref, w1k_ref, w1v_ref, w2k_ref, w2v_ref, cos_ref, sin_ref, ov_ref, ex_ref,
     perm_ref) = rest[9:9 + n_const]
    o_ref, wout_ref = rest[-2:]
    past = n_pages * PAGE_SIZE
    n_half = past // CMP_STRIDE
    n_cmp = (past + 1 - CMP_BLOCK) // CMP_STRIDE + 1

    groups = PAGE_SIZE // CMP_STRIDE

    k_tiles, v_tiles = [], []
    for p in range(nb * n_pages):
        page = cmp_pages[p][0, 0].reshape(2 * KV_W, PAGE_SIZE).astype(jnp.bfloat16)
        z = lax.dot_general(perm_ref[...], page, _NT, preferred_element_type=jnp.float32)
        for tiles, z_kv in ((k_tiles, z[:, :KV_W]), (v_tiles, z[:, KV_W:])):
            tiles.append(jnp.concatenate([z_kv[j * groups:(j + 1) * groups] for j in range(CMP_STRIDE)], axis=1))
    k_cat = jnp.concatenate(k_tiles, axis=0).astype(jnp.bfloat16)
    v_cat = jnp.concatenate(v_tiles, axis=0).astype(jnp.bfloat16)

    rows = nb * n_half
    valid = (lax.broadcasted_iota(jnp.int32, (rows, KV_W), 0) % n_half) < n_cmp
    kcmp = _compress_rows(k_cat, wk_ref, pek_ref, w1k_ref, w2k_ref, rows)
    kcmp = jnp.where(valid, _rope_lanes(kcmp, cos_ref[...], sin_ref[...]), 0.0).astype(jnp.bfloat16)
    vcmp = _compress_rows(v_cat, wv_ref, pev_ref, w1v_ref, w2v_ref, rows)
    vcmp = jnp.where(valid, vcmp, 0.0).astype(jnp.bfloat16)
    for s in range(nb):
        _nsa_sample_one(s, kcmp[s * n_half:(s + 1) * n_half], vcmp[s * n_half:(s + 1) * n_half],
                        sel_pages_all[s * n_pages:(s + 1) * n_pages], win_ref, q_ref, gl_ref, ksn_ref, vsn_ref,
                        kwn_ref, vwn_ref, kwc_ref, vwc_ref, ov_ref, ex_ref, o_ref, wout_ref, n_pages, n_sb)


def _nsa_sample_one(s, kcmp, vcmp, sel_pages, win_ref, q_ref, gl_ref, ksn_ref, vsn_ref, kwn_ref, vwn_ref,
                    kwc_ref, vwc_ref, ov_ref, ex_ref, o_ref, wout_ref, n_pages, n_sb):
    past = n_pages * PAGE_SIZE
    qpos = past
    n_half = past // CMP_STRIDE
    n_cmp = (past + 1 - CMP_BLOCK) // CMP_STRIDE + 1
    wb = win_ref.shape[-1]
    bf = jnp.bfloat16

    row8 = lax.broadcasted_iota(jnp.int32, (8, KV_W), 0)
    lane8 = lax.broadcasted_iota(jnp.int32, (8, KV_W), 1)
    top1 = lax.broadcasted_iota(jnp.int32, (8, 1), 0) < GROUP
    q8 = q_ref[s].astype(jnp.float32)
    q2 = jnp.where((row8 < GROUP) == (lane8 < HEAD_DIM), jnp.concatenate([q8, q8], axis=1), 0.0).astype(bf)
    q2f = q2.astype(jnp.float32)

    def halves(x):
        return jnp.where(top1, x[:, :HEAD_DIM], x[:, HEAD_DIM:])

    def new_score(k_new):
        return jnp.sum(q2f * k_new.astype(bf).astype(jnp.float32), axis=1, keepdims=True)

    sc = lax.dot_general(q2, kcmp, _NT, preferred_element_type=jnp.float32)
    nl = lax.broadcasted_iota(jnp.int32, (8, n_half), 1)
    pc = _softmax_rows(sc, ((nl * CMP_STRIDE + (CMP_BLOCK - 1)) <= qpos) & (nl < n_cmp))
    o_cmp = halves(jnp.dot(pc.astype(bf), vcmp, preferred_element_type=jnp.float32))
    pk0 = jnp.sum(jnp.where(top1, pc, 0.0), axis=0, keepdims=True)
    pk1 = jnp.sum(jnp.where(top1, 0.0, pc), axis=0, keepdims=True)
    rown = lax.broadcasted_iota(jnp.int32, (8, n_half), 0)
    p2 = jnp.where(rown == 0, pk0, jnp.where(rown == 1, pk1, 0.0))
    p_hi = p2.astype(bf)
    p_lo = (p2 - p_hi.astype(jnp.float32)).astype(bf)
    imp = (jnp.dot(p_hi, ov_ref[...], preferred_element_type=jnp.float32)
           + jnp.dot(p_lo, ov_ref[...], preferred_element_type=jnp.float32))
    sel2 = _select_blocks(imp, jnp.full((8, 1), qpos, jnp.int32), n_sb)

    picked2 = jnp.dot(sel2.astype(bf), ex_ref[...], preferred_element_type=jnp.float32)
    mask_s = jnp.where(top1, picked2[0:1], picked2[1:2]) > 0.5
    seln = jnp.sum(jnp.where(lane8 == qpos // SEL_BLOCK, sel2, 0.0), axis=1, keepdims=True)
    mask_new = jnp.where(top1, seln[0:1], seln[1:2]) > 0.5
    kt = jnp.concatenate([sel_pages[p][0, 0, 0].astype(bf) for p in range(n_pages)], axis=1)
    vt = jnp.concatenate([sel_pages[p][0, 0, 1].astype(bf) for p in range(n_pages)], axis=1)
    s_s = jnp.dot(q2, kt, preferred_element_type=jnp.float32)
    pv_sel = lambda e: lax.dot_general(e, vt, _NT, preferred_element_type=jnp.float32)

    o_sel = halves(_decode_attend(s_s, mask_s, new_score(ksn_ref[s]), mask_new, pv_sel, vsn_ref[s]))

    s_w = jnp.dot(q2, win_ref[0, s, 0].astype(bf), preferred_element_type=jnp.float32)
    kpos = past - wb + lax.broadcasted_iota(jnp.int32, (8, wb), 1)
    mask_w = (qpos - kpos >= 0) & (qpos - kpos < WINDOW) & (kpos >= 0)
    pv_win = lambda e: lax.dot_general(e, win_ref[0, s, 1].astype(bf), _NT, preferred_element_type=jnp.float32)
    o_win = halves(_decode_attend(s_w, mask_w, new_score(kwn_ref[s]), jnp.full((8, 1), True), pv_win, vwn_ref[s]))

    gate = jax.nn.sigmoid(gl_ref[s])
    o_ref[s] = (gate[:, 0:1] * o_cmp + gate[:, 1:2] * o_sel + gate[:, 2:3] * o_win).astype(o_ref.dtype)

    lane_w = lax.broadcasted_iota(jnp.int32, (KV_W, wb), 1)
    wout_ref[s, 0] = jnp.where(lane_w == wb - 1, kwc_ref[s], pltpu.roll(win_ref[0, s, 0], wb - 1, 1))
    wout_ref[s, 1] = jnp.where(lane_w == wb - 1, vwc_ref[s], pltpu.roll(win_ref[0, s, 1], wb - 1, 1))


def _cache_rows_on_lanes(c):
    nd = c.ndim
    c = jnp.moveaxis(c, nd - 4, nd - 1)
    return c.reshape(c.shape[:-3] + (c.shape[-3] * c.shape[-2], c.shape[-1]))


def _nsa_sample_pallas(layer, q, gl, ks, vs, kw, vw, cmpw, cmp_t, sel_t, win_t, page_table, win_out):
    B = q.shape[0]
    n_pages = page_table.shape[1]
    past = n_pages * PAGE_SIZE
    wb = win_t.shape[-1]
    n_half = past // CMP_STRIDE
    n_cmp = (past + 1 - CMP_BLOCK) // CMP_STRIDE + 1
    n_sb = -(-(past + 1) // SEL_BLOCK)
    wk1, pek, wk2, wv1, pev, wv2 = cmpw
    wk, pekf, w1k, w2k = _compress_weights(wk1, pek, wk2)
    wv, pevf, w1v, w2v = _compress_weights(wv1, pev, wv2)
    nb = _pick_tile(B, (SAMPLE_NB,))
    nb = nb if nb == SAMPLE_NB else 1
    cos, sin = _rope_tables(jnp.tile(jnp.arange(n_half) * CMP_STRIDE + CMP_BLOCK - 1, nb), N_KV_HEADS)
    ov = _overlap_matrix(n_cmp, n_sb)[:n_half]
    ex = jnp.asarray((np.arange(past)[None, :] // SEL_BLOCK) == np.arange(128)[:, None], jnp.bfloat16)
    groups = PAGE_SIZE // CMP_STRIDE
    src_row = (np.arange(PAGE_SIZE) % groups) * CMP_STRIDE + np.arange(PAGE_SIZE) // groups
    perm = jnp.asarray(src_row[:, None] == np.arange(PAGE_SIZE)[None, :], jnp.bfloat16)
    consts = (wk, wv, pekf, pevf, w1k, w1v, w2k, w2v, cos, sin, ov, ex, perm)
    row3 = lambda x: x.reshape(B, 1, KV_W)
    col3 = lambda x: x.reshape(B, KV_W, 1)
    per_b = (q.reshape(B, N_HEADS, HEAD_DIM), gl.reshape(B, N_HEADS, 3), row3(ks), row3(vs), row3(kw), row3(vw),
             col3(kw), col3(vw))
    page_spec = lambda s, p: pl.BlockSpec((1, 1, 2, KV_W, PAGE_SIZE),
                                          lambda b, pt: (layer, pt[b * nb + s, p], 0, 0, 0))
    b_spec = lambda a: pl.BlockSpec((nb,) + a.shape[1:], lambda b, pt: (b,) + (0,) * (a.ndim - 1))
    full = lambda a: pl.BlockSpec(a.shape, lambda b, pt: (0,) * a.ndim)
    in_specs = ([page_spec(s, p) for s in range(nb) for p in range(n_pages)] * 2
                + [pl.BlockSpec((1, nb, 2, KV_W, wb), lambda b, pt: (layer, b, 0, 0, 0))]
                + [b_spec(a) for a in per_b] + [full(a) for a in consts])
    args = (page_table, *([cmp_t] * (nb * n_pages)), *([sel_t] * (nb * n_pages)), win_t, *per_b, *consts)
    aliases = {}
    if win_out is not None:
        in_specs.append(pl.BlockSpec(memory_space=pl.ANY))
        aliases = {len(args): 1}
        args += (win_out,)
    return pl.pallas_call(
        functools.partial(_nsa_sample_kernel, n_pages=n_pages, n_sb=n_sb, nb=nb, n_const=len(consts)),
        out_shape=(jax.ShapeDtypeStruct((B, N_HEADS, HEAD_DIM), jnp.bfloat16),
                   jax.ShapeDtypeStruct(win_t.shape, jnp.float32)),
        grid_spec=pltpu.PrefetchScalarGridSpec(
            num_scalar_prefetch=1, grid=(B // nb,), in_specs=in_specs,
            out_specs=(pl.BlockSpec((nb, N_HEADS, HEAD_DIM), lambda b, pt: (b, 0, 0)),
                       pl.BlockSpec((None, nb, 2, KV_W, wb), lambda b, pt: (layer, b, 0, 0, 0)))),
        input_output_aliases=aliases,
        compiler_params=pltpu.CompilerParams(dimension_semantics=("arbitrary",), vmem_limit_bytes=VMEM_LIMIT_BYTES),
        name="nsa_sample",
    )(*args)


SSM_N = SSM_GROUPS * SSM_STATE
SSM_LANE_BLK = 512
SSM_TL = 64


def _ssm_kernel(u_ref, h0r_ref, h0i_ref, ar_ref, ai_ref, bm_ref, cr_ref, ci_ref, d_ref,
                y_ref, hr_ref, hi_ref, xr_s, xi_s, *, tl, nb):
    c = pl.program_id(0)

    @pl.when(c == 0)
    def _():
        hr_ref[...] = h0r_ref[...]
        hi_ref[...] = h0i_ref[...]

    u = u_ref[...]
    ub = u.astype(jnp.bfloat16)
    n_grp = SSM_WIDTH // 128
    for j in range(n_grp):
        bu = jnp.dot(ub[:, 128 * j:128 * (j + 1)], bm_ref[j], preferred_element_type=jnp.float32)
        xr_s[:, 512 * j:512 * (j + 1)] = bu[:, :512]
        xi_s[:, 512 * j:512 * (j + 1)] = bu[:, 512:]

    for lb in range(SSM_N // SSM_LANE_BLK):
        sl = slice(lb * SSM_LANE_BLK, (lb + 1) * SSM_LANE_BLK)
        ar = jnp.broadcast_to(ar_ref[:, sl], (8, SSM_LANE_BLK))
        ai = jnp.broadcast_to(ai_ref[:, sl], (8, SSM_LANE_BLK))
        for r in range(nb // 8):
            def step(t, carry):
                hr, hi = carry
                row = pl.multiple_of(t * nb + r * 8, 8)
                xr = xr_s[pl.ds(row, 8), sl]
                xi = xi_s[pl.ds(row, 8), sl]
                nr = ar * hr - ai * hi + xr
                ni = ar * hi + ai * hr + xi
                xr_s[pl.ds(row, 8), sl] = nr
                xi_s[pl.ds(row, 8), sl] = ni
                return nr, ni

            hr, hi = lax.fori_loop(0, tl, step, (hr_ref[r * 8:(r + 1) * 8, sl], hi_ref[r * 8:(r + 1) * 8, sl]))
            hr_ref[r * 8:(r + 1) * 8, sl] = hr
            hi_ref[r * 8:(r + 1) * 8, sl] = hi

    for j in range(n_grp):
        yr = jnp.dot(xr_s[:, 512 * j:512 * (j + 1)].astype(jnp.bfloat16), cr_ref[j], preferred_element_type=jnp.float32)
        yi = jnp.dot(xi_s[:, 512 * j:512 * (j + 1)].astype(jnp.bfloat16), ci_ref[j], preferred_element_type=jnp.float32)
        y_ref[:, 128 * j:128 * (j + 1)] = yr - yi + d_ref[:, 128 * j:128 * (j + 1)] * u[:, 128 * j:128 * (j + 1)]


def _ssm_params(a_re, a_im, log_dt, b_re, b_im, c_re, c_im, d_skip):
    dt = jnp.exp(log_dt)[:, None]
    mag = jnp.exp(dt * a_re)
    ab_re = mag * jnp.cos(dt * a_im)
    ab_im = mag * jnp.sin(dt * a_im)
    den = a_re * a_re + a_im * a_im
    zr = ((ab_re - 1.0) * a_re + ab_im * a_im) / den
    zi = (ab_im * a_re - (ab_re - 1.0) * a_im) / den
    bb_re = zr[..., None] * b_re - zi[..., None] * b_im
    bb_im = zr[..., None] * b_im + zi[..., None] * b_re
    n_grp = SSM_WIDTH // 128
    gpl = 128 // SSM_GROUP
    eye = jnp.eye(gpl, dtype=jnp.float32)

    def b_blocks(bb):
        x = bb.reshape(n_grp, gpl, SSM_STATE, SSM_GROUP)
        return jnp.einsum('jgpc,gh->jgchp', x, eye).reshape(n_grp, 128, gpl * SSM_STATE)

    def c_blocks(cc):
        x = cc.reshape(n_grp, gpl, SSM_GROUP, SSM_STATE)
        return jnp.einsum('jgcp,gh->jgphc', x, eye).reshape(n_grp, gpl * SSM_STATE, 128)

    bm = jnp.concatenate([b_blocks(bb_re), b_blocks(bb_im)], axis=-1).astype(jnp.bfloat16)
    return (ab_re.reshape(1, SSM_N), ab_im.reshape(1, SSM_N), bm,
            c_blocks(c_re).astype(jnp.bfloat16), c_blocks(c_im).astype(jnp.bfloat16), d_skip.reshape(1, SSM_WIDTH))


def _ssm_pallas(u_tb, B, T, h0_re, h0_im, params):
    ab_re, ab_im, bm, cr, ci, d = params
    tl = _pick_tile(T, (SSM_TL,))
    full = lambda a: pl.BlockSpec(a.shape, lambda c: (0,) * a.ndim)
    h0r = h0_re.reshape(B, SSM_N)
    h0i = h0_im.reshape(B, SSM_N)
    consts = (h0r, h0i, ab_re, ab_im, bm, cr, ci, d)
    y, hr, hi = pl.pallas_call(
        functools.partial(_ssm_kernel, tl=tl, nb=B),
        out_shape=(jax.ShapeDtypeStruct((T * B, SSM_WIDTH), jnp.float32),
                   jax.ShapeDtypeStruct((B, SSM_N), jnp.float32), jax.ShapeDtypeStruct((B, SSM_N), jnp.float32)),
        grid=(T // tl,),
        in_specs=[pl.BlockSpec((tl * B, SSM_WIDTH), lambda c: (c, 0))] + [full(a) for a in consts],
        out_specs=(pl.BlockSpec((tl * B, SSM_WIDTH), lambda c: (c, 0)),
                   pl.BlockSpec((B, SSM_N), lambda c: (0, 0)), pl.BlockSpec((B, SSM_N), lambda c: (0, 0))),
        scratch_shapes=[pltpu.VMEM((tl * B, SSM_N), jnp.float32), pltpu.VMEM((tl * B, SSM_N), jnp.float32)],
        compiler_params=pltpu.CompilerParams(dimension_semantics=("arbitrary",), vmem_limit_bytes=VMEM_LIMIT_BYTES),
        name="ssm",
    )(u_tb, *consts)
    return y, hr.reshape(B, SSM_GROUPS, SSM_STATE), hi.reshape(B, SSM_GROUPS, SSM_STATE)


def _block(h, p_all, pos, lw, layer, sample, h0_re, h0_im, conv_prefix):
    (g_attn, w_in, g_q, g_kc, g_ks, g_kw, wk1, pek, wk2, wv1, pev, wv2,
     a_re, a_im, log_dt, b_re, b_im, c_re, c_im, d_skip,
     w_a, w_glu1, w_glu2, w_o, g_ffn, w_up, conv_w, conv_b, w_down,
     g_ple, w_ple_gate, w_ple) = lw
    B, T = p_all.shape[1:3]
    N = B * T
    qs, rows_cmp, rows_sel, rows_win, gl, u, gab, *attn = _in_proj_pallas(h, w_in, layer, g_attn, g_q, g_kc, g_ks,
                                                                           g_kw, pos, T)
    gl = gl[:, :3 * N_HEADS]
    cmpw = (wk1, pek, wk2, wv1, pev, wv2)
    as_rows = lambda a: a.reshape(B, T, 2, N_KV_HEADS, HEAD_DIM)
    if sample is None:
        kcmp, vcmp = _compress_pallas(rows_cmp.reshape(B, T, 2 * KV_W), cmpw)
        o = _nsa_prompt_pallas(qs.reshape(B, T, Q_W), kcmp, vcmp, *attn).reshape(N, Q_W)
        n_keep = min(WINDOW, T)
        rows = (as_rows(rows_cmp), as_rows(rows_sel), as_rows(rows_win)[:, T - n_keep:])
    else:
        assert T == 1
        cmp_t, sel_t, win_t, page_table, win_out = sample
        o, win_out = _nsa_sample_pallas(layer, qs, gl, rows_sel[:, :KV_W], rows_sel[:, KV_W:], rows_win[:, :KV_W],
                                        rows_win[:, KV_W:], cmpw, cmp_t, sel_t, win_t, page_table, win_out)
        o = o.reshape(N, Q_W)
        rows = (as_rows(rows_cmp), as_rows(rows_sel), win_out)
    y, hr, hi = _ssm_pallas(u.reshape(T * B, SSM_WIDTH), B, T, h0_re, h0_im,
                            _ssm_params(a_re, a_im, log_dt, b_re, b_im, c_re, c_im, d_skip))
    h, conv_rows = _mix_ffn_pallas(h, o, y.reshape(u.shape), gab, p_all.reshape(p_all.shape[0], N, -1), conv_prefix,
                                   (w_a, w_glu1, w_glu2, w_o, g_ffn, w_up, conv_w, conv_b, w_down,
                                    g_ple, w_ple_gate, w_ple), layer, T)
    return h, rows, hr, hi, conv_rows


def kernel(x_prompt, x_sample, cache_cmp, cache_sel, cache_win, state_ssm_re, state_ssm_im, state_conv, page_table, p_prompt, p_sample, g_attn, w_in, g_q, g_kc, g_ks, g_kw, cmp_wk1, cmp_pek, cmp_wk2, cmp_wv1, cmp_pev, cmp_wv2, ssm_a_re, ssm_a_im, ssm_log_dt, ssm_b_re, ssm_b_im, ssm_c_re, ssm_c_im, ssm_d, w_a, w_glu1, w_glu2, w_o, g_ffn, w_up, conv_w, conv_b, w_down, g_ple, w_ple_gate, w_ple):
    Bp, Tp = x_prompt.shape[:2]
    Ts = x_sample.shape[1]
    depth = w_in.shape[0]
    past = page_table.shape[1] * PAGE_SIZE
    pos_p = jnp.arange(Tp)
    pos_s = past + jnp.arange(Ts)
    zeros_h = jnp.zeros((Bp, SSM_GROUPS, SSM_STATE), x_prompt.dtype)
    whole = lambda w: (w.astype(jnp.bfloat16),)
    layer_w = (g_attn, (_pad_w_in(w_in),), g_q, g_kc, g_ks, g_kw, cmp_wk1, cmp_pek, cmp_wk2, cmp_wv1, cmp_pev, cmp_wv2,
               ssm_a_re, ssm_a_im, ssm_log_dt, ssm_b_re, ssm_b_im, ssm_c_re, ssm_c_im, ssm_d,
               whole(w_a), whole(w_glu1), whole(w_glu2), whole(w_o), g_ffn, whole(w_up), conv_w, conv_b, whole(w_down),
               g_ple, whole(w_ple_gate), whole(w_ple))
    cmp_t, sel_t, win_t = (_cache_rows_on_lanes(c) for c in (cache_cmp, cache_sel, cache_win))
    st = [[] for _ in range(12)]
    hp, hs = x_prompt.reshape(Bp * Tp, D_MODEL), x_sample.reshape(-1, D_MODEL)
    win_out = None
    for i in range(depth):
        lw = [w[0] if isinstance(w, tuple) else w[i] for w in layer_w]
        hp, rows, hr, hi, cv = _block(hp, p_prompt, pos_p, lw, i, None, zeros_h, zeros_h, None)
        for j, a in enumerate(list(rows) + [hr, hi, cv]):
            st[j].append(a)
        hs, rows, hr, hi, cv = _block(hs, p_sample, pos_s, lw, i, (cmp_t, sel_t, win_t, page_table, win_out),
                                      state_ssm_re[i], state_ssm_im[i], state_conv[i])
        win_out = rows[2]
        for j, a in enumerate(list(rows) + [hr, hi, cv]):
            st[6 + j].append(a)
    outs = [None if j == 8 else jnp.stack(s) for j, s in enumerate(st)]
    wb = win_out.shape[-1]
    outs[8] = win_out.reshape(win_out.shape[:3] + (N_KV_HEADS, HEAD_DIM, wb)).transpose(0, 1, 5, 2, 3, 4)
    return (hp.reshape(x_prompt.shape), hs.reshape(x_sample.shape)) + tuple(outs)
```

```python
import functools
import math

import numpy as np
import jax
import jax.numpy as jnp
from jax import lax
from jax.experimental import pallas as pl
from jax.experimental.pallas import tpu as pltpu

D_MODEL = 1024
N_HEADS = 8
N_KV_HEADS = 2
HEAD_DIM = 64
GROUP = N_HEADS // N_KV_HEADS
Q_W = N_HEADS * HEAD_DIM
KV_W = N_KV_HEADS * HEAD_DIM
CMP_BLOCK = 32
CMP_STRIDE = 16
SEL_BLOCK = 64
N_SEL = 8
WINDOW = 512
PAGE_SIZE = 128
ROPE_THETA = 10000.0
SSM_WIDTH = D_MODEL // 2
SSM_GROUP = 16
SSM_GROUPS = SSM_WIDTH // SSM_GROUP
SSM_STATE = 64
D_FF = 11 * D_MODEL // 4
CONV_W = 3
EPS = 1e-6
NEG_INF = -1e30
FORCE_SCORE = 1e9
SCALE = HEAD_DIM ** -0.5
QK_SCALE = SCALE * math.log2(math.e)

VMEM_LIMIT_BYTES = 56 * 1024 * 1024


def _pick_tile(n, cands):
    for c in cands:
        if n % c == 0:
            return c
    return n


def _rms(x, g):
    return x * lax.rsqrt(jnp.mean(x * x, axis=-1, keepdims=True) + EPS) * g


def _bdot(a, b):
    return jnp.dot(a.astype(jnp.bfloat16), b, preferred_element_type=jnp.float32)


def _layer_spec(stack, layer):
    nd = stack.ndim
    return pl.BlockSpec((None,) + stack.shape[1:], lambda *_: (layer,) + (0,) * (nd - 1),
                        pipeline_mode=pl.Buffered(1))


GL_PAD = 128
_IN_WIDTHS = (Q_W, 6 * KV_W, GL_PAD, SSM_WIDTH, 2 * D_MODEL)
_IN_OFFS = tuple(int(v) for v in np.cumsum((0,) + _IN_WIDTHS))
ROW_TILE = 256
IN_TILE = 512


def _in_proj_kernel(h_ref, g_ref, w_ref, gq_ref, gk_ref, ones_ref, cos_ref, sin_ref,
                    q_ref, cmp_ref, sel_ref, win_ref, gl_ref, u_ref, gab_ref, *attn_refs):
    xn = _rms(h_ref[...], g_ref[...]).astype(jnp.bfloat16)
    seg = lambda s: jnp.dot(xn, w_ref[:, _IN_OFFS[s]:_IN_OFFS[s + 1]], preferred_element_type=jnp.float32)
    cos, sin = cos_ref[...], sin_ref[...]
    ones = ones_ref[...]

    def head_norm(x, gain):
        x2 = x * x
        hi = x2.astype(jnp.bfloat16)
        lo = (x2 - hi.astype(jnp.float32)).astype(jnp.bfloat16)
        ss = (jnp.dot(hi, ones, preferred_element_type=jnp.float32)
              + jnp.dot(lo, ones, preferred_element_type=jnp.float32))
        return x * lax.rsqrt(ss * (1.0 / HEAD_DIM) + EPS) * gain

    zq = seg(0)
    for c in range(Q_W // KV_W):
        qn = _rope_lanes(head_norm(zq[:, c * KV_W:(c + 1) * KV_W], gq_ref[...]), cos, sin)
        q_ref[:, c * KV_W:(c + 1) * KV_W] = (qn * QK_SCALE).astype(q_ref.dtype)
    zkv = seg(1)
    part = lambda c: zkv[:, c * KV_W:(c + 1) * KV_W]
    cmp_ref[:, :KV_W] = head_norm(part(0), gk_ref[0:1])
    cmp_ref[:, KV_W:] = part(1)
    sel_ref[:, :KV_W] = _rope_lanes(head_norm(part(2), gk_ref[1:2]), cos, sin)
    sel_ref[:, KV_W:] = part(3)
    win_ref[:, :KV_W] = _rope_lanes(head_norm(part(4), gk_ref[2:3]), cos, sin)
    win_ref[:, KV_W:] = part(5)
    gl = seg(2)
    gl_ref[...] = gl
    u_ref[...] = seg(3)
    gab_ref[...] = seg(4)
    if attn_refs:
        ksb_ref, vst_ref, kwb_ref, vwt_ref, glt_ref = attn_refs
        tq = vst_ref.shape[-1]
        for k_ref, v_ref, src in ((ksb_ref, vst_ref, sel_ref), (kwb_ref, vwt_ref, win_ref)):
            kb = src[:, :KV_W].astype(jnp.bfloat16)
            for hd in range(N_KV_HEADS):
                k_ref[0, hd] = kb[:, hd * HEAD_DIM:(hd + 1) * HEAD_DIM]
            for r in range(src.shape[0] // tq):
                vt = src[r * tq:(r + 1) * tq, KV_W:].T.astype(jnp.bfloat16)
                for hd in range(N_KV_HEADS):
                    v_ref[0, hd, r] = vt[hd * HEAD_DIM:(hd + 1) * HEAD_DIM]
        glt = gl.T
        for hd in range(N_KV_HEADS):
            glt_ref[0, hd] = glt[hd * 3 * GROUP:(hd + 1) * 3 * GROUP]


def _pad_w_in(w_in):
    a = Q_W + 6 * KV_W + 3 * N_HEADS
    pad = jnp.zeros(w_in.shape[:-1] + (GL_PAD - 3 * N_HEADS,), w_in.dtype)
    return jnp.concatenate([w_in[..., :a], pad, w_in[..., a:]], axis=-1).astype(jnp.bfloat16)


def _in_proj_pallas(h2d, w_in_p, layer, g_attn, g_q, g_kc, g_ks, g_kw, pos, seq_len):
    N = h2d.shape[0]
    T = seq_len
    B = N // T
    tm = _pick_tile(N, (IN_TILE, ROW_TILE, 128))
    nt = max(T // tm, 1)
    assert T == 1 or T % tm == 0
    cos, sin = _rope_tables(pos, N_KV_HEADS)
    if T == 1:
        cos, sin = (jnp.broadcast_to(t, (tm, KV_W)) for t in (cos, sin))
    tile2 = lambda g: jnp.tile(g.reshape(1, HEAD_DIM), (1, N_KV_HEADS))
    gk = jnp.concatenate([tile2(g_kc), tile2(g_ks), tile2(g_kw), jnp.zeros((5, KV_W), jnp.float32)], axis=0)
    ones = jnp.asarray(np.kron(np.eye(N_KV_HEADS), np.ones((HEAD_DIM, HEAD_DIM))), jnp.bfloat16)
    row = lambda w: pl.BlockSpec((tm, w), lambda i: (i, 0))
    const = lambda a: pl.BlockSpec(a.shape, lambda i: (0,) * a.ndim, pipeline_mode=pl.Buffered(1))
    tab = pl.BlockSpec((tm, KV_W), lambda i: (i % nt, 0))
    if T == 1:
        u_shape, u_spec = (N, SSM_WIDTH), row(SSM_WIDTH)
    else:
        u_shape, u_spec = (T, B * SSM_WIDTH), pl.BlockSpec((tm, SSM_WIDTH), lambda i: (i % nt, i // nt))
    consts = (g_attn.reshape(1, D_MODEL), w_in_p, tile2(g_q), gk, ones)
    f32, bf = jnp.float32, jnp.bfloat16
    out_shape = ((jax.ShapeDtypeStruct((N, Q_W), bf),) + (jax.ShapeDtypeStruct((N, 2 * KV_W), f32),) * 3
                 + (jax.ShapeDtypeStruct((N, GL_PAD), f32), jax.ShapeDtypeStruct(u_shape, f32),
                    jax.ShapeDtypeStruct((N, 2 * D_MODEL), f32)))
    out_specs = (row(Q_W), row(2 * KV_W), row(2 * KV_W), row(2 * KV_W), row(GL_PAD), u_spec, row(2 * D_MODEL))
    if T > 1:
        tq = ATT_TQ
        assert tm % tq == 0
        k_shape = jax.ShapeDtypeStruct((B, N_KV_HEADS, T, HEAD_DIM), bf)
        k_spec = pl.BlockSpec((1, N_KV_HEADS, tm, HEAD_DIM), lambda i: (i // nt, 0, i % nt, 0))
        v_shape = jax.ShapeDtypeStruct((B, N_KV_HEADS, T // tq, HEAD_DIM, tq), bf)
        v_spec = pl.BlockSpec((1, N_KV_HEADS, tm // tq, HEAD_DIM, tq), lambda i: (i // nt, 0, i % nt, 0, 0))
        g_shape = jax.ShapeDtypeStruct((B, N_KV_HEADS, 3 * GROUP, T), f32)
        g_spec = pl.BlockSpec((1, N_KV_HEADS, 3 * GROUP, tm), lambda i: (i // nt, 0, 0, i % nt))
        out_shape += (k_shape, v_shape, k_shape, v_shape, g_shape)
        out_specs += (k_spec, v_spec, k_spec, v_spec, g_spec)
    return pl.pallas_call(
        _in_proj_kernel,
        out_shape=out_shape,
        grid=(N // tm,),
        in_specs=[row(D_MODEL)] + [_layer_spec(a, layer) if a.ndim == 3 else const(a) for a in consts] + [tab, tab],
        out_specs=out_specs,
        compiler_params=pltpu.CompilerParams(dimension_semantics=("parallel",), vmem_limit_bytes=VMEM_LIMIT_BYTES),
        name="in_proj",
    )(h2d, *consts, cos, sin)


FF_CHUNK = D_FF // 2
assert FF_CHUNK % 128 == 0


def _mix_ffn_kernel(h_ref, o_ref, y_ref, gab_ref, p_ref, pre0_ref, pre1_ref,
                    wa_ref, wg1_ref, wg2_ref, wo_ref, gffn_ref, wup_ref, cw_ref, cb_ref, wdn_ref,
                    gple_ref, wpg_ref, wpl_ref, hout_ref, cs0_ref, cs1_ref, carry_s, *, seq_tiles):
    tm = h_ref.shape[0]
    a_out = jnp.dot(o_ref[...], wa_ref[...], preferred_element_type=jnp.float32)
    yg = jax.nn.gelu(y_ref[...]).astype(jnp.bfloat16)
    b_out = (jnp.dot(yg, wg1_ref[...], preferred_element_type=jnp.float32)
             * jax.nn.sigmoid(jnp.dot(yg, wg2_ref[...], preferred_element_type=jnp.float32)))
    mixed = (jax.nn.sigmoid(gab_ref[:, :D_MODEL]) * a_out + jax.nn.sigmoid(gab_ref[:, D_MODEL:]) * b_out)
    h1 = h_ref[...] + _bdot(mixed, wo_ref[...])

    xn = _rms(h1, gffn_ref[...]).astype(jnp.bfloat16)
    if seq_tiles:
        @pl.when(pl.program_id(0) % seq_tiles == 0)
        def _():
            carry_s[...] = jnp.zeros_like(carry_s)
        row = lax.broadcasted_iota(jnp.int32, (tm, 1), 0)
    ffn = jnp.zeros((tm, D_MODEL), jnp.float32)
    for c in range(D_FF // FF_CHUNK):
        sl = slice(c * FF_CHUNK, (c + 1) * FF_CHUNK)
        gp = jnp.dot(xn, wup_ref[:, sl], preferred_element_type=jnp.float32)
        val = jnp.dot(xn, wup_ref[:, D_FF + c * FF_CHUNK:D_FF + (c + 1) * FF_CHUNK], preferred_element_type=jnp.float32)
        if seq_tiles:
            old1, old2 = carry_s[7:8, sl], carry_s[6:7, sl]
            prev1 = jnp.where(row == 0, old1, pltpu.roll(gp, 1, 0))
            prev2 = jnp.where(row == 0, old2, jnp.where(row == 1, old1, pltpu.roll(gp, 2, 0)))
            carry_s[:, sl] = gp[tm - 8:, :]
            cs0_ref[0, :, sl] = gp[tm - 2:tm - 1, :]
            cs1_ref[0, :, sl] = gp[tm - 1:tm, :]
        else:
            prev2, prev1 = pre0_ref[:, sl], pre1_ref[:, sl]
            cs0_ref[:, sl] = prev1
            cs1_ref[:, sl] = gp
        conv = cb_ref[:, sl] + cw_ref[0:1, sl] * prev2 + cw_ref[1:2, sl] * prev1 + cw_ref[2:3, sl] * gp
        ffn = ffn + _bdot(jax.nn.gelu(conv) * val, wdn_ref[sl, :])
    h2 = h1 + ffn

    gate = jax.nn.sigmoid(_bdot(_rms(h2, gple_ref[...]), wpg_ref[...]))
    hout_ref[...] = h2 + gate * _bdot(p_ref[...], wpl_ref[...])


def _mix_ffn_pallas(h2d, o2d, y, gab, p2d, prefix, w, layer, seq_len):
    (w_a, w_glu1, w_glu2, w_o, g_ffn, w_up, conv_w, conv_b, w_down, g_ple, w_ple_gate, w_ple) = w
    N = h2d.shape[0]
    T = seq_len
    B = N // T
    tm = _pick_tile(N, (ROW_TILE, 128))
    nt = max(T // tm, 1)
    seq = prefix is None
    assert (seq and T % tm == 0 and tm >= 8) or (not seq and T == 1)
    row = lambda wd: pl.BlockSpec((tm, wd), lambda i: (i, 0))
    const = lambda a: pl.BlockSpec(a.shape, lambda i: (0,) * a.ndim, pipeline_mode=pl.Buffered(1))
    f32 = jnp.float32
    if seq:
        y_spec = pl.BlockSpec((tm, SSM_WIDTH), lambda i: (i % nt, i // nt))
        pre = (jnp.zeros((8, D_FF), f32),) * 2
        pre_spec = const(pre[0])
        cs_shape = jax.ShapeDtypeStruct((B, 1, D_FF), f32)
        cs_spec = pl.BlockSpec((1, 1, D_FF), lambda i: (i // nt, 0, 0))
    else:
        y_spec = row(SSM_WIDTH)
        pre = (prefix[:, 0], prefix[:, 1])
        pre_spec = row(D_FF)
        cs_shape = jax.ShapeDtypeStruct((N, D_FF), f32)
        cs_spec = row(D_FF)
    vec = lambda a: a.reshape(1, -1)
    cw8 = jnp.concatenate([conv_w, jnp.zeros((8 - CONV_W, D_FF), f32)], axis=0)
    consts = (w_a, w_glu1, w_glu2, w_o, vec(g_ffn), w_up, cw8, vec(conv_b), w_down, vec(g_ple), w_ple_gate, w_ple)
    spec = lambda a: _layer_spec(a, layer) if a.ndim == 3 else const(a)
    hout, cs0, cs1 = pl.pallas_call(
        functools.partial(_mix_ffn_kernel, seq_tiles=nt if seq else 0),
        out_shape=(jax.ShapeDtypeStruct((N, D_MODEL), f32), cs_shape, cs_shape),
        grid=(N // tm,),
        in_specs=[row(D_MODEL), row(Q_W), y_spec, row(2 * D_MODEL),
                  pl.BlockSpec((None, tm, p2d.shape[-1]), lambda i: (layer, i, 0)), pre_spec, pre_spec]
        + [spec(a) for a in consts],
        out_specs=(row(D_MODEL), cs_spec, cs_spec),
        scratch_shapes=[pltpu.VMEM((8, D_FF), f32)],
        compiler_params=pltpu.CompilerParams(dimension_semantics=("arbitrary",), vmem_limit_bytes=VMEM_LIMIT_BYTES),
        name="mix_ffn",
    )(h2d, o2d, y, gab, p2d, *pre, *consts)
    return hout, jnp.stack([cs0.reshape(B, D_FF), cs1.reshape(B, D_FF)], axis=1)


HALF_ROWS = CMP_BLOCK // CMP_STRIDE
assert HALF_ROWS == 2


def _rope_lanes(x, cos, sin_signed):
    w = x.shape[-1]
    half = HEAD_DIM // 2
    lane = lax.broadcasted_iota(jnp.int32, x.shape, x.ndim - 1)
    first = (lane % HEAD_DIM) < half
    partner = jnp.where(first, pltpu.roll(x, w - half, x.ndim - 1), pltpu.roll(x, half, x.ndim - 1))
    return x * cos + partner * sin_signed


def _compress_rows(xcat, w_ref, pe_ref, w1_ref, w2_ref, n_half):
    acc = jnp.dot(xcat, w_ref[...], preferred_element_type=jnp.float32)
    pa = acc[:, :256]
    pb = pltpu.roll(acc[:, 256:], n_half - 1, 0)
    bias = jnp.dot(pe_ref[...].astype(jnp.bfloat16), w1_ref[...], preferred_element_type=jnp.float32)[0:1]
    bias2 = jnp.concatenate([bias, bias], axis=1)
    hdn = jax.nn.gelu(pa + pb + bias2)
    return jnp.dot(hdn.astype(jnp.bfloat16), w2_ref[...], preferred_element_type=jnp.float32)


def _compress_kernel(xk_ref, xv_ref, wk_ref, wv_ref, pek_ref, pev_ref, w1k_ref, w1v_ref, w2k_ref, w2v_ref,
                     cos_ref, sin_ref, ko_ref, vo_ref, *, n_half, n_cmp):
    row = lax.broadcasted_iota(jnp.int32, (n_half, KV_W), 0)
    gather = lambda x_ref: jnp.concatenate(
        [x_ref[0, pl.ds(j, n_half, stride=CMP_STRIDE), :].astype(jnp.bfloat16) for j in range(CMP_STRIDE)], axis=1)
    k = _compress_rows(gather(xk_ref), wk_ref, pek_ref, w1k_ref, w2k_ref, n_half)
    k = _rope_lanes(k, cos_ref[...], sin_ref[...])
    v = _compress_rows(gather(xv_ref), wv_ref, pev_ref, w1v_ref, w2v_ref, n_half)
    ko_ref[0] = jnp.where(row < n_cmp, k, 0.0).astype(ko_ref.dtype)
    vo_ref[0] = jnp.where(row < n_cmp, v, 0.0).astype(vo_ref.dtype)


def _blockdiag2(w):
    z = jnp.zeros_like(w)
    return jnp.concatenate([jnp.concatenate([w, z], axis=-1), jnp.concatenate([z, w], axis=-1)], axis=-2)


def _compress_weights(w1, pe, w2):
    bd = _blockdiag2(w1)
    wcat = jnp.concatenate([bd[:CMP_STRIDE], bd[CMP_STRIDE:]], axis=-1)
    wcat = wcat.reshape(CMP_STRIDE * KV_W, -1).astype(jnp.bfloat16)
    pe_flat = jnp.broadcast_to(pe.reshape(1, -1), (8, pe.size))
    w1_flat = w1.reshape(-1, w1.shape[-1]).astype(jnp.bfloat16)
    w2bd = _blockdiag2(w2).astype(jnp.bfloat16)
    return wcat, pe_flat, w1_flat, w2bd


def _rope_tables(pos, reps):
    half = HEAD_DIM // 2
    inv = jnp.float32(ROPE_THETA) ** (-jnp.arange(half, dtype=jnp.float32) / half)
    ang = pos.astype(jnp.float32)[:, None] * inv[None, :]
    cos = jnp.cos(ang)
    sin = jnp.sin(ang)
    return (jnp.tile(jnp.concatenate([cos, cos], axis=-1), (1, reps)),
            jnp.tile(jnp.concatenate([-sin, sin], axis=-1), (1, reps)))


def _compress_pallas(rows, cmpw):
    wk1, pek, wk2, wv1, pev, wv2 = cmpw
    B, L, _ = rows.shape
    n_half = L // CMP_STRIDE
    n_cmp = n_half - 1
    wk, pekf, w1k, w2k = _compress_weights(wk1, pek, wk2)
    wv, pevf, w1v, w2v = _compress_weights(wv1, pev, wv2)
    end = jnp.arange(n_half) * CMP_STRIDE + CMP_BLOCK - 1
    cos, sin = _rope_tables(end, N_KV_HEADS)
    full = lambda a: pl.BlockSpec(a.shape, lambda b: (0,) * a.ndim)
    consts = (wk, wv, pekf, pevf, w1k, w1v, w2k, w2v, cos, sin)
    return pl.pallas_call(
        functools.partial(_compress_kernel, n_half=n_half, n_cmp=n_cmp),
        out_shape=(jax.ShapeDtypeStruct((B, n_half, KV_W), jnp.bfloat16),) * 2,
        grid=(B,),
        in_specs=[pl.BlockSpec((1, L, KV_W), lambda b: (b, 0, 0)), pl.BlockSpec((1, L, KV_W), lambda b: (b, 0, 1))]
        + [full(a) for a in consts],
        out_specs=(pl.BlockSpec((1, n_half, KV_W), lambda b: (b, 0, 0)),) * 2,
        compiler_params=pltpu.CompilerParams(dimension_semantics=("parallel",), vmem_limit_bytes=VMEM_LIMIT_BYTES),
        name="compress",
    )(rows, rows, *consts)


ATT_TQ = 256
SEL_KC = 512
BIG_NEG = -3.0e38


def _softmax_rows(s, mask):
    s = jnp.where(mask, s, NEG_INF)
    m = jnp.max(s, axis=-1, keepdims=True)
    e = jnp.where(mask, jnp.exp2(s - m), 0.0)
    return e / jnp.maximum(jnp.sum(e, axis=-1, keepdims=True), 1e-30)


def _select_blocks(imp, tpos, n_sb):
    rows = imp.shape[0]
    jl = lax.broadcasted_iota(jnp.int32, (rows, 128), 1)
    forced = (jl == 0) | (jl == (tpos >> 6))
    imp = jnp.where(forced, FORCE_SCORE, imp)
    imp = jnp.where(jl * SEL_BLOCK <= tpos, imp, NEG_INF)
    imp = jnp.where(jl < n_sb, imp, BIG_NEG)
    beaten_by = jnp.zeros((rows, 128), jnp.float32)
    for i in range(n_sb):
        col = imp[:, i:i + 1]
        beaten_by = beaten_by + jnp.where((col > imp) | ((col == imp) & (i < jl)), 1.0, 0.0)
    return jnp.where((beaten_by < N_SEL) & (imp > 0.5 * NEG_INF), 1.0, 0.0)


def _select_blocks_t(imp, tpos, n_sb):
    nj, tq = imp.shape
    jr = lax.broadcasted_iota(jnp.int32, (nj, tq), 0)
    jf = jr.astype(jnp.float32)
    forced = (jr == 0) | (jr == (tpos >> 6))
    imp = jnp.where(forced, FORCE_SCORE, imp)
    imp = jnp.where(jr * SEL_BLOCK <= tpos, imp, NEG_INF)
    imp = jnp.where(jr < n_sb, imp, BIG_NEG)
    sel = jnp.zeros((nj, tq), jnp.float32)
    for _ in range(N_SEL):
        m = jnp.max(imp, axis=0, keepdims=True)
        first = jnp.min(jnp.where(imp == m, jf, 1e9), axis=0, keepdims=True)
        hit = jf == first
        sel = jnp.where(hit & (m > 0.5 * NEG_INF), 1.0, sel)
        imp = jnp.where(hit, BIG_NEG, imp)
    return sel


def _nsa_prompt_kernel(q_ref, kcmp_ref, vcmpt_ref, ks_ref, vst_ref, kw_ref, vwt_ref, glt_ref, ovt_ref, et_ref, o_ref,
                       *, tq, n_cmp, n_sb):
    i = pl.program_id(2)
    t0 = i * tq
    bf = jnp.bfloat16
    qf = q_ref[0]
    q4 = jnp.concatenate([qf[:, g * HEAD_DIM:(g + 1) * HEAD_DIM] for g in range(GROUP)], axis=0)
    lanes4 = lambda x: jnp.concatenate([x] * GROUP, axis=1)
    tq_pos = t0 + lax.broadcasted_iota(jnp.int32, (1, tq), 1)
    tpos = lanes4(tq_pos)

    n_wc = WINDOW // tq + 1
    c0 = jnp.maximum(i - WINDOW // tq, 0)
    w0 = pl.multiple_of(c0 * tq, tq)
    d = tpos - (w0 + lax.broadcasted_iota(jnp.int32, (n_wc * tq, 1), 0))
    sw = lax.dot_general(kw_ref[0, 0, pl.ds(w0, n_wc * tq), :], q4, _NT, preferred_element_type=jnp.float32)
    sw = jnp.where((d >= 0) & (d < WINDOW), sw, NEG_INF)
    ew = jnp.exp2(sw - jnp.max(sw, axis=0, keepdims=True))
    vwt = jnp.concatenate([vwt_ref[0, 0, c0 + r] for r in range(n_wc)], axis=1)
    o_win = (jnp.dot(vwt, ew.astype(bf), preferred_element_type=jnp.float32)
             / jnp.maximum(jnp.sum(ew, axis=0, keepdims=True), 1e-30))

    nr = lax.broadcasted_iota(jnp.int32, (128, 1), 0)
    maskc = ((nr * CMP_STRIDE + (CMP_BLOCK - 1)) <= tpos) & (nr < n_cmp)
    sc = jnp.where(maskc, lax.dot_general(kcmp_ref[0, 0], q4, _NT, preferred_element_type=jnp.float32), NEG_INF)
    ec = jnp.where(maskc, jnp.exp2(sc - jnp.max(sc, axis=0, keepdims=True)), 0.0)
    pc = ec / jnp.maximum(jnp.sum(ec, axis=0, keepdims=True), 1e-30)
    o_cmp = jnp.dot(vcmpt_ref[0, 0], pc.astype(bf), preferred_element_type=jnp.float32)
    psum = pc[:, :tq] + pc[:, tq:2 * tq] + pc[:, 2 * tq:3 * tq] + pc[:, 3 * tq:]
    p_hi = psum.astype(bf)
    p_lo = (psum - p_hi.astype(jnp.float32)).astype(bf)
    imp = (jnp.dot(ovt_ref[...], p_hi, preferred_element_type=jnp.float32)
           + jnp.dot(ovt_ref[...], p_lo, preferred_element_type=jnp.float32))
    nj = -(-n_sb // 8) * 8
    sel = _select_blocks_t(imp[:nj], tq_pos, n_sb)
    sel = jnp.concatenate([sel, jnp.zeros((128 - nj, tq), jnp.float32)], axis=0).astype(bf)

    per_kc = SEL_KC // tq

    def sel_step(c, carry):
        m, l, acc = carry
        k0 = pl.multiple_of(c * SEL_KC, SEL_KC)
        kpos = k0 + lax.broadcasted_iota(jnp.int32, (SEL_KC, 1), 0)
        picked = jnp.dot(et_ref[pl.ds(k0, SEL_KC), :], sel, preferred_element_type=jnp.float32)
        mask = lanes4((picked > 0.5) & (kpos <= tq_pos))
        s = lax.dot_general(ks_ref[0, 0, pl.ds(k0, SEL_KC), :], q4, _NT, preferred_element_type=jnp.float32)
        s = jnp.where(mask, s, NEG_INF)
        m_new = jnp.maximum(m, jnp.max(s, axis=0, keepdims=True))
        alpha = jnp.exp2(m - m_new)
        p = jnp.exp2(s - m_new)
        l = alpha * l + jnp.sum(p, axis=0, keepdims=True)
        vt = jnp.concatenate([vst_ref[0, 0, c * per_kc + r] for r in range(per_kc)], axis=1)
        acc = alpha * acc + jnp.dot(vt, p.astype(bf), preferred_element_type=jnp.float32)
        return m_new, l, acc

    nq = GROUP * tq
    init = (jnp.full((1, nq), NEG_INF, jnp.float32), jnp.zeros((1, nq), jnp.float32),
            jnp.zeros((HEAD_DIM, nq), jnp.float32))
    n_kc = (t0 + tq + SEL_KC - 1) // SEL_KC
    _, l_s, acc_s = lax.fori_loop(0, n_kc, sel_step, init)
    o_sel = acc_s / jnp.maximum(l_s, 1e-30)

    gate = jax.nn.sigmoid(glt_ref[0, 0])
    for g in range(GROUP):
        sl = slice(g * tq, (g + 1) * tq)
        ot = (gate[3 * g:3 * g + 1] * o_cmp[:, sl] + gate[3 * g + 1:3 * g + 2] * o_sel[:, sl]
              + gate[3 * g + 2:3 * g + 3] * o_win[:, sl])
        o_ref[0, :, g * HEAD_DIM:(g + 1) * HEAD_DIM] = ot.T.astype(o_ref.dtype)


def _overlap_matrix(n_cmp, n_sb):
    start = np.arange(128) * CMP_STRIDE
    end = start + CMP_BLOCK - 1
    sb = np.arange(128) * SEL_BLOCK
    ov = (start[:, None] < sb[None, :] + SEL_BLOCK) & (end[:, None] >= sb[None, :])
    ov &= (np.arange(128)[:, None] < n_cmp) & (np.arange(128)[None, :] < n_sb)
    return jnp.asarray(ov, jnp.bfloat16)


def _heads_major(x):
    B, T, W = x.shape
    return x.reshape(B, T, N_KV_HEADS, W // N_KV_HEADS).transpose(0, 2, 1, 3)


def _nsa_prompt_pallas(q, kcmp, vcmp, ks, vs_t, kw, vw_t, gl_t):
    B, T, _ = q.shape
    tq = ATT_TQ
    n_cmp = (T - CMP_BLOCK) // CMP_STRIDE + 1
    n_sb = -(-T // SEL_BLOCK)
    assert T % SEL_KC == 0 and T >= WINDOW + tq and kcmp.shape[1] <= 128 and n_sb <= 128
    kcmp, vcmp = (jnp.pad(a, ((0, 0), (0, 128 - a.shape[1]), (0, 0))) for a in (kcmp, vcmp))
    vcmp_t = vcmp.reshape(B, 128, N_KV_HEADS, HEAD_DIM).transpose(0, 2, 3, 1)
    et = jnp.asarray((np.arange(T)[:, None] // SEL_BLOCK) == np.arange(128)[None, :], jnp.bfloat16)
    k_spec = pl.BlockSpec((1, 1, T, HEAD_DIM), lambda b, k, i: (b, k, 0, 0))
    v_spec = pl.BlockSpec((1, 1, T // tq, HEAD_DIM, tq), lambda b, k, i: (b, k, 0, 0, 0))
    return pl.pallas_call(
        functools.partial(_nsa_prompt_kernel, tq=tq, n_cmp=n_cmp, n_sb=n_sb),
        out_shape=jax.ShapeDtypeStruct((B, T, Q_W), jnp.bfloat16),
        grid=(B, N_KV_HEADS, T // tq),
        in_specs=[pl.BlockSpec((1, tq, GROUP * HEAD_DIM), lambda b, k, i: (b, i, k)),
                  pl.BlockSpec((1, 1, 128, HEAD_DIM), lambda b, k, i: (b, k, 0, 0)),
                  pl.BlockSpec((1, 1, HEAD_DIM, 128), lambda b, k, i: (b, k, 0, 0)),
                  k_spec, v_spec, k_spec, v_spec,
                  pl.BlockSpec((1, 1, 3 * GROUP, tq), lambda b, k, i: (b, k, 0, i)),
                  pl.BlockSpec((128, 128), lambda b, k, i: (0, 0)),
                  pl.BlockSpec((T, 128), lambda b, k, i: (0, 0))],
        out_specs=pl.BlockSpec((1, tq, GROUP * HEAD_DIM), lambda b, k, i: (b, i, k)),
        compiler_params=pltpu.CompilerParams(
            dimension_semantics=("parallel", "parallel", "arbitrary"), vmem_limit_bytes=VMEM_LIMIT_BYTES),
        name="nsa_prompt",
    )(q, _heads_major(kcmp), vcmp_t, ks, vs_t, kw, vw_t, gl_t, _overlap_matrix(n_cmp, n_sb).T, et)


_NT = (((1,), (1,)), ((), ()))
SAMPLE_NB = 2


def _decode_attend(s, mask, s_new, mask_new, pv_fn, v_new):
    sm = jnp.where(mask, s, NEG_INF)
    sn = jnp.where(mask_new, s_new, NEG_INF)
    m = jnp.maximum(jnp.max(sm, axis=1, keepdims=True), sn)
    e = jnp.where(mask, jnp.exp2(sm - m), 0.0)
    en = jnp.where(mask_new, jnp.exp2(sn - m), 0.0)
    l = jnp.sum(e, axis=1, keepdims=True) + en
    acc = pv_fn(e.astype(jnp.bfloat16)) + (en.astype(jnp.bfloat16).astype(jnp.float32)
                                            * v_new.astype(jnp.bfloat16).astype(jnp.float32))
    return acc / jnp.maximum(l, 1e-30)


def _nsa_sample_kernel(pt_ref, *refs, n_pages, n_sb, nb, n_const):
    del pt_ref
    cmp_pages = refs[:nb * n_pages]
    sel_pages_all = refs[nb * n_pages:2 * nb * n_pages]
    rest = refs[2 * nb * n_pages:]
    win_ref, q_ref, gl_ref, ksn_ref, vsn_ref, kwn_ref, vwn_ref, kwc_ref, vwc_ref = rest[:9]
    (wk_ref, wv_ref, pek_ref, pev_ref, w1k_ref, w1v_ref, w2k_ref, w2v_ref, cos_ref, sin_ref, ov_ref, ex_ref,
     perm_ref) = rest[9:9 + n_const]
    o_ref, wout_ref = rest[-2:]
    past = n_pages * PAGE_SIZE
    n_half = past // CMP_STRIDE
    n_cmp = (past + 1 - CMP_BLOCK) // CMP_STRIDE + 1

    groups = PAGE_SIZE // CMP_STRIDE

    k_tiles, v_tiles = [], []
    for p in range(nb * n_pages):
        page = cmp_pages[p][0, 0].reshape(2 * KV_W, PAGE_SIZE).astype(jnp.bfloat16)
        z = lax.dot_general(perm_ref[...], page, _NT, preferred_element_type=jnp.float32)
        for tiles, z_kv in ((k_tiles, z[:, :KV_W]), (v_tiles, z[:, KV_W:])):
            tiles.append(jnp.concatenate([z_kv[j * groups:(j + 1) * groups] for j in range(CMP_STRIDE)], axis=1))
    k_cat = jnp.concatenate(k_tiles, axis=0).astype(jnp.bfloat16)
    v_cat = jnp.concatenate(v_tiles, axis=0).astype(jnp.bfloat16)

    rows = nb * n_half
    valid = (lax.broadcasted_iota(jnp.int32, (rows, KV_W), 0) % n_half) < n_cmp
    kcmp = _compress_rows(k_cat, wk_ref, pek_ref, w1k_ref, w2k_ref, rows)
    kcmp = jnp.where(valid, _rope_lanes(kcmp, cos_ref[...], sin_ref[...]), 0.0).astype(jnp.bfloat16)
    vcmp = _compress_rows(v_cat, wv_ref, pev_ref, w1v_ref, w2v_ref, rows)
    vcmp = jnp.where(valid, vcmp, 0.0).astype(jnp.bfloat16)
    for s in range(nb):
        _nsa_sample_one(s, kcmp[s * n_half:(s + 1) * n_half], vcmp[s * n_half:(s + 1) * n_half],
                        sel_pages_all[s * n_pages:(s + 1) * n_pages], win_ref, q_ref, gl_ref, ksn_ref, vsn_ref,
                        kwn_ref, vwn_ref, kwc_ref, vwc_ref, ov_ref, ex_ref, o_ref, wout_ref, n_pages, n_sb)


def _nsa_sample_one(s, kcmp, vcmp, sel_pages, win_ref, q_ref, gl_ref, ksn_ref, vsn_ref, kwn_ref, vwn_ref,
                    kwc_ref, vwc_ref, ov_ref, ex_ref, o_ref, wout_ref, n_pages, n_sb):
    past = n_pages * PAGE_SIZE
    qpos = past
    n_half = past // CMP_STRIDE
    n_cmp = (past + 1 - CMP_BLOCK) // CMP_STRIDE + 1
    wb = win_ref.shape[-1]
    bf = jnp.bfloat16

    row8 = lax.broadcasted_iota(jnp.int32, (8, KV_W), 0)
    lane8 = lax.broadcasted_iota(jnp.int32, (8, KV_W), 1)
    top1 = lax.broadcasted_iota(jnp.int32, (8, 1), 0) < GROUP
    q8 = q_ref[s].astype(jnp.float32)
    q2 = jnp.where((row8 < GROUP) == (lane8 < HEAD_DIM), jnp.concatenate([q8, q8], axis=1), 0.0).astype(bf)
    q2f = q2.astype(jnp.float32)

    def halves(x):
        return jnp.where(top1, x[:, :HEAD_DIM], x[:, HEAD_DIM:])

    def new_score(k_new):
        return jnp.sum(q2f * k_new.astype(bf).astype(jnp.float32), axis=1, keepdims=True)

    sc = lax.dot_general(q2, kcmp, _NT, preferred_element_type=jnp.float32)
    nl = lax.broadcasted_iota(jnp.int32, (8, n_half), 1)
    pc = _softmax_rows(sc, ((nl * CMP_STRIDE + (CMP_BLOCK - 1)) <= qpos) & (nl < n_cmp))
    o_cmp = halves(jnp.dot(pc.astype(bf), vcmp, preferred_element_type=jnp.float32))
    pk0 = jnp.sum(jnp.where(top1, pc, 0.0), axis=0, keepdims=True)
    pk1 = jnp.sum(jnp.where(top1, 0.0, pc), axis=0, keepdims=True)
    rown = lax.broadcasted_iota(jnp.int32, (8, n_half), 0)
    p2 = jnp.where(rown == 0, pk0, jnp.where(rown == 1, pk1, 0.0))
    p_hi = p2.astype(bf)
    p_lo = (p2 - p_hi.astype(jnp.float32)).astype(bf)
    imp = (jnp.dot(p_hi, ov_ref[...], preferred_element_type=jnp.float32)
           + jnp.dot(p_lo, ov_ref[...], preferred_element_type=jnp.float32))
    sel2 = _select_blocks(imp, jnp.full((8, 1), qpos, jnp.int32), n_sb)

    picked2 = jnp.dot(sel2.astype(bf), ex_ref[...], preferred_element_type=jnp.float32)
    mask_s = jnp.where(top1, picked2[0:1], picked2[1:2]) > 0.5
    seln = jnp.sum(jnp.where(lane8 == qpos // SEL_BLOCK, sel2, 0.0), axis=1, keepdims=True)
    mask_new = jnp.where(top1, seln[0:1], seln[1:2]) > 0.5
    kt = jnp.concatenate([sel_pages[p][0, 0, 0].astype(bf) for p in range(n_pages)], axis=1)
    vt = jnp.concatenate([sel_pages[p][0, 0, 1].astype(bf) for p in range(n_pages)], axis=1)
    s_s = jnp.dot(q2, kt, preferred_element_type=jnp.float32)
    pv_sel = lambda e: lax.dot_general(e, vt, _NT, preferred_element_type=jnp.float32)

    o_sel = halves(_decode_attend(s_s, mask_s, new_score(ksn_ref[s]), mask_new, pv_sel, vsn_ref[s]))

    s_w = jnp.dot(q2, win_ref[0, s, 0].astype(bf), preferred_element_type=jnp.float32)
    kpos = past - wb + lax.broadcasted_iota(jnp.int32, (8, wb), 1)
    mask_w = (qpos - kpos >= 0) & (qpos - kpos < WINDOW) & (kpos >= 0)
    pv_win = lambda e: lax.dot_general(e, win_ref[0, s, 1].astype(bf), _NT, preferred_element_type=jnp.float32)
    o_win = halves(_decode_attend(s_w, mask_w, new_score(kwn_ref[s]), jnp.full((8, 1), True), pv_win, vwn_ref[s]))

    gate = jax.nn.sigmoid(gl_ref[s])
    o_ref[s] = (gate[:, 0:1] * o_cmp + gate[:, 1:2] * o_sel + gate[:, 2:3] * o_win).astype(o_ref.dtype)

    lane_w = lax.broadcasted_iota(jnp.int32, (KV_W, wb), 1)
    wout_ref[s, 0] = jnp.where(lane_w == wb - 1, kwc_ref[s], pltpu.roll(win_ref[0, s, 0], wb - 1, 1))
    wout_ref[s, 1] = jnp.where(lane_w == wb - 1, vwc_ref[s], pltpu.roll(win_ref[0, s, 1], wb - 1, 1))


def _cache_rows_on_lanes(c):
    nd = c.ndim
    c = jnp.moveaxis(c, nd - 4, nd - 1)
    return c.reshape(c.shape[:-3] + (c.shape[-3] * c.shape[-2], c.shape[-1]))


def _nsa_sample_pallas(layer, q, gl, ks, vs, kw, vw, cmpw, cmp_t, sel_t, win_t, page_table, win_out):
    B = q.shape[0]
    n_pages = page_table.shape[1]
    past = n_pages * PAGE_SIZE
    wb = win_t.shape[-1]
    n_half = past // CMP_STRIDE
    n_cmp = (past + 1 - CMP_BLOCK) // CMP_STRIDE + 1
    n_sb = -(-(past + 1) // SEL_BLOCK)
    wk1, pek, wk2, wv1, pev, wv2 = cmpw
    wk, pekf, w1k, w2k = _compress_weights(wk1, pek, wk2)
    wv, pevf, w1v, w2v = _compress_weights(wv1, pev, wv2)
    nb = _pick_tile(B, (SAMPLE_NB,))
    nb = nb if nb == SAMPLE_NB else 1
    cos, sin = _rope_tables(jnp.tile(jnp.arange(n_half) * CMP_STRIDE + CMP_BLOCK - 1, nb), N_KV_HEADS)
    ov = _overlap_matrix(n_cmp, n_sb)[:n_half]
    ex = jnp.asarray((np.arange(past)[None, :] // SEL_BLOCK) == np.arange(128)[:, None], jnp.bfloat16)
    groups = PAGE_SIZE // CMP_STRIDE
    src_row = (np.arange(PAGE_SIZE) % groups) * CMP_STRIDE + np.arange(PAGE_SIZE) // groups
    perm = jnp.asarray(src_row[:, None] == np.arange(PAGE_SIZE)[None, :], jnp.bfloat16)
    consts = (wk, wv, pekf, pevf, w1k, w1v, w2k, w2v, cos, sin, ov, ex, perm)
    row3 = lambda x: x.reshape(B, 1, KV_W)
    col3 = lambda x: x.reshape(B, KV_W, 1)
    per_b = (q.reshape(B, N_HEADS, HEAD_DIM), gl.reshape(B, N_HEADS, 3), row3(ks), row3(vs), row3(kw), row3(vw),
             col3(kw), col3(vw))
    page_spec = lambda s, p: pl.BlockSpec((1, 1, 2, KV_W, PAGE_SIZE),
                                          lambda b, pt: (layer, pt[b * nb + s, p], 0, 0, 0))
    b_spec = lambda a: pl.BlockSpec((nb,) + a.shape[1:], lambda b, pt: (b,) + (0,) * (a.ndim - 1))
    full = lambda a: pl.BlockSpec(a.shape, lambda b, pt: (0,) * a.ndim)
    in_specs = ([page_spec(s, p) for s in range(nb) for p in range(n_pages)] * 2
                + [pl.BlockSpec((1, nb, 2, KV_W, wb), lambda b, pt: (layer, b, 0, 0, 0))]
                + [b_spec(a) for a in per_b] + [full(a) for a in consts])
    args = (page_table, *([cmp_t] * (nb * n_pages)), *([sel_t] * (nb * n_pages)), win_t, *per_b, *consts)
    in_specs.append(pl.BlockSpec(memory_space=pl.ANY))
    aliases = {len(args): 1}
    args += (win_out,)
    return pl.pallas_call(
        functools.partial(_nsa_sample_kernel, n_pages=n_pages, n_sb=n_sb, nb=nb, n_const=len(consts)),
        out_shape=(jax.ShapeDtypeStruct((B, N_HEADS, HEAD_DIM), jnp.bfloat16),
                   jax.ShapeDtypeStruct(win_t.shape, jnp.float32)),
        grid_spec=pltpu.PrefetchScalarGridSpec(
            num_scalar_prefetch=1, grid=(B // nb,), in_specs=in_specs,
            out_specs=(pl.BlockSpec((nb, N_HEADS, HEAD_DIM), lambda b, pt: (b, 0, 0)),
                       pl.BlockSpec((None, nb, 2, KV_W, wb), lambda b, pt: (layer, b, 0, 0, 0)))),
        input_output_aliases=aliases,
        compiler_params=pltpu.CompilerParams(dimension_semantics=("arbitrary",), vmem_limit_bytes=VMEM_LIMIT_BYTES),
        name="nsa_sample",
    )(*args)


SSM_N = SSM_GROUPS * SSM_STATE
SSM_LANE_BLK = 512
SSM_TL = 64


def _ssm_kernel(u_ref, h0r_ref, h0i_ref, ar_ref, ai_ref, bm_ref, cr_ref, ci_ref, d_ref,
                y_ref, hr_ref, hi_ref, xr_s, xi_s, *, tl, nb):
    c = pl.program_id(0)

    @pl.when(c == 0)
    def _():
        hr_ref[...] = h0r_ref[...]
        hi_ref[...] = h0i_ref[...]

    u = u_ref[...]
    ub = u.astype(jnp.bfloat16)
    n_grp = SSM_WIDTH // 128
    for j in range(n_grp):
        bu = jnp.dot(ub[:, 128 * j:128 * (j + 1)], bm_ref[j], preferred_element_type=jnp.float32)
        xr_s[:, 512 * j:512 * (j + 1)] = bu[:, :512]
        xi_s[:, 512 * j:512 * (j + 1)] = bu[:, 512:]

    for lb in range(SSM_N // SSM_LANE_BLK):
        sl = slice(lb * SSM_LANE_BLK, (lb + 1) * SSM_LANE_BLK)
        ar = jnp.broadcast_to(ar_ref[:, sl], (8, SSM_LANE_BLK))
        ai = jnp.broadcast_to(ai_ref[:, sl], (8, SSM_LANE_BLK))
        for r in range(nb // 8):
            def step(t, carry):
                hr, hi = carry
                row = pl.multiple_of(t * nb + r * 8, 8)
                xr = xr_s[pl.ds(row, 8), sl]
                xi = xi_s[pl.ds(row, 8), sl]
                nr = ar * hr - ai * hi + xr
                ni = ar * hi + ai * hr + xi
                xr_s[pl.ds(row, 8), sl] = nr
                xi_s[pl.ds(row, 8), sl] = ni
                return nr, ni

            hr, hi = lax.fori_loop(0, tl, step, (hr_ref[r * 8:(r + 1) * 8, sl], hi_ref[r * 8:(r + 1) * 8, sl]))
            hr_ref[r * 8:(r + 1) * 8, sl] = hr
            hi_ref[r * 8:(r + 1) * 8, sl] = hi

    for j in range(n_grp):
        yr = jnp.dot(xr_s[:, 512 * j:512 * (j + 1)].astype(jnp.bfloat16), cr_ref[j], preferred_element_type=jnp.float32)
        yi = jnp.dot(xi_s[:, 512 * j:512 * (j + 1)].astype(jnp.bfloat16), ci_ref[j], preferred_element_type=jnp.float32)
        y_ref[:, 128 * j:128 * (j + 1)] = yr - yi + d_ref[:, 128 * j:128 * (j + 1)] * u[:, 128 * j:128 * (j + 1)]


def _ssm_params(a_re, a_im, log_dt, b_re, b_im, c_re, c_im, d_skip):
    dt = jnp.exp(log_dt)[:, None]
    mag = jnp.exp(dt * a_re)
    ab_re = mag * jnp.cos(dt * a_im)
    ab_im = mag * jnp.sin(dt * a_im)
    den = a_re * a_re + a_im * a_im
    zr = ((ab_re - 1.0) * a_re + ab_im * a_im) / den
    zi = (ab_im * a_re - (ab_re - 1.0) * a_im) / den
    bb_re = zr[..., None] * b_re - zi[..., None] * b_im
    bb_im = zr[..., None] * b_im + zi[..., None] * b_re
    n_grp = SSM_WIDTH // 128
    gpl = 128 // SSM_GROUP
    eye = jnp.eye(gpl, dtype=jnp.float32)

    def b_blocks(bb):
        x = bb.reshape(n_grp, gpl, SSM_STATE, SSM_GROUP)
        return jnp.einsum('jgpc,gh->jgchp', x, eye).reshape(n_grp, 128, gpl * SSM_STATE)

    def c_blocks(cc):
        x = cc.reshape(n_grp, gpl, SSM_GROUP, SSM_STATE)
        return jnp.einsum('jgcp,gh->jgphc', x, eye).reshape(n_grp, gpl * SSM_STATE, 128)

    bm = jnp.concatenate([b_blocks(bb_re), b_blocks(bb_im)], axis=-1).astype(jnp.bfloat16)
    return (ab_re.reshape(1, SSM_N), ab_im.reshape(1, SSM_N), bm,
            c_blocks(c_re).astype(jnp.bfloat16), c_blocks(c_im).astype(jnp.bfloat16), d_skip.reshape(1, SSM_WIDTH))


def _ssm_pallas(u_tb, B, T, h0_re, h0_im, params):
    ab_re, ab_im, bm, cr, ci, d = params
    tl = _pick_tile(T, (SSM_TL,))
    full = lambda a: pl.BlockSpec(a.shape, lambda c: (0,) * a.ndim)
    h0r = h0_re.reshape(B, SSM_N)
    h0i = h0_im.reshape(B, SSM_N)
    consts = (h0r, h0i, ab_re, ab_im, bm, cr, ci, d)
    y, hr, hi = pl.pallas_call(
        functools.partial(_ssm_kernel, tl=tl, nb=B),
        out_shape=(jax.ShapeDtypeStruct((T * B, SSM_WIDTH), jnp.float32),
                   jax.ShapeDtypeStruct((B, SSM_N), jnp.float32), jax.ShapeDtypeStruct((B, SSM_N), jnp.float32)),
        grid=(T // tl,),
        in_specs=[pl.BlockSpec((tl * B, SSM_WIDTH), lambda c: (c, 0))] + [full(a) for a in consts],
        out_specs=(pl.BlockSpec((tl * B, SSM_WIDTH), lambda c: (c, 0)),
                   pl.BlockSpec((B, SSM_N), lambda c: (0, 0)), pl.BlockSpec((B, SSM_N), lambda c: (0, 0))),
        scratch_shapes=[pltpu.VMEM((tl * B, SSM_N), jnp.float32), pltpu.VMEM((tl * B, SSM_N), jnp.float32)],
        compiler_params=pltpu.CompilerParams(dimension_semantics=("arbitrary",), vmem_limit_bytes=VMEM_LIMIT_BYTES),
        name="ssm",
    )(u_tb, *consts)
    return y, hr.reshape(B, SSM_GROUPS, SSM_STATE), hi.reshape(B, SSM_GROUPS, SSM_STATE)


def _block(h, p_all, pos, lw, layer, sample, h0_re, h0_im, conv_prefix):
    (g_attn, w_in, g_q, g_kc, g_ks, g_kw, wk1, pek, wk2, wv1, pev, wv2,
     a_re, a_im, log_dt, b_re, b_im, c_re, c_im, d_skip,
     w_a, w_glu1, w_glu2, w_o, g_ffn, w_up, conv_w, conv_b, w_down,
     g_ple, w_ple_gate, w_ple) = lw
    B, T = p_all.shape[1:3]
    N = B * T
    qs, rows_cmp, rows_sel, rows_win, gl, u, gab, *attn = _in_proj_pallas(h, w_in, layer, g_attn, g_q, g_kc, g_ks,
                                                                           g_kw, pos, T)
    gl = gl[:, :3 * N_HEADS]
    cmpw = (wk1, pek, wk2, wv1, pev, wv2)
    as_rows = lambda a: a.reshape(B, T, 2, N_KV_HEADS, HEAD_DIM)
    if sample is None:
        kcmp, vcmp = _compress_pallas(rows_cmp.reshape(B, T, 2 * KV_W), cmpw)
        o = _nsa_prompt_pallas(qs.reshape(B, T, Q_W), kcmp, vcmp, *attn).reshape(N, Q_W)
        n_keep = min(WINDOW, T)
        rows = (as_rows(rows_cmp), as_rows(rows_sel), as_rows(rows_win)[:, T - n_keep:])
    else:
        assert T == 1
        cmp_t, sel_t, win_t, page_table, win_out = sample
        o, win_out = _nsa_sample_pallas(layer, qs, gl, rows_sel[:, :KV_W], rows_sel[:, KV_W:], rows_win[:, :KV_W],
                                        rows_win[:, KV_W:], cmpw, cmp_t, sel_t, win_t, page_table, win_out)
        o = o.reshape(N, Q_W)
        rows = (as_rows(rows_cmp), as_rows(rows_sel), win_out)
    y, hr, hi = _ssm_pallas(u.reshape(T * B, SSM_WIDTH), B, T, h0_re, h0_im,
                            _ssm_params(a_re, a_im, log_dt, b_re, b_im, c_re, c_im, d_skip))
    h, conv_rows = _mix_ffn_pallas(h, o, y.reshape(u.shape), gab, p_all.reshape(p_all.shape[0], N, -1), conv_prefix,
                                   (w_a, w_glu1, w_glu2, w_o, g_ffn, w_up, conv_w, conv_b, w_down,
                                    g_ple, w_ple_gate, w_ple), layer, T)
    return h, rows, hr, hi, conv_rows


def kernel(x_prompt, x_sample, cache_cmp, cache_sel, cache_win, state_ssm_re, state_ssm_im, state_conv, page_table, p_prompt, p_sample, g_attn, w_in, g_q, g_kc, g_ks, g_kw, cmp_wk1, cmp_pek, cmp_wk2, cmp_wv1, cmp_pev, cmp_wv2, ssm_a_re, ssm_a_im, ssm_log_dt, ssm_b_re, ssm_b_im, ssm_c_re, ssm_c_im, ssm_d, w_a, w_glu1, w_glu2, w_o, g_ffn, w_up, conv_w, conv_b, w_down, g_ple, w_ple_gate, w_ple):
    Bp, Tp = x_prompt.shape[:2]
    Ts = x_sample.shape[1]
    depth = w_in.shape[0]
    past = page_table.shape[1] * PAGE_SIZE
    pos_p = jnp.arange(Tp)
    pos_s = past + jnp.arange(Ts)
    zeros_h = jnp.zeros((Bp, SSM_GROUPS, SSM_STATE), x_prompt.dtype)
    whole = lambda w: (w.astype(jnp.bfloat16),)
    layer_w = (g_attn, (_pad_w_in(w_in),), g_q, g_kc, g_ks, g_kw, cmp_wk1, cmp_pek, cmp_wk2, cmp_wv1, cmp_pev, cmp_wv2,
               ssm_a_re, ssm_a_im, ssm_log_dt, ssm_b_re, ssm_b_im, ssm_c_re, ssm_c_im, ssm_d,
               whole(w_a), whole(w_glu1), whole(w_glu2), whole(w_o), g_ffn, whole(w_up), conv_w, conv_b, whole(w_down),
               g_ple, whole(w_ple_gate), whole(w_ple))
    cmp_t, sel_t, win_t = (_cache_rows_on_lanes(c) for c in (cache_cmp, cache_sel, cache_win))
    st = [[] for _ in range(12)]
    hp, hs = x_prompt.reshape(Bp * Tp, D_MODEL), x_sample.reshape(-1, D_MODEL)
    win_out = jnp.zeros(win_t.shape, jnp.float32)
    for i in range(depth):
        lw = [w[0] if isinstance(w, tuple) else w[i] for w in layer_w]
        hp, rows, hr, hi, cv = _block(hp, p_prompt, pos_p, lw, i, None, zeros_h, zeros_h, None)
        for j, a in enumerate(list(rows) + [hr, hi, cv]):
            st[j].append(a)
        hs, rows, hr, hi, cv = _block(hs, p_sample, pos_s, lw, i, (cmp_t, sel_t, win_t, page_table, win_out),
                                      state_ssm_re[i], state_ssm_im[i], state_conv[i])
        win_out = rows[2]
        for j, a in enumerate(list(rows) + [hr, hi, cv]):
            st[6 + j].append(a)
    outs = [None if j == 8 else jnp.stack(s) for j, s in enumerate(st)]
    wb = win_out.shape[-1]
    outs[8] = win_out.reshape(win_out.shape[:3] + (N_KV_HEADS, HEAD_DIM, wb)).transpose(0, 1, 5, 2, 3, 4)
    return (hp.reshape(x_prompt.shape), hs.reshape(x_sample.shape)) + tuple(outs)
```
